```python
import math
import functools
import jax
import jax.numpy as jnp
from jax import lax
import numpy as np

D_MODEL = 2048
BATCH = 1
SEQ = 8192
DEPTH = 2
DEC_BATCH = 8
DEC_SEQ = 64
PAST_LEN = 2048

CHUNK = 64
Q_BLOCK = 128
ROPE_THETA = 500000.0
NORM_EPS = 1e-6
NEG_INF = -1e30
N_BRANCH = 4
BRANCH_WIDTH = 512

GM_CHUNK = 128
GM_GROUPS = 4
GM_WIDTH = 512
GM_GROUP_DIM = GM_WIDTH // GM_GROUPS

MLA_HEADS = 8
MLA_Q_LORA = 512
MLA_KV_LORA = 256
MLA_NOPE = 64
MLA_ROPE = 32
MLA_V = 64

DSA_HEADS = 8
DSA_HEAD_DIM = 64
DSA_ROT = DSA_HEAD_DIM // 4
IDX_HEADS = 8
IDX_DIM = 32
IDX_ROT = IDX_DIM // 4
IDX_SCALE = (IDX_DIM ** -0.5) * (IDX_HEADS ** -0.5)
DSA_TOPK = 256

DIFF_HEADS = 4
DIFF_QK_DIM = 64
DIFF_V_DIM = 128
DIFF_ROT = DIFF_QK_DIM // 4

FF_DIM = 5632
N_EXPERTS = 8
TOP_K = 2
EXPERT_FF = 1408
N_DENSE = (DEPTH + 1) // 2
N_MOE = DEPTH // 2

SPLIT_SIZES = (
    GM_WIDTH, GM_WIDTH,
    MLA_Q_LORA, MLA_KV_LORA, MLA_ROPE,
    DSA_HEADS * DSA_HEAD_DIM, DSA_HEADS * DSA_HEAD_DIM, DSA_HEADS * DSA_HEAD_DIM,
    IDX_HEADS * IDX_DIM, IDX_DIM, IDX_HEADS,
    DIFF_HEADS * 2 * DIFF_QK_DIM, DIFF_HEADS * 2 * DIFF_QK_DIM, DIFF_HEADS * DIFF_V_DIM,
)
IN_WIDTH = sum(SPLIT_SIZES)

kernel_name = 'hybrid_streaming_encoder_step'


def rmsnorm(x, g):
    xf = x.astype(jnp.float32)
    y = xf * lax.rsqrt(jnp.mean(xf * xf, axis=-1, keepdims=True) + NORM_EPS)
    return (y * g.astype(jnp.float32)).astype(x.dtype)


def layernorm(x, g, b):
    xf = x.astype(jnp.float32)
    mu = jnp.mean(xf, axis=-1, keepdims=True)
    var = jnp.mean(jnp.square(xf - mu), axis=-1, keepdims=True)
    y = (xf - mu) * lax.rsqrt(var + NORM_EPS) * g.astype(jnp.float32) + b.astype(jnp.float32)
    return y.astype(x.dtype)


def rope(x, pos, rot_dim):
    half = rot_dim // 2
    inv = ROPE_THETA ** (-jnp.arange(half, dtype=jnp.float32) / half)
    ang = pos.astype(jnp.float32)[:, None] * inv[None, :]
    ang = ang.reshape((ang.shape[0],) + (1,) * (x.ndim - 3) + (half,))
    cos, sin = jnp.cos(ang), jnp.sin(ang)
    xr = x[..., :rot_dim].astype(jnp.float32)
    x1, x2 = xr[..., :half], xr[..., half:]
    rot = jnp.concatenate([x1 * cos - x2 * sin, x2 * cos + x1 * sin], axis=-1).astype(x.dtype)
    return jnp.concatenate([rot, x[..., rot_dim:]], axis=-1)


def chunk_visible(q_pos, k_pos):
    return (k_pos[None, :] // CHUNK) <= (q_pos[:, None] // CHUNK)


def extend(past, new):
    return new if past is None else jnp.concatenate([past, new], axis=1)


def over_queries(fn, qs, q_pos):
    T = q_pos.shape[0]
    if T <= Q_BLOCK:
        return fn((qs, q_pos))
    nb = T // Q_BLOCK

    def split(a):
        return jnp.moveaxis(a.reshape((a.shape[0], nb, Q_BLOCK) + a.shape[2:]), 1, 0)

    out = lax.map(fn, (tuple(split(a) for a in qs), q_pos.reshape(nb, Q_BLOCK)))
    out = jnp.moveaxis(out, 0, 1)
    return out.reshape((out.shape[0], T) + out.shape[3:])


def softmax_attention(k, v, k_pos, scale):
    def fn(args):
        (q,), q_pos = args
        s = jnp.einsum('bthd,bshd->bhts', q, k).astype(jnp.float32) * scale
        s = jnp.where(chunk_visible(q_pos, k_pos), s, NEG_INF)
        p = jax.nn.softmax(s, axis=-1).astype(v.dtype)
        return jnp.einsum('bhts,bshd->bthd', p, v)
    return fn


def gmlp_branch(gu, gv, ln_g, ln_b, ws, bs):
    B, T, _ = gu.shape
    c_len = min(T, GM_CHUNK)
    u = jax.nn.gelu(gu)
    v = layernorm(jax.nn.gelu(gv), ln_g, ln_b)
    vc = v.reshape(B, T // c_len, c_len, GM_GROUPS, GM_GROUP_DIM)
    w = jnp.tril(ws[:, :c_len, :c_len])
    bias = bs[:, :c_len].T
    sp = jnp.einsum('gts,bnsgd->bntgd', w, vc) + bias[:, :, None]
    out = u * sp.reshape(B, T, GM_WIDTH)
    return out, v[:, T - c_len:]


def mla_branch(cq, ckv, krope, pos, k_pos, past, q_norm_g, kv_norm_g, w_uq, w_ukv):
    B, T, _ = cq.shape
    q = (rmsnorm(cq, q_norm_g) @ w_uq).reshape(B, T, MLA_HEADS, MLA_NOPE + MLA_ROPE)
    q = jnp.concatenate([q[..., :MLA_NOPE], rope(q[..., MLA_NOPE:], pos, MLA_ROPE)], axis=-1)
    ckv_n = rmsnorm(ckv, kv_norm_g)
    kpe = rope(krope, pos, MLA_ROPE)
    all_c = extend(past[0], ckv_n)
    all_pe = extend(past[1], kpe)
    S = all_c.shape[1]
    kv = (all_c @ w_ukv).reshape(B, S, MLA_HEADS, MLA_NOPE + MLA_V)
    k = jnp.concatenate(
        [kv[..., :MLA_NOPE], jnp.broadcast_to(all_pe[:, :, None, :], (B, S, MLA_HEADS, MLA_ROPE))], axis=-1)
    v = kv[..., MLA_NOPE:]
    o = over_queries(softmax_attention(k, v, k_pos, (MLA_NOPE + MLA_ROPE) ** -0.5), (q,), pos)
    return o.reshape(B, T, MLA_HEADS * MLA_V), ckv_n, kpe


def dsa_branch(dq, dk, dv, qi, ki, wi, pos, k_pos, past):
    B, T, _ = dq.shape
    q = rope(dq.reshape(B, T, DSA_HEADS, DSA_HEAD_DIM), pos, DSA_ROT)
    k = rope(dk.reshape(B, T, DSA_HEADS, DSA_HEAD_DIM), pos, DSA_ROT)
    v = dv.reshape(B, T, DSA_HEADS, DSA_HEAD_DIM)
    qi = rope(qi.reshape(B, T, IDX_HEADS, IDX_DIM), pos, IDX_ROT)
    ki = rope(ki, pos, IDX_ROT)
    k_all = extend(past[0], k)
    v_all = extend(past[1], v)
    ki_all = extend(past[2], ki)
    S = k_all.shape[1]
    n_sel = min(DSA_TOPK, S // 4)
    scale = DSA_HEAD_DIM ** -0.5

    def fn(args):
        (q_b, qi_b, wi_b), q_pos = args
        rel = jax.nn.relu(jnp.einsum('bthd,bsd->bths', qi_b, ki_all))
        score = jnp.einsum('bths,bth->bts', rel, wi_b).astype(jnp.float32) * IDX_SCALE
        score = jnp.where(chunk_visible(q_pos, k_pos)[None], score, NEG_INF)
        _, idx = lax.top_k(score, n_sel)
        sel_ok = (k_pos[idx] // CHUNK) <= (q_pos[None, :, None] // CHUNK)
        k_sel = jax.vmap(lambda kk, ii: kk[ii])(k_all, idx)
        v_sel = jax.vmap(lambda vv, ii: vv[ii])(v_all, idx)
        s = jnp.einsum('bthd,btnhd->bthn', q_b, k_sel).astype(jnp.float32) * scale
        s = jnp.where(sel_ok[:, :, None, :], s, NEG_INF)
        p = jax.nn.softmax(s, axis=-1).astype(v_sel.dtype)
        return jnp.einsum('bthn,btnhd->bthd', p, v_sel)

    o = over_queries(fn, (q, qi, wi), pos)
    return o.reshape(B, T, DSA_HEADS * DSA_HEAD_DIM), k, v, ki


def diff_branch(fq, fk, fv, pos, k_pos, past, lam_params, subln_g, layer_idx):
    B, T, _ = fq.shape
    q = rope(fq.reshape(B, T, DIFF_HEADS, 2, DIFF_QK_DIM), pos, DIFF_ROT)
    k = rope(fk.reshape(B, T, DIFF_HEADS, 2, DIFF_QK_DIM), pos, DIFF_ROT)
    v = fv.reshape(B, T, DIFF_HEADS, DIFF_V_DIM)
    k_all = extend(past[0], k)
    v_all = extend(past[1], v)
    lam_init = 0.8 - 0.6 * math.exp(-0.3 * layer_idx)
    lp = lam_params.astype(jnp.float32)
    lam = jnp.exp(jnp.sum(lp[0] * lp[1])) - jnp.exp(jnp.sum(lp[2] * lp[3])) + lam_init
    scale = DIFF_QK_DIM ** -0.5

    def fn(args):
        (q_b,), q_pos = args
        s = jnp.einsum('bthcd,bshcd->bhcts', q_b, k_all).astype(jnp.float32) * scale
        s = jnp.where(chunk_visible(q_pos, k_pos), s, NEG_INF)
        p = jax.nn.softmax(s, axis=-1)
        a = (p[:, :, 0] - lam * p[:, :, 1]).astype(v_all.dtype)
        o = jnp.einsum('bhts,bshv->bthv', a, v_all)
        return rmsnorm(o, subln_g) * (1.0 - lam_init)

    o = over_queries(fn, (q,), pos)
    return o.reshape(B, T, DIFF_HEADS * DIFF_V_DIM), k, v


def token_mixers(h, pos, k_pos, cache, lp, layer_idx):
    B, T, _ = h.shape
    points, acc = [], 0
    for size in SPLIT_SIZES[:-1]:
        acc += size
        points.append(acc)
    (gu, gv, cq, ckv, krope, dq, dk, dv, qi, ki, wi, fq, fk, fv) = jnp.split(h @ lp['w_in'], points, axis=-1)
    past = (None,) * 7 if cache is None else cache
    o_a, gm_v = gmlp_branch(gu, gv, lp['gm_ln_g'], lp['gm_ln_b'], lp['gm_ws'], lp['gm_bs'])
    o_b, mla_ckv, mla_kpe = mla_branch(cq, ckv, krope, pos, k_pos, past[0:2], lp['mla_q_norm_g'],
                                       lp['mla_kv_norm_g'], lp['mla_w_uq'], lp['mla_w_ukv'])
    o_c, dsa_k, dsa_v, dsa_ki = dsa_branch(dq, dk, dv, qi, ki, wi, pos, k_pos, past[2:5])
    o_d, diff_k, diff_v = diff_branch(fq, fk, fv, pos, k_pos, past[5:7], lp['diff_lambda'],
                                      lp['diff_subln_g'], layer_idx)
    branches = jnp.stack([o_a, o_b, o_c, o_d], axis=0)
    proj = jnp.einsum('nbtw,nwd->nbtd', branches, lp['w_branch'])
    gates = jax.nn.sigmoid(h @ lp['w_gate'] + lp['b_gate']).reshape(B, T, N_BRANCH, D_MODEL)
    merged = jnp.einsum('btnd,nbtd->btd', gates, proj)
    out = merged @ lp['w_out']
    return out, (gm_v, mla_ckv, mla_kpe, dsa_k, dsa_v, dsa_ki, diff_k, diff_v)


def swiglu(h, w1, w3, w2):
    return (jax.nn.silu(h @ w1) * (h @ w3)) @ w2


def moe_swiglu(h, router_w, router_b, w1, w3, w2):
    logits = (h @ router_w).astype(jnp.float32) + router_b.astype(jnp.float32)
    top_v, top_i = lax.top_k(logits, TOP_K)
    wts = jax.nn.softmax(top_v, axis=-1)
    gate = jnp.sum(jax.nn.one_hot(top_i, N_EXPERTS, dtype=jnp.float32) * wts[..., None], axis=-2).astype(h.dtype)
    y = jnp.zeros_like(h)
    for e in range(N_EXPERTS):
        y = y + gate[..., e:e + 1] * swiglu(h, w1[e], w3[e], w2[e])
    return y


def trunk_layer(x, c, pos, k_pos, cache, lp, ffn, layer_idx):
    mod = (jax.nn.silu(c) @ lp['ada_w'] + lp['ada_b'])[:, None, :]
    sh1, sc1, g1, sh2, sc2, g2 = jnp.split(mod, 6, axis=-1)
    h = rmsnorm(x, lp['norm_mix_g']) * (1 + sc1) + sh1
    m, new_state = token_mixers(h, pos, k_pos, cache, lp, layer_idx)
    x = x + g1 * m
    h = rmsnorm(x, lp['norm_ff_g']) * (1 + sc2) + sh2
    x = x + g2 * ffn(h)
    return x, new_state


def stack_layers(states, i):
    return jnp.stack([s[i] for s in states], axis=0)


def setup_inputs(seed: int = 0) -> dict:
    key = jax.random.key(seed)
    keys = iter(jax.random.split(key, 64))
    D = D_MODEL

    def nrm(shape, scale=1.0):
        return jax.random.normal(next(keys), shape, jnp.float32) * scale

    return {
        'x_prompt': nrm((BATCH, SEQ, D)),
        'x_sample': nrm((DEC_BATCH, DEC_SEQ, D)),
        'c_prompt': nrm((BATCH, D)),
        'c_sample': nrm((DEC_BATCH, D)),
        'cache_mla_ckv': nrm((DEPTH, DEC_BATCH, PAST_LEN, MLA_KV_LORA)),
        'cache_mla_krope': nrm((DEPTH, DEC_BATCH, PAST_LEN, MLA_ROPE)),
        'cache_dsa_k': nrm((DEPTH, DEC_BATCH, PAST_LEN, DSA_HEADS, DSA_HEAD_DIM)),
        'cache_dsa_v': nrm((DEPTH, DEC_BATCH, PAST_LEN, DSA_HEADS, DSA_HEAD_DIM)),
        'cache_dsa_kidx': nrm((DEPTH, DEC_BATCH, PAST_LEN, IDX_DIM)),
        'cache_diff_k': nrm((DEPTH, DEC_BATCH, PAST_LEN, DIFF_HEADS, 2, DIFF_QK_DIM)),
        'cache_diff_v': nrm((DEPTH, DEC_BATCH, PAST_LEN, DIFF_HEADS, DIFF_V_DIM)),
        'ada_w': nrm((DEPTH, D, 6 * D), 0.25 * D ** -0.5),
        'ada_b': nrm((DEPTH, 6 * D), 0.02),
        'norm_mix_g': 1.0 + nrm((DEPTH, D), 0.02),
        'norm_ff_g': 1.0 + nrm((DEPTH, D), 0.02),
        'w_in': nrm((DEPTH, D, IN_WIDTH), D ** -0.5),
        'w_gate': nrm((DEPTH, D, N_BRANCH * D), D ** -0.5),
        'b_gate': nrm((DEPTH, N_BRANCH * D), 0.02),
        'gm_ln_g': 1.0 + nrm((DEPTH, GM_WIDTH), 0.02),
        'gm_ln_b': nrm((DEPTH, GM_WIDTH), 0.02),
        'gm_ws': nrm((DEPTH, GM_GROUPS, GM_CHUNK, GM_CHUNK), GM_CHUNK ** -0.5),
        'gm_bs': 1.0 + nrm((DEPTH, GM_GROUPS, GM_CHUNK), 0.02),
        'mla_q_norm_g': 1.0 + nrm((DEPTH, MLA_Q_LORA), 0.02),
        'mla_kv_norm_g': 1.0 + nrm((DEPTH, MLA_KV_LORA), 0.02),
        'mla_w_uq': nrm((DEPTH, MLA_Q_LORA, MLA_HEADS * (MLA_NOPE + MLA_ROPE)), MLA_Q_LORA ** -0.5),
        'mla_w_ukv': nrm((DEPTH, MLA_KV_LORA, MLA_HEADS * (MLA_NOPE + MLA_V)), MLA_KV_LORA ** -0.5),
        'diff_lambda': nrm((DEPTH, 4, DIFF_QK_DIM), 0.1),
        'diff_subln_g': 1.0 + nrm((DEPTH, DIFF_V_DIM), 0.02),
        'w_branch': nrm((DEPTH, N_BRANCH, BRANCH_WIDTH, D), BRANCH_WIDTH ** -0.5),
        'w_out': nrm((DEPTH, D, D), D ** -0.5),
        'ffn_w1': nrm((N_DENSE, D, FF_DIM), D ** -0.5),
        'ffn_w3': nrm((N_DENSE, D, FF_DIM), D ** -0.5),
        'ffn_w2': nrm((N_DENSE, FF_DIM, D), FF_DIM ** -0.5),
        'moe_router_w': nrm((N_MOE, D, N_EXPERTS), D ** -0.5),
        'moe_router_b': nrm((N_MOE, N_EXPERTS), 0.01),
        'moe_w1': nrm((N_MOE, N_EXPERTS, D, EXPERT_FF), D ** -0.5),
        'moe_w3': nrm((N_MOE, N_EXPERTS, D, EXPERT_FF), D ** -0.5),
        'moe_w2': nrm((N_MOE, N_EXPERTS, EXPERT_FF, D), EXPERT_FF ** -0.5),
        'final_norm_g': 1.0 + nrm((D,), 0.02),
    }


def reference(x_prompt, x_sample, c_prompt, c_sample,
              cache_mla_ckv, cache_mla_krope, cache_dsa_k, cache_dsa_v, cache_dsa_kidx,
              cache_diff_k, cache_diff_v,
              ada_w, ada_b, norm_mix_g, norm_ff_g, w_in, w_gate, b_gate,
              gm_ln_g, gm_ln_b, gm_ws, gm_bs,
              mla_q_norm_g, mla_kv_norm_g, mla_w_uq, mla_w_ukv,
              diff_lambda, diff_subln_g, w_branch, w_out,
              ffn_w1, ffn_w3, ffn_w2,
              moe_router_w, moe_router_b, moe_w1, moe_w3, moe_w2,
              final_norm_g):
    t_p = x_prompt.shape[1]
    t_s = x_sample.shape[1]
    past_len = cache_mla_ckv.shape[2]
    pos_p = jnp.arange(t_p, dtype=jnp.int32)
    pos_s = past_len + jnp.arange(t_s, dtype=jnp.int32)
    kpos_s = jnp.arange(past_len + t_s, dtype=jnp.int32)

    xp, xs = x_prompt, x_sample
    states_p, states_s = [], []
    for l in range(DEPTH):
        lp = {
            'ada_w': ada_w[l], 'ada_b': ada_b[l],
            'norm_mix_g': norm_mix_g[l], 'norm_ff_g': norm_ff_g[l],
            'w_in': w_in[l], 'w_gate': w_gate[l], 'b_gate': b_gate[l],
            'gm_ln_g': gm_ln_g[l], 'gm_ln_b': gm_ln_b[l], 'gm_ws': gm_ws[l], 'gm_bs': gm_bs[l],
            'mla_q_norm_g': mla_q_norm_g[l], 'mla_kv_norm_g': mla_kv_norm_g[l],
            'mla_w_uq': mla_w_uq[l], 'mla_w_ukv': mla_w_ukv[l],
            'diff_lambda': diff_lambda[l], 'diff_subln_g': diff_subln_g[l],
            'w_branch': w_branch[l], 'w_out': w_out[l],
        }
        j = l // 2
        if l % 2 == 0:
            ffn = functools.partial(swiglu, w1=ffn_w1[j], w3=ffn_w3[j], w2=ffn_w2[j])
        else:
            ffn = functools.partial(moe_swiglu, router_w=moe_router_w[j], router_b=moe_router_b[j],
                                    w1=moe_w1[j], w3=moe_w3[j], w2=moe_w2[j])
        xp, st_p = trunk_layer(xp, c_prompt, pos_p, pos_p, None, lp, ffn, l)
        cache_l = (cache_mla_ckv[l], cache_mla_krope[l], cache_dsa_k[l], cache_dsa_v[l],
                   cache_dsa_kidx[l], cache_diff_k[l], cache_diff_v[l])
        xs, st_s = trunk_layer(xs, c_sample, pos_s, kpos_s, cache_l, lp, ffn, l)
        states_p.append(st_p)
        states_s.append(st_s)

    y_prompt = rmsnorm(xp, final_norm_g)
    y_sample = rmsnorm(xs, final_norm_g)
    gm_v_p, gm_v_s = stack_layers(states_p, 0), stack_layers(states_s, 0)
    mla_ckv_p, mla_ckv_s = stack_layers(states_p, 1), stack_layers(states_s, 1)
    mla_krope_p, mla_krope_s = stack_layers(states_p, 2), stack_layers(states_s, 2)
    dsa_k_p, dsa_k_s = stack_layers(states_p, 3), stack_layers(states_s, 3)
    dsa_v_p, dsa_v_s = stack_layers(states_p, 4), stack_layers(states_s, 4)
    dsa_kidx_p, dsa_kidx_s = stack_layers(states_p, 5), stack_layers(states_s, 5)
    diff_k_p, diff_k_s = stack_layers(states_p, 6), stack_layers(states_s, 6)
    diff_v_p, diff_v_s = stack_layers(states_p, 7), stack_layers(states_s, 7)
    return (y_prompt, y_sample,
            gm_v_p, gm_v_s,
            mla_ckv_p, mla_ckv_s,
            mla_krope_p, mla_krope_s,
            dsa_k_p, dsa_k_s,
            dsa_v_p, dsa_v_s,
            dsa_kidx_p, dsa_kidx_s,
            diff_k_p, diff_k_s,
            diff_v_p, diff_v_s)
```

```python
import functools
import math

import jax
import jax.numpy as jnp
from jax import lax
from jax.experimental import pallas as pl
from jax.experimental.pallas import tpu as pltpu

F32 = jnp.float32
BF16 = jnp.bfloat16
I32 = jnp.int32

CHUNK = 64
ROPE_THETA = 500000.0
NORM_EPS = 1e-6
NEG_INF = -1e30
N_BRANCH = 4
BRANCH_WIDTH = 512
GM_CHUNK = 128
GM_GROUPS = 4
GM_WIDTH = 512
GM_GROUP_DIM = GM_WIDTH // GM_GROUPS
MLA_HEADS = 8
MLA_Q_LORA = 512
MLA_KV_LORA = 256
MLA_NOPE = 64
MLA_ROPE = 32
MLA_V = 64
DSA_HEADS = 8
DSA_HEAD_DIM = 64
DSA_ROT = DSA_HEAD_DIM // 4
IDX_HEADS = 8
IDX_DIM = 32
IDX_ROT = IDX_DIM // 4
IDX_SCALE = (IDX_DIM ** -0.5) * (IDX_HEADS ** -0.5)
DSA_TOPK = 256
DIFF_HEADS = 4
DIFF_QK_DIM = 64
DIFF_V_DIM = 128
DIFF_ROT = DIFF_QK_DIM // 4
N_EXPERTS = 8

LANE = 128
MXU_WIDTH = 256
VMEM_LIMIT_BYTES = 56 * 1024 * 1024

ROW_TILE = 512
HEAD_GROUP = MXU_WIDTH // DSA_HEAD_DIM
LOG2E = 1.4426950408889634
INT_MIN = -2 ** 31

_SPLIT = (
    ('gu', GM_WIDTH), ('gv', GM_WIDTH), ('cq', MLA_Q_LORA), ('ckv', MLA_KV_LORA), ('krope', MLA_ROPE),
    ('dq', 512), ('dk', 512), ('dv', 512), ('qi', IDX_HEADS * IDX_DIM), ('ki', IDX_DIM), ('wi', IDX_HEADS),
    ('fq', 512), ('fk', 512), ('fv', 512),
)
_ORDER = ('gu', 'gv', 'cq', 'dq', 'dk', 'dv', 'fq', 'fk', 'fv', 'ckv', 'qi', 'krope', 'ki', 'wi')


def _in_layout():
    src, acc = {}, 0
    for name, w in _SPLIT:
        src[name] = (acc, w)
        acc += w
    dst, off = {}, 0
    for name in _ORDER:
        w = src[name][1]
        pw = -(-w // LANE) * LANE
        dst[name] = (off, pw)
        off += pw
    total = -(-off // ROW_TILE) * ROW_TILE
    return src, dst, total


_SRC, _DST, IN_PAD = _in_layout()


def _cp(*sem):
    return pltpu.CompilerParams(dimension_semantics=sem, vmem_limit_bytes=VMEM_LIMIT_BYTES)


def _rms(x, g):
    return x * lax.rsqrt(jnp.mean(x * x, axis=-1, keepdims=True) + NORM_EPS) * g


def _ada_kernel(c_ref, w_ref, b_ref, o_ref):
    c = c_ref[...]
    a = (c * jax.nn.sigmoid(c)).astype(BF16)
    o_ref[...] = jnp.dot(a, w_ref[...].astype(BF16), preferred_element_type=F32) + b_ref[...]


def _ada(c_rows, ada_w, ada_b):
    L, D, N = ada_w.shape
    R = c_rows.shape[0]
    tn = 512
    return pl.pallas_call(
        _ada_kernel,
        grid=(L, N // tn),
        in_specs=[pl.BlockSpec((R, D), lambda l, j: (0, 0)),
                  pl.BlockSpec((None, D, tn), lambda l, j: (l, 0, j)),
                  pl.BlockSpec((None, 1, tn), lambda l, j: (l, 0, j))],
        out_specs=pl.BlockSpec((None, R, tn), lambda l, j: (l, 0, j)),
        out_shape=jax.ShapeDtypeStruct((L, R, N), F32),
        compiler_params=_cp("parallel", "parallel"),
        name="ada_mod",
    )(c_rows, ada_w, ada_b.reshape(L, 1, N))


def _norm_mod_kernel(x_ref, g_ref, sc_ref, sh_ref, h_ref):
    y = _rms(x_ref[...], g_ref[...])
    h_ref[...] = (y * (1.0 + sc_ref[...]) + sh_ref[...]).astype(BF16)


def _norm_mod_router_kernel(x_ref, g_ref, sc_ref, sh_ref, rwh_ref, rwl_ref, rb_ref, h_ref, gate_ref):
    y = _rms(x_ref[...], g_ref[...])
    h = y * (1.0 + sc_ref[...]) + sh_ref[...]
    h_hi = h.astype(BF16)
    h_ref[...] = h_hi
    h_lo = (h - h_hi.astype(F32)).astype(BF16)
    logits = (jnp.dot(h_hi, rwh_ref[...], preferred_element_type=F32)
              + jnp.dot(h_lo, rwh_ref[...], preferred_element_type=F32)
              + jnp.dot(h_hi, rwl_ref[...], preferred_element_type=F32)) + rb_ref[...]
    lane = lax.broadcasted_iota(I32, logits.shape, 1)
    lg = jnp.where(lane < N_EXPERTS, logits, -jnp.inf)
    m1 = jnp.max(lg, axis=-1, keepdims=True)
    i1 = jnp.min(jnp.where(lg == m1, lane, LANE), axis=-1, keepdims=True)
    lg2 = jnp.where(lane == i1, -jnp.inf, lg)
    m2 = jnp.max(lg2, axis=-1, keepdims=True)
    i2 = jnp.min(jnp.where(lg2 == m2, lane, LANE), axis=-1, keepdims=True)
    e2 = jnp.exp(m2 - m1)
    w1 = 1.0 / (1.0 + e2)
    gate_ref[...] = jnp.where(lane == i1, w1, 0.0) + jnp.where(lane == i2, e2 * w1, 0.0)


def _mod_spec(mod_map, k, D):
    return pl.BlockSpec((None, None, ROW_TILE, D), lambda i, *_: (k, mod_map(i), 0, 0))


def _norm_mod(x, g, modexp, mod_map, k_scale, k_shift, router=None):
    T, D = x.shape
    specs = [pl.BlockSpec((ROW_TILE, D), lambda i: (i, 0)),
             pl.BlockSpec((1, D), lambda i: (0, 0)),
             _mod_spec(mod_map, k_scale, D), _mod_spec(mod_map, k_shift, D)]
    args = [x, g.reshape(1, D), modexp, modexp]
    h_spec = pl.BlockSpec((ROW_TILE, D), lambda i: (i, 0))
    h_shape = jax.ShapeDtypeStruct((T, D), BF16)
    if router is None:
        return pl.pallas_call(
            _norm_mod_kernel, grid=(T // ROW_TILE,), in_specs=specs, out_specs=h_spec, out_shape=h_shape,
            compiler_params=_cp("parallel"), name="norm_mod")(*args)
    rw_hi, rw_lo, rb = router
    specs += [pl.BlockSpec((D, LANE), lambda i: (0, 0)), pl.BlockSpec((D, LANE), lambda i: (0, 0)),
              pl.BlockSpec((1, LANE), lambda i: (0, 0))]
    return pl.pallas_call(
        _norm_mod_router_kernel, grid=(T // ROW_TILE,), in_specs=specs,
        out_specs=[h_spec, pl.BlockSpec((ROW_TILE, LANE), lambda i: (i, 0))],
        out_shape=[h_shape, jax.ShapeDtypeStruct((T, LANE), F32)],
        compiler_params=_cp("parallel"), name="norm_mod_router")(*args, rw_hi, rw_lo, rb)


def _mm_kernel(a_ref, b_ref, o_ref):
    o_ref[...] = jnp.dot(a_ref[...], b_ref[...], preferred_element_type=F32).astype(o_ref.dtype)


def _mm(a, b, out_dtype, tm=ROW_TILE, tn=512, name="mm"):
    M, K = a.shape
    N = b.shape[1]
    return pl.pallas_call(
        _mm_kernel, grid=(M // tm, N // tn),
        in_specs=[pl.BlockSpec((tm, K), lambda i, j: (i, 0)), pl.BlockSpec((K, tn), lambda i, j: (0, j))],
        out_specs=pl.BlockSpec((tm, tn), lambda i, j: (i, j)),
        out_shape=jax.ShapeDtypeStruct((M, N), out_dtype),
        compiler_params=_cp("parallel", "parallel"), name=name)(a, b)


def _mm_res_kernel(a_ref, b_ref, x_ref, g_ref, o_ref, acc_ref):
    k = pl.program_id(2)

    @pl.when(k == 0)
    def _():
        acc_ref[...] = jnp.zeros_like(acc_ref)

    acc_ref[...] += jnp.dot(a_ref[...], b_ref[...], preferred_element_type=F32)

    @pl.when(k == pl.num_programs(2) - 1)
    def _():
        o_ref[...] = x_ref[...] + g_ref[...] * acc_ref[...]


def _mm_res(a, b, x, modexp, mod_map, k_gate, tk, tn=512, name="mm_res"):
    M, K = a.shape
    N = b.shape[1]
    tm = ROW_TILE
    return pl.pallas_call(
        _mm_res_kernel, grid=(M // tm, N // tn, K // tk),
        in_specs=[pl.BlockSpec((tm, tk), lambda i, j, k: (i, k)),
                  pl.BlockSpec((tk, tn), lambda i, j, k: (k, j)),
                  pl.BlockSpec((tm, tn), lambda i, j, k: (i, j)),
                  pl.BlockSpec((None, None, tm, tn), lambda i, j, k: (k_gate, mod_map(i), 0, j))],
        out_specs=pl.BlockSpec((tm, tn), lambda i, j, k: (i, j)),
        out_shape=jax.ShapeDtypeStruct((M, N), F32),
        scratch_shapes=[pltpu.VMEM((tm, tn), F32)],
        compiler_params=_cp("parallel", "parallel", "arbitrary"), name=name)(a, b, x, modexp)


def _gmlp_kernel(gu_ref, gv_ref, lng_ref, lnb_ref, w_ref, bias_ref, o_ref, v_ref, *, c):
    u = jax.nn.gelu(gu_ref[...])
    gv = jax.nn.gelu(gv_ref[...])
    mu = jnp.mean(gv, axis=-1, keepdims=True)
    var = jnp.mean(jnp.square(gv - mu), axis=-1, keepdims=True)
    v = (gv - mu) * lax.rsqrt(var + NORM_EPS) * lng_ref[...] + lnb_ref[...]
    v_ref[...] = v
    vb = v.astype(BF16)
    row = lax.broadcasted_iota(I32, (c, c), 0)
    col = lax.broadcasted_iota(I32, (c, c), 1)
    bias = bias_ref[...]
    for g in range(GM_GROUPS):
        w = jnp.where(row >= col, w_ref[g], 0.0).astype(BF16)
        lo, hi = g * GM_GROUP_DIM, (g + 1) * GM_GROUP_DIM
        for n in range(ROW_TILE // c):
            r0, r1 = n * c, (n + 1) * c
            sp = jnp.dot(w, vb[r0:r1, lo:hi], preferred_element_type=F32) + bias[:, g:g + 1]
            o_ref[r0:r1, lo:hi] = (u[r0:r1, lo:hi] * sp).astype(BF16)


def _gmlp(hw, row0, rows, c, ln_g, ln_b, ws, bs):
    t0 = row0 // ROW_TILE
    gu_blk = _DST['gu'][0] // GM_WIDTH
    gv_blk = _DST['gv'][0] // GM_WIDTH
    w = ws[:, :c, :c]
    bias = bs[:, :c].T
    return pl.pallas_call(
        functools.partial(_gmlp_kernel, c=c), grid=(rows // ROW_TILE,),
        in_specs=[pl.BlockSpec((ROW_TILE, GM_WIDTH), lambda i: (t0 + i, gu_blk)),
                  pl.BlockSpec((ROW_TILE, GM_WIDTH), lambda i: (t0 + i, gv_blk)),
                  pl.BlockSpec((1, GM_WIDTH), lambda i: (0, 0)),
                  pl.BlockSpec((1, GM_WIDTH), lambda i: (0, 0)),
                  pl.BlockSpec((GM_GROUPS, c, c), lambda i: (0, 0, 0)),
                  pl.BlockSpec((c, GM_GROUPS), lambda i: (0, 0))],
        out_specs=[pl.BlockSpec((ROW_TILE, GM_WIDTH), lambda i: (i, 0)),
                   pl.BlockSpec((ROW_TILE, GM_WIDTH), lambda i: (i, 0))],
        out_shape=[jax.ShapeDtypeStruct((rows, GM_WIDTH), BF16), jax.ShapeDtypeStruct((rows, GM_WIDTH), F32)],
        compiler_params=_cp("parallel"), name="gmlp")(
            hw, hw, ln_g.reshape(1, -1), ln_b.reshape(1, -1), w, bias)


def _rope_tables(pos, rot, period):
    half = rot // 2
    inv = ROPE_THETA ** (-jnp.arange(half, dtype=F32) / half)
    ang = pos.astype(F32)[:, None] * inv[None, :]
    cos, sin = jnp.cos(ang), jnp.sin(ang)
    T = pos.shape[0]
    pad = jnp.zeros((T, period - rot), F32)
    c = jnp.concatenate([cos, cos, pad + 1.0], axis=1)
    s1 = jnp.concatenate([-sin, jnp.zeros_like(sin), pad], axis=1)
    s2 = jnp.concatenate([jnp.zeros_like(sin), sin, pad], axis=1)
    rep = LANE // period
    return jnp.stack([jnp.tile(c, (1, rep)), jnp.tile(s1, (1, rep)), jnp.tile(s2, (1, rep))], axis=0)


def _rope(x, tab_ref, half):
    w = x.shape[1]
    rep = w // LANE

    def wide(t):
        return t if rep == 1 else jnp.concatenate([t] * rep, axis=1)

    xl = pltpu.roll(x, w - half, axis=1)
    xr = pltpu.roll(x, half, axis=1)
    return x * wide(tab_ref[0]) + xl * wide(tab_ref[1]) + xr * wide(tab_ref[2])


def _state_kernel(cq_ref, ckv_ref, kr_ref, dq_ref, dk_ref, qi_ref, ki_ref, fq_ref, fk_ref,
                  qg_ref, kvg_ref, wqn_ref, wqp_ref, t64_ref, t32_ref, tidx_ref,
                  ckv_o, kpe_o, dsak_o, ki_o, diffk_o, qn_o, qp_o, dsaq_o, qi_o, diffq_o):
    cqn = _rms(cq_ref[...], qg_ref[...]).astype(BF16)
    qn_o[...] = jnp.dot(cqn, wqn_ref[...], preferred_element_type=F32).astype(BF16)
    qp = jnp.dot(cqn, wqp_ref[...], preferred_element_type=F32)
    qp_o[...] = _rope(qp, t32_ref, MLA_ROPE // 2).astype(BF16)
    ckv_o[...] = _rms(ckv_ref[...], kvg_ref[...])
    kpe_o[...] = _rope(kr_ref[...], t32_ref, MLA_ROPE // 2)
    dsaq_o[...] = _rope(dq_ref[...], t64_ref, DSA_ROT // 2).astype(BF16)
    dsak_o[...] = _rope(dk_ref[...], t64_ref, DSA_ROT // 2)
    qi_o[...] = _rope(qi_ref[...], tidx_ref, IDX_ROT // 2).astype(BF16)
    ki_o[...] = _rope(ki_ref[...], tidx_ref, IDX_ROT // 2)
    diffq_o[...] = _rope(fq_ref[...], t64_ref, DIFF_ROT // 2).astype(BF16)
    diffk_o[...] = _rope(fk_ref[...], t64_ref, DIFF_ROT // 2)


def _state(hw, q_norm_g, kv_norm_g, w_uq_nope, w_uq_pe, t64, t32, tidx):
    T = hw.shape[0]

    def col(name):
        off, w = _DST[name]
        return pl.BlockSpec((ROW_TILE, w), lambda i: (i, off // w))

    def full(shape):
        return pl.BlockSpec(shape, lambda i: (0,) * len(shape))

    def tab():
        return pl.BlockSpec((3, ROW_TILE, LANE), lambda i: (0, i, 0))

    def out(w, dt):
        return pl.BlockSpec((ROW_TILE, w), lambda i: (i, 0)), jax.ShapeDtypeStruct((T, w), dt)

    outs = [out(MLA_KV_LORA, F32), out(LANE, F32), out(512, F32), out(LANE, F32), out(512, F32),
            out(512, BF16), out(MLA_HEADS * MLA_ROPE, BF16), out(512, BF16), out(IDX_HEADS * IDX_DIM, BF16),
            out(512, BF16)]
    return pl.pallas_call(
        _state_kernel, grid=(T // ROW_TILE,),
        in_specs=[col('cq'), col('ckv'), col('krope'), col('dq'), col('dk'), col('qi'), col('ki'),
                  col('fq'), col('fk'),
                  full((1, MLA_Q_LORA)), full((1, MLA_KV_LORA)),
                  full((MLA_Q_LORA, MLA_HEADS * MLA_NOPE)), full((MLA_Q_LORA, MLA_HEADS * MLA_ROPE)),
                  tab(), tab(), tab()],
        out_specs=[o[0] for o in outs], out_shape=[o[1] for o in outs],
        compiler_params=_cp("parallel"), name="mixer_state")(
            hw, hw, hw, hw, hw, hw, hw, hw, hw,
            q_norm_g.reshape(1, -1), kv_norm_g.reshape(1, -1), w_uq_nope, w_uq_pe, t64, t32, tidx)


def _visible(q0, k0, tq, tk):
    qc = (q0 + lax.broadcasted_iota(I32, (tq, tk), 0)) // CHUNK
    kc = (k0 + lax.broadcasted_iota(I32, (tq, tk), 1)) // CHUNK
    return kc <= qc


def _flash_kernel(*refs, tq, tk, nkb, causal, s_valid, dk, dv, scale, diff_layer):
    if diff_layer is None:
        q_ref, kt_ref, v_ref, o_ref, m_ref, l_ref, acc_ref = refs
    else:
        q_ref, kt_ref, v_ref, lam_ref, sg_ref, o_ref, m_ref, l_ref, acc_ref = refs
    i = pl.program_id(2)
    c = scale * LOG2E
    n_full = i * tq // tk if causal else nkb
    n_heads = q_ref.shape[0]

    def head(hl):
        q = q_ref[hl]

        def step(j, masked):
            s = jnp.dot(q, kt_ref[hl, j], preferred_element_type=F32)
            if masked:
                ok = None
                if causal:
                    ok = _visible(i * tq, j * tk, tq, tk)
                if s_valid < nkb * tk:
                    okp = (j * tk + lax.broadcasted_iota(I32, (tq, tk), 1)) < s_valid
                    ok = okp if ok is None else jnp.logical_and(ok, okp)
                s = jnp.where(ok, s, NEG_INF)
            m_prev = m_ref[...]
            m_new = jnp.maximum(m_prev, jnp.max(s, axis=-1, keepdims=True))
            alpha = jnp.exp2((m_prev - m_new) * c)
            p = jnp.exp2((s - m_new) * c)
            l_ref[...] = alpha * l_ref[...] + jnp.sum(p, axis=-1, keepdims=True)
            off = j * tk if isinstance(j, int) else pl.multiple_of(j * tk, tk)
            acc_ref[...] = alpha * acc_ref[...] + jnp.dot(
                p.astype(BF16), v_ref[pl.ds(off, tk), :], preferred_element_type=F32)
            m_ref[...] = m_new

        m_ref[...] = jnp.full(m_ref.shape, NEG_INF, F32)
        l_ref[...] = jnp.zeros(l_ref.shape, F32)
        acc_ref[...] = jnp.zeros(acc_ref.shape, F32)
        if causal:
            lax.fori_loop(0, n_full, lambda j, _: (step(j, False), 0)[1], 0)
            n_diag = (tq + tk - 1) // tk
            for d in range(n_diag):
                step(n_full + d, True)
        else:
            pad = s_valid < nkb * tk
            lax.fori_loop(0, nkb - 1 if pad else nkb, lambda j, _: (step(j, False), 0)[1], 0)
            if pad:
                step(nkb - 1, True)
        return acc_ref[...] / l_ref[...]

    if diff_layer is None:
        for hl in range(n_heads):
            o = head(hl)
            o_ref[:, hl * dv:(hl + 1) * dv] = o[:, hl * dv:(hl + 1) * dv].astype(o_ref.dtype)
    else:
        lam_init = 0.8 - 0.6 * math.exp(-0.3 * diff_layer)
        lp = lam_ref[...]
        lam = (jnp.exp(jnp.sum(lp[0:1] * lp[1:2], axis=-1, keepdims=True))
               - jnp.exp(jnp.sum(lp[2:3] * lp[3:4], axis=-1, keepdims=True)) + lam_init)
        for h2 in range(n_heads // 2):
            lo, hi = h2 * dv, (h2 + 1) * dv
            o1 = head(2 * h2)[:, lo:hi]
            o2 = head(2 * h2 + 1)[:, lo:hi]
            o = o1 - lam * o2
            o_ref[:, lo:hi] = (_rms(o, sg_ref[...]) * (1.0 - lam_init)).astype(o_ref.dtype)


def _flash(q, kt, v, *, tq, tk, causal, s_valid, dv, scale, diff=None, name="flash"):
    B, G, Tq, dk = q.shape
    nkb = kt.shape[2]
    S_pad = nkb * tk
    hg = HEAD_GROUP
    n_grp = G // hg
    in_specs = [pl.BlockSpec((None, hg, tq, dk), lambda b, g, i: (b, g, i, 0)),
                pl.BlockSpec((None, hg, nkb, dk, tk), lambda b, g, i: (b, g, 0, 0, 0)),
                pl.BlockSpec((None, S_pad, MXU_WIDTH), lambda b, g, i: (b, 0, g))]
    args = [q, kt, v]
    diff_layer = None
    if diff is not None:
        lam_params, subln_g, diff_layer = diff
        in_specs += [pl.BlockSpec(lam_params.shape, lambda b, g, i: (0, 0)),
                     pl.BlockSpec((1, dv), lambda b, g, i: (0, 0))]
        args += [lam_params, subln_g.reshape(1, dv)]
    kern = functools.partial(_flash_kernel, tq=tq, tk=tk, nkb=nkb, causal=causal, s_valid=s_valid,
                             dk=dk, dv=dv, scale=scale, diff_layer=diff_layer)
    return pl.pallas_call(
        kern, grid=(B, n_grp, Tq // tq), in_specs=in_specs,
        out_specs=pl.BlockSpec((None, tq, MXU_WIDTH), lambda b, g, i: (b, i, g)),
        out_shape=jax.ShapeDtypeStruct((B, Tq, n_grp * MXU_WIDTH), BF16),
        scratch_shapes=[pltpu.VMEM((tq, 1), F32), pltpu.VMEM((tq, 1), F32), pltpu.VMEM((tq, MXU_WIDTH), F32)],
        compiler_params=_cp("parallel", "parallel", "parallel"), name=name)(*args)


def _dsa_kernel(qi_ref, wi_ref, kit_ref, q_ref, kt_ref, v_ref, o_ref,
                key_ref, bias_ref, wib_ref, m_ref, l_ref, acc_ref,
                *, tq, tk, nkb, causal, s_valid, n_sel, scale):
    i = pl.program_id(1)
    nvis = jnp.minimum((i * tq + tq + tk - 1) // tk, nkb) if causal else nkb

    wi = wi_ref[...] * IDX_SCALE
    for h in range(IDX_HEADS):
        wib_ref[h] = jnp.broadcast_to(wi[:, h:h + 1], (tq, LANE))

    def lanes(x):
        return jnp.concatenate([x] * (tk // LANE), axis=1)

    def score_block(j, _):
        kit = kit_ref[j]
        sc = jnp.zeros((tq, tk), F32)
        for h in range(IDX_HEADS):
            r = jnp.dot(qi_ref[h], kit, preferred_element_type=F32)
            sc = sc + jnp.maximum(r, 0.0) * lanes(wib_ref[h])
        sc = sc + 0.0
        bits = pltpu.bitcast(sc, I32)
        key = jnp.where(bits < 0, bits ^ 0x7FFFFFFF, bits)
        ok = (j * tk + lax.broadcasted_iota(I32, (tq, tk), 1)) < s_valid
        if causal:
            ok = jnp.logical_and(ok, _visible(i * tq, j * tk, tq, tk))
        key_ref[j] = jnp.where(ok, key, INT_MIN)
        return 0

    lax.fori_loop(0, nvis, score_block, 0)

    def count(pred):
        def body(j, cnt):
            hit = jnp.where(pred(key_ref[j], j), 1.0, 0.0)
            part = hit[:, 0:LANE]
            for s in range(1, tk // LANE):
                part = part + hit[:, s * LANE:(s + 1) * LANE]
            return cnt + part
        cnt = lax.fori_loop(0, nvis, body, jnp.zeros((tq, LANE), F32))
        return jnp.sum(cnt, axis=-1, keepdims=True)

    def bit_step(it, t):
        cand = t + lax.shift_left(jnp.int32(1), 31 - it)
        cnt = count(lambda k, j: k >= cand)
        return jnp.where(cnt >= n_sel, cand, t)

    thr = lax.fori_loop(0, 32, bit_step, jnp.full((tq, 1), INT_MIN, I32))
    cnt_ge = count(lambda k, j: k >= thr)
    need_tie = jnp.logical_and(cnt_ge > n_sel, thr > INT_MIN)
    any_tie = jnp.max(jnp.where(need_tie, 1.0, 0.0)) > 0.0

    @pl.when(jnp.logical_not(any_tie))
    def _():
        tsel = jnp.maximum(thr, INT_MIN + 1)

        def wr(j, _):
            bias_ref[j] = jnp.where(key_ref[j] >= tsel, 0.0, NEG_INF)
            return 0
        lax.fori_loop(0, nvis, wr, 0)

    @pl.when(any_tie)
    def _():
        cnt_gt = count(lambda k, j: k > thr)

        def colidx(j):
            return j * tk + lax.broadcasted_iota(I32, (tq, tk), 1)

        def idx_step(it, x):
            cand = x + lax.shift_left(jnp.int32(1), 14 - it)
            g = cnt_gt + count(lambda k, j: jnp.logical_and(k == thr, colidx(j) < cand))
            return jnp.where(g < n_sel, cand, x)

        xj = lax.fori_loop(0, 15, idx_step, jnp.zeros((tq, 1), I32))
        jmax = jnp.where(need_tie, xj, jnp.where(thr == INT_MIN, -1, 2 ** 30))

        def wr(j, _):
            k = key_ref[j]
            sel = jnp.logical_or(k > thr, jnp.logical_and(k == thr, colidx(j) <= jmax))
            bias_ref[j] = jnp.where(sel, 0.0, NEG_INF)
            return 0
        lax.fori_loop(0, nvis, wr, 0)

    c = scale * LOG2E
    for hl in range(DSA_HEADS):
        grp, hin = hl // HEAD_GROUP, hl % HEAD_GROUP
        q = q_ref[hl]
        m_ref[...] = jnp.full(m_ref.shape, NEG_INF, F32)
        l_ref[...] = jnp.zeros(l_ref.shape, F32)
        acc_ref[...] = jnp.zeros(acc_ref.shape, F32)

        def step(j, _, hl=hl, grp=grp, q=q):
            s = jnp.dot(q, kt_ref[hl, j], preferred_element_type=F32) + bias_ref[j]
            m_prev = m_ref[...]
            m_new = jnp.maximum(m_prev, jnp.max(s, axis=-1, keepdims=True))
            alpha = jnp.exp2((m_prev - m_new) * c)
            p = jnp.exp2((s - m_new) * c)
            l_ref[...] = alpha * l_ref[...] + jnp.sum(p, axis=-1, keepdims=True)
            off = pl.multiple_of(j * tk, tk)
            acc_ref[...] = alpha * acc_ref[...] + jnp.dot(
                p.astype(BF16), v_ref[pl.ds(off, tk), grp * MXU_WIDTH:(grp + 1) * MXU_WIDTH],
                preferred_element_type=F32)
            m_ref[...] = m_new
            return 0

        lax.fori_loop(0, nvis, step, 0)
        o = acc_ref[...] / l_ref[...]
        lo = hin * DSA_HEAD_DIM
        o_ref[:, hl * DSA_HEAD_DIM:(hl + 1) * DSA_HEAD_DIM] = o[:, lo:lo + DSA_HEAD_DIM].astype(o_ref.dtype)


def _dsa(qi, wi, kit, q, kt, v, *, tq, tk, causal, s_valid, n_sel, name="dsa"):
    B, H, Tq, dk = q.shape
    nkb = kt.shape[2]
    S_pad = nkb * tk
    kern = functools.partial(_dsa_kernel, tq=tq, tk=tk, nkb=nkb, causal=causal, s_valid=s_valid,
                             n_sel=n_sel, scale=DSA_HEAD_DIM ** -0.5)
    return pl.pallas_call(
        kern, grid=(B, Tq // tq),
        in_specs=[pl.BlockSpec((None, IDX_HEADS, tq, IDX_DIM), lambda b, i: (b, 0, i, 0)),
                  pl.BlockSpec((None, tq, LANE), lambda b, i: (b, i, 0)),
                  pl.BlockSpec((None, nkb, IDX_DIM, tk), lambda b, i: (b, 0, 0, 0)),
                  pl.BlockSpec((None, H, tq, dk), lambda b, i: (b, 0, i, 0)),
                  pl.BlockSpec((None, H, nkb, dk, tk), lambda b, i: (b, 0, 0, 0, 0)),
                  pl.BlockSpec((None, S_pad, H * dk), lambda b, i: (b, 0, 0))],
        out_specs=pl.BlockSpec((None, tq, H * dk), lambda b, i: (b, i, 0)),
        out_shape=jax.ShapeDtypeStruct((B, Tq, H * dk), BF16),
        scratch_shapes=[pltpu.VMEM((nkb, tq, tk), I32), pltpu.VMEM((nkb, tq, tk), F32),
                        pltpu.VMEM((IDX_HEADS, tq, LANE), F32),
                        pltpu.VMEM((tq, 1), F32), pltpu.VMEM((tq, 1), F32), pltpu.VMEM((tq, MXU_WIDTH), F32)],
        compiler_params=_cp("parallel", "parallel"), name=name)(qi, wi, kit, q, kt, v)


def _merge_kernel(h_ref, oa_ref, ob_ref, oc_ref, od_ref, wg0, wg1, wg2, wg3, bg0, bg1, bg2, bg3, wb_ref, o_ref):
    h = h_ref[...]
    acc = None
    for n, (o_r, wg, bg) in enumerate(((oa_ref, wg0, bg0), (ob_ref, wg1, bg1), (oc_ref, wg2, bg2),
                                       (od_ref, wg3, bg3))):
        gate = jax.nn.sigmoid(jnp.dot(h, wg[...], preferred_element_type=F32) + bg[...])
        term = gate * jnp.dot(o_r[...], wb_ref[n], preferred_element_type=F32)
        acc = term if acc is None else acc + term
    o_ref[...] = acc.astype(BF16)


def _merge(h, branches, w_gate, b_gate, w_branch, tn=512):
    T, D = h.shape
    nd = D // tn
    b_gate = b_gate.reshape(1, -1)
    specs = [pl.BlockSpec((ROW_TILE, D), lambda i, j: (i, 0))]
    specs += [pl.BlockSpec((ROW_TILE, BRANCH_WIDTH), lambda i, j: (i, 0))] * N_BRANCH
    specs += [pl.BlockSpec((D, tn), lambda i, j, n=n: (0, n * nd + j)) for n in range(N_BRANCH)]
    specs += [pl.BlockSpec((1, tn), lambda i, j, n=n: (0, n * nd + j)) for n in range(N_BRANCH)]
    specs += [pl.BlockSpec((N_BRANCH, BRANCH_WIDTH, tn), lambda i, j: (0, 0, j))]
    return pl.pallas_call(
        _merge_kernel, grid=(T // ROW_TILE, nd), in_specs=specs,
        out_specs=pl.BlockSpec((ROW_TILE, tn), lambda i, j: (i, j)),
        out_shape=jax.ShapeDtypeStruct((T, D), BF16),
        compiler_params=_cp("parallel", "parallel"), name="merge")(
            h, *branches, w_gate, w_gate, w_gate, w_gate, b_gate, b_gate, b_gate, b_gate, w_branch)


def _ffn_up_kernel(h_ref, w1_ref, w3_ref, o_ref):
    h = h_ref[...]
    a = jnp.dot(h, w1_ref[...], preferred_element_type=F32)
    b = jnp.dot(h, w3_ref[...], preferred_element_type=F32)
    o_ref[...] = (a * jax.nn.sigmoid(a) * b).astype(BF16)


def _moe_up_kernel(h_ref, w1_ref, w3_ref, gate_ref, o_ref):
    e = pl.program_id(1)
    h = h_ref[...]
    a = jnp.dot(h, w1_ref[...], preferred_element_type=F32)
    b = jnp.dot(h, w3_ref[...], preferred_element_type=F32)
    gate = gate_ref[...]
    lane = lax.broadcasted_iota(I32, gate.shape, 1)
    ge = jnp.sum(jnp.where(lane == e, gate, 0.0), axis=-1, keepdims=True)
    o_ref[...] = (a * jax.nn.sigmoid(a) * b * ge).astype(BF16)


def _ffn_up(h, w1, w3, tn=512):
    T, D = h.shape
    F = w1.shape[1]
    return pl.pallas_call(
        _ffn_up_kernel, grid=(T // ROW_TILE, F // tn),
        in_specs=[pl.BlockSpec((ROW_TILE, D), lambda i, j: (i, 0)),
                  pl.BlockSpec((D, tn), lambda i, j: (0, j)), pl.BlockSpec((D, tn), lambda i, j: (0, j))],
        out_specs=pl.BlockSpec((ROW_TILE, tn), lambda i, j: (i, j)),
        out_shape=jax.ShapeDtypeStruct((T, F), BF16),
        compiler_params=_cp("parallel", "parallel"), name="ffn_up")(h, w1, w3)


def _moe_up(h, w1, w3, gate):
    T, D = h.shape
    E, _, F = w1.shape
    return pl.pallas_call(
        _moe_up_kernel, grid=(T // ROW_TILE, E),
        in_specs=[pl.BlockSpec((ROW_TILE, D), lambda i, e: (i, 0)),
                  pl.BlockSpec((None, D, F), lambda i, e: (e, 0, 0)),
                  pl.BlockSpec((None, D, F), lambda i, e: (e, 0, 0)),
                  pl.BlockSpec((ROW_TILE, LANE), lambda i, e: (i, 0))],
        out_specs=pl.BlockSpec((ROW_TILE, F), lambda i, e: (i, e)),
        out_shape=jax.ShapeDtypeStruct((T, E * F), BF16),
        compiler_params=_cp("parallel", "parallel"), name="moe_up")(h, w1, w3, gate)


def _final_norm_kernel(x_ref, g_ref, o_ref):
    o_ref[...] = _rms(x_ref[...], g_ref[...])


def _final_norm(x, g):
    T, D = x.shape
    return pl.pallas_call(
        _final_norm_kernel, grid=(T // ROW_TILE,),
        in_specs=[pl.BlockSpec((ROW_TILE, D), lambda i: (i, 0)), pl.BlockSpec((1, D), lambda i: (0, 0))],
        out_specs=pl.BlockSpec((ROW_TILE, D), lambda i: (i, 0)),
        out_shape=jax.ShapeDtypeStruct((T, D), F32),
        compiler_params=_cp("parallel"), name="final_norm")(x, g.reshape(1, D))


def _pad_rows(a, s_pad):
    pad = s_pad - a.shape[1]
    if pad == 0:
        return a
    return jnp.pad(a, ((0, 0), (0, pad)) + ((0, 0),) * (a.ndim - 2))


def _heads_q(a, n_heads):
    B, T, W = a.shape
    return a.reshape(B, T, n_heads, W // n_heads).transpose(0, 2, 1, 3)


def _heads_kt(a, tk):
    B, S, H, d = a.shape
    return a.reshape(B, S // tk, tk, H, d).transpose(0, 3, 1, 4, 2)


def _pad_in_weight(w_in):
    D = w_in.shape[0]
    out = jnp.zeros((D, IN_PAD), BF16)
    for name in _ORDER:
        s0, sw = _SRC[name]
        d0, _ = _DST[name]
        out = lax.dynamic_update_slice(out, w_in[:, s0:s0 + sw].astype(BF16), (0, d0))
    return out


def _col(hw, name, width=None):
    off, w = _DST[name]
    return hw[:, off:off + (width or w)]


def _mixers(l, hw, st, caches, lp, geom):
    n_p, B_p, T_p, B_s, T_s, past = geom
    ckv_n, kpe, dsa_k, ki, diff_k, q_nope, q_pe, dsa_q, qi, diff_q = st
    wi = _col(hw, 'wi')
    dsa_v = _col(hw, 'dv')
    diff_v = _col(hw, 'fv')

    def split(a):
        W = a.shape[1]
        return a[:n_p].reshape(B_p, T_p, W), a[n_p:].reshape(B_s, T_s, W)

    o_a_p, v_p = _gmlp(hw, 0, n_p, min(T_p, GM_CHUNK), lp['gm_ln_g'], lp['gm_ln_b'], lp['gm_ws'], lp['gm_bs'])
    o_a_s, v_s = _gmlp(hw, n_p, hw.shape[0] - n_p, min(T_s, GM_CHUNK), lp['gm_ln_g'], lp['gm_ln_b'],
                       lp['gm_ws'], lp['gm_bs'])
    c_p, c_s = min(T_p, GM_CHUNK), min(T_s, GM_CHUNK)
    gm_v_p = v_p.reshape(B_p, T_p, GM_WIDTH)[:, T_p - c_p:]
    gm_v_s = v_s.reshape(B_s, T_s, GM_WIDTH)[:, T_s - c_s:]

    outs = {}
    for grp in ('p', 's'):
        pi = 0 if grp == 'p' else 1
        B, T = (B_p, T_p) if grp == 'p' else (B_s, T_s)

        def g(a, pi=pi):
            return split(a)[pi]

        if grp == 'p':
            S = T
            tk = 512 if S % 512 == 0 else S
            s_pad = S
            tq_f, tq_d = min(512, T), min(128, T)
            causal = True

            def ext(new, cache):
                return new
        else:
            S = past + T
            s_pad = -(-S // LANE) * LANE
            tk = s_pad
            tq_f = tq_d = T
            causal = False

            def ext(new, cache, s_pad=s_pad):
                return _pad_rows(jnp.concatenate([cache.reshape(cache.shape[:2] + (-1,)), new], axis=1), s_pad)

        cm_ckv, cm_kr, cd_k, cd_v, cd_ki, cf_k, cf_v = caches

        all_c = ext(g(ckv_n), cm_ckv).astype(BF16)
        all_pe = ext(g(kpe)[..., :MLA_ROPE], cm_kr).astype(BF16)
        kv = _mm(all_c.reshape(B * s_pad, MLA_KV_LORA), lp['w_ukv'], BF16,
                 tm=ROW_TILE if (B * s_pad) % ROW_TILE == 0 else LANE, name="mla_kv")
        kv = kv.reshape(B, s_pad, 2 * MLA_HEADS * MLA_NOPE)
        k_nope = kv[..., :MLA_HEADS * MLA_NOPE].reshape(B, s_pad, MLA_HEADS, MLA_NOPE)
        k_mla = jnp.concatenate(
            [k_nope, jnp.broadcast_to(all_pe[:, :, None, :], (B, s_pad, MLA_HEADS, MLA_ROPE))], axis=-1)
        q_mla = jnp.concatenate([g(q_nope).reshape(B, T, MLA_HEADS, MLA_NOPE),
                                 g(q_pe).reshape(B, T, MLA_HEADS, MLA_ROPE)], axis=-1).transpose(0, 2, 1, 3)
        o_b = _flash(q_mla, _heads_kt(k_mla, tk), kv[..., MLA_HEADS * MLA_NOPE:], tq=tq_f, tk=tk,
                     causal=causal, s_valid=S, dv=MLA_V, scale=(MLA_NOPE + MLA_ROPE) ** -0.5, name="mla_attn")

        k_all = ext(g(dsa_k), cd_k).astype(BF16)
        v_all = ext(g(dsa_v), cd_v).astype(BF16)
        ki_all = ext(g(ki)[..., :IDX_DIM], cd_ki).astype(BF16)
        kit = ki_all.reshape(B, s_pad // tk, tk, IDX_DIM).transpose(0, 1, 3, 2)
        o_c = _dsa(_heads_q(g(qi), IDX_HEADS), g(wi), kit, _heads_q(g(dsa_q), DSA_HEADS),
                   _heads_kt(k_all.reshape(B, s_pad, DSA_HEADS, DSA_HEAD_DIM), tk), v_all,
                   tq=tq_d, tk=tk, causal=causal, s_valid=S, n_sel=min(DSA_TOPK, S // 4))

        fk_all = ext(g(diff_k), cf_k).astype(BF16)
        fv_all = ext(g(diff_v), cf_v).astype(BF16)
        o_d = _flash(_heads_q(g(diff_q), 2 * DIFF_HEADS),
                     _heads_kt(fk_all.reshape(B, s_pad, 2 * DIFF_HEADS, DIFF_QK_DIM), tk), fv_all,
                     tq=tq_f, tk=tk, causal=causal, s_valid=S, dv=DIFF_V_DIM, scale=DIFF_QK_DIM ** -0.5,
                     diff=(lp['diff_lambda'], lp['diff_subln_g'], l), name="diff_attn")
        outs[grp] = (o_b.reshape(B * T, -1), o_c.reshape(B * T, -1), o_d.reshape(B * T, -1))

    o_a = jnp.concatenate([o_a_p, o_a_s], axis=0)
    o_b, o_c, o_d = (jnp.concatenate([outs['p'][k], outs['s'][k]], axis=0) for k in range(3))

    def state(grp):
        pi = 0 if grp == 'p' else 1
        B, T = (B_p, T_p) if grp == 'p' else (B_s, T_s)

        def g(a):
            return split(a)[pi]
        return (gm_v_p if grp == 'p' else gm_v_s,
                g(ckv_n), g(kpe)[..., :MLA_ROPE],
                g(dsa_k).reshape(B, T, DSA_HEADS, DSA_HEAD_DIM), g(dsa_v).reshape(B, T, DSA_HEADS, DSA_HEAD_DIM),
                g(ki)[..., :IDX_DIM],
                g(diff_k).reshape(B, T, DIFF_HEADS, 2, DIFF_QK_DIM), g(diff_v).reshape(B, T, DIFF_HEADS, DIFF_V_DIM))

    return (o_a, o_b, o_c, o_d), state('p'), state('s')


def kernel(x_prompt, x_sample, c_prompt, c_sample, cache_mla_ckv, cache_mla_krope, cache_dsa_k, cache_dsa_v, cache_dsa_kidx, cache_diff_k, cache_diff_v, ada_w, ada_b, norm_mix_g, norm_ff_g, w_in, w_gate, b_gate, gm_ln_g, gm_ln_b, gm_ws, gm_bs, mla_q_norm_g, mla_kv_norm_g, mla_w_uq, mla_w_ukv, diff_lambda, diff_subln_g, w_branch, w_out, ffn_w1, ffn_w3, ffn_w2, moe_router_w, moe_router_b, moe_w1, moe_w3, moe_w2, final_norm_g):
    B_p, T_p, D = x_prompt.shape
    B_s, T_s, _ = x_sample.shape
    depth = ada_w.shape[0]
    past = cache_mla_ckv.shape[2]
    n_p, n_s = B_p * T_p, B_s * T_s
    assert T_p % ROW_TILE == 0 and n_s % ROW_TILE == 0 and ROW_TILE % T_s == 0
    assert past % CHUNK == 0 and T_s <= CHUNK
    n_p_tiles, tiles_per_b = n_p // ROW_TILE, T_p // ROW_TILE
    geom = (n_p, B_p, T_p, B_s, T_s, past)

    def mod_map(i):
        return jnp.where(i < n_p_tiles, i // tiles_per_b, B_p + i - n_p_tiles)

    x = jnp.concatenate([x_prompt.reshape(n_p, D), x_sample.reshape(n_s, D)], axis=0)

    n_c = B_p + B_s
    c_rows = jnp.concatenate([c_prompt, c_sample, jnp.zeros((-n_c % 8, D), F32)], axis=0)
    mod = _ada(c_rows, ada_w, ada_b)
    mod = mod[:, :n_c].reshape(depth, n_c, 6, D).transpose(0, 2, 1, 3)
    mod_p = jnp.broadcast_to(mod[:, :, :B_p, None, :], (depth, 6, B_p, ROW_TILE, D))
    mod_s = jnp.broadcast_to(mod[:, :, B_p:, None, :], (depth, 6, B_s, T_s, D)).reshape(
        depth, 6, n_s // ROW_TILE, ROW_TILE, D)
    modexp = jnp.concatenate([mod_p, mod_s], axis=2)

    pos = jnp.concatenate([jnp.tile(jnp.arange(T_p, dtype=I32), B_p),
                           jnp.tile(past + jnp.arange(T_s, dtype=I32), B_s)])
    t64 = _rope_tables(pos, DSA_ROT, DSA_HEAD_DIM)
    t32 = _rope_tables(pos, MLA_ROPE, MLA_ROPE)
    tidx = _rope_tables(pos, IDX_ROT, IDX_DIM)

    states_p, states_s = [], []
    for l in range(depth):
        uq = mla_w_uq[l].reshape(MLA_Q_LORA, MLA_HEADS, MLA_NOPE + MLA_ROPE)
        ukv = mla_w_ukv[l].reshape(MLA_KV_LORA, MLA_HEADS, MLA_NOPE + MLA_V)
        lp = {
            'gm_ln_g': gm_ln_g[l], 'gm_ln_b': gm_ln_b[l], 'gm_ws': gm_ws[l], 'gm_bs': gm_bs[l],
            'w_ukv': jnp.concatenate([ukv[..., :MLA_NOPE].reshape(MLA_KV_LORA, -1),
                                      ukv[..., MLA_NOPE:].reshape(MLA_KV_LORA, -1)], axis=1).astype(BF16),
            'diff_lambda': diff_lambda[l], 'diff_subln_g': diff_subln_g[l],
        }
        h = _norm_mod(x, norm_mix_g[l], modexp[l], mod_map, 1, 0)
        hw = _mm(h, _pad_in_weight(w_in[l]), F32, name="in_proj")
        st = _state(hw, mla_q_norm_g[l], mla_kv_norm_g[l],
                    uq[..., :MLA_NOPE].reshape(MLA_Q_LORA, -1).astype(BF16),
                    uq[..., MLA_NOPE:].reshape(MLA_Q_LORA, -1).astype(BF16), t64, t32, tidx)
        caches = (cache_mla_ckv[l], cache_mla_krope[l], cache_dsa_k[l], cache_dsa_v[l], cache_dsa_kidx[l],
                  cache_diff_k[l], cache_diff_v[l])
        branches, st_p, st_s = _mixers(l, hw, st, caches, lp, geom)
        states_p.append(st_p)
        states_s.append(st_s)
        merged = _merge(h, branches, w_gate[l].astype(BF16), b_gate[l], w_branch[l].astype(BF16))
        x = _mm_res(merged, w_out[l].astype(BF16), x, modexp[l], mod_map, 2, tk=D, name="out_proj")

        j = l // 2
        if l % 2 == 0:
            h2 = _norm_mod(x, norm_ff_g[l], modexp[l], mod_map, 4, 3)
            act = _ffn_up(h2, ffn_w1[j].astype(BF16), ffn_w3[j].astype(BF16))
            w2 = ffn_w2[j].astype(BF16)
        else:
            rw = jnp.zeros((D, LANE), F32).at[:, :N_EXPERTS].set(moe_router_w[j])
            rw_hi = rw.astype(BF16)
            rw_lo = (rw - rw_hi.astype(F32)).astype(BF16)
            rb = jnp.zeros((1, LANE), F32).at[0, :N_EXPERTS].set(moe_router_b[j])
            h2, gate = _norm_mod(x, norm_ff_g[l], modexp[l], mod_map, 4, 3, router=(rw_hi, rw_lo, rb))
            act = _moe_up(h2, moe_w1[j].astype(BF16), moe_w3[j].astype(BF16), gate)
            w2 = moe_w2[j].astype(BF16).reshape(-1, D)
        x = _mm_res(act, w2, x, modexp[l], mod_map, 5, tk=w2.shape[0] // 4, name="ffn_down")

    y = _final_norm(x, final_norm_g)
    outs = [y[:n_p].reshape(B_p, T_p, D), y[n_p:].reshape(B_s, T_s, D)]
    for k in range(8):
        outs.append(jnp.stack([s[k] for s in states_p], axis=0))
        outs.append(jnp.stack([s[k] for s in states_s], axis=0))
    return tuple(outs)
```

```python
import functools
import math

import jax
import jax.numpy as jnp
from jax import lax
from jax.experimental import pallas as pl
from jax.experimental.pallas import tpu as pltpu

F32 = jnp.float32
BF16 = jnp.bfloat16
I32 = jnp.int32

CHUNK = 64
ROPE_THETA = 500000.0
NORM_EPS = 1e-6
NEG_INF = -1e30
N_BRANCH = 4
BRANCH_WIDTH = 512
GM_CHUNK = 128
GM_GROUPS = 4
GM_WIDTH = 512
GM_GROUP_DIM = GM_WIDTH // GM_GROUPS
MLA_HEADS = 8
MLA_Q_LORA = 512
MLA_KV_LORA = 256
MLA_NOPE = 64
MLA_ROPE = 32
MLA_V = 64
DSA_HEADS = 8
DSA_HEAD_DIM = 64
DSA_ROT = DSA_HEAD_DIM // 4
IDX_HEADS = 8
IDX_DIM = 32
IDX_ROT = IDX_DIM // 4
IDX_SCALE = (IDX_DIM ** -0.5) * (IDX_HEADS ** -0.5)
DSA_TOPK = 256
DIFF_HEADS = 4
DIFF_QK_DIM = 64
DIFF_V_DIM = 128
DIFF_ROT = DIFF_QK_DIM // 4
N_EXPERTS = 8

LANE = 128
MXU_WIDTH = 256
VMEM_LIMIT_BYTES = 56 * 1024 * 1024

ROW_TILE = 512
HEAD_GROUP = MXU_WIDTH // DSA_HEAD_DIM
LOG2E = 1.4426950408889634
INT_MIN = -2 ** 31
MLA_QSCALE = (MLA_NOPE + MLA_ROPE) ** -0.5 * LOG2E
DSA_QSCALE = DSA_HEAD_DIM ** -0.5 * LOG2E
DIFF_QSCALE = DIFF_QK_DIM ** -0.5 * LOG2E
KV_BLOCK = 256
Q_TILE = MXU_WIDTH

_SPLIT = (
    ('gu', GM_WIDTH), ('gv', GM_WIDTH), ('cq', MLA_Q_LORA), ('ckv', MLA_KV_LORA), ('krope', MLA_ROPE),
    ('dq', 512), ('dk', 512), ('dv', 512), ('qi', IDX_HEADS * IDX_DIM), ('ki', IDX_DIM), ('wi', IDX_HEADS),
    ('fq', 512), ('fk', 512), ('fv', 512),
)
_ORDER = ('gu', 'gv', 'cq', 'dq', 'dk', 'dv', 'fq', 'fk', 'fv', 'ckv', 'qi', 'krope', 'ki', 'wi')


def _in_layout():
    src, acc = {}, 0
    for name, w in _SPLIT:
        src[name] = (acc, w)
        acc += w
    dst, off = {}, 0
    for name in _ORDER:
        w = src[name][1]
        pw = -(-w // LANE) * LANE
        dst[name] = (off, pw)
        off += pw
    total = -(-off // ROW_TILE) * ROW_TILE
    return src, dst, total


_SRC, _DST, IN_PAD = _in_layout()


def _cp(*sem):
    return pltpu.CompilerParams(dimension_semantics=sem, vmem_limit_bytes=VMEM_LIMIT_BYTES)


def _rms(x, g):
    return x * lax.rsqrt(jnp.mean(x * x, axis=-1, keepdims=True) + NORM_EPS) * g


def _ada_kernel(c_ref, w_ref, b_ref, o_ref):
    c = c_ref[...]
    a = (c * jax.nn.sigmoid(c)).astype(BF16)
    o_ref[...] = jnp.dot(a, w_ref[...].astype(BF16), preferred_element_type=F32) + b_ref[...]


def _ada(c_rows, ada_w, ada_b):
    L, D, N = ada_w.shape
    R = c_rows.shape[0]
    tn = 512
    return pl.pallas_call(
        _ada_kernel,
        grid=(L, N // tn),
        in_specs=[pl.BlockSpec((R, D), lambda l, j: (0, 0)),
                  pl.BlockSpec((None, D, tn), lambda l, j: (l, 0, j)),
                  pl.BlockSpec((None, 1, tn), lambda l, j: (l, 0, j))],
        out_specs=pl.BlockSpec((None, R, tn), lambda l, j: (l, 0, j)),
        out_shape=jax.ShapeDtypeStruct((L, R, N), F32),
        compiler_params=_cp("parallel", "parallel"),
        name="ada_mod",
    )(c_rows, ada_w, ada_b.reshape(L, 1, N))


def _norm_mod_kernel(x_ref, g_ref, sc_ref, sh_ref, h_ref):
    y = _rms(x_ref[...], g_ref[...])
    h_ref[...] = (y * (1.0 + sc_ref[...]) + sh_ref[...]).astype(BF16)


def _norm_mod_router_kernel(x_ref, g_ref, sc_ref, sh_ref, rwh_ref, rwl_ref, rb_ref, h_ref, gate_ref):
    y = _rms(x_ref[...], g_ref[...])
    h = y * (1.0 + sc_ref[...]) + sh_ref[...]
    h_hi = h.astype(BF16)
    h_ref[...] = h_hi
    h_lo = (h - h_hi.astype(F32)).astype(BF16)
    logits = (jnp.dot(h_hi, rwh_ref[...], preferred_element_type=F32)
              + jnp.dot(h_lo, rwh_ref[...], preferred_element_type=F32)
              + jnp.dot(h_hi, rwl_ref[...], preferred_element_type=F32)) + rb_ref[...]
    lane = lax.broadcasted_iota(I32, logits.shape, 1)
    lg = jnp.where(lane < N_EXPERTS, logits, -jnp.inf)
    m1 = jnp.max(lg, axis=-1, keepdims=True)
    i1 = jnp.min(jnp.where(lg == m1, lane, LANE), axis=-1, keepdims=True)
    lg2 = jnp.where(lane == i1, -jnp.inf, lg)
    m2 = jnp.max(lg2, axis=-1, keepdims=True)
    i2 = jnp.min(jnp.where(lg2 == m2, lane, LANE), axis=-1, keepdims=True)
    e2 = jnp.exp(m2 - m1)
    w1 = 1.0 / (1.0 + e2)
    gate_ref[...] = jnp.where(lane == i1, w1, 0.0) + jnp.where(lane == i2, e2 * w1, 0.0)


def _mod_spec(mod_map, k, D):
    return pl.BlockSpec((None, None, ROW_TILE, D), lambda i, *_: (k, mod_map(i), 0, 0))


def _norm_mod(x, g, modexp, mod_map, k_scale, k_shift, router=None):
    T, D = x.shape
    specs = [pl.BlockSpec((ROW_TILE, D), lambda i: (i, 0)),
             pl.BlockSpec((1, D), lambda i: (0, 0)),
             _mod_spec(mod_map, k_scale, D), _mod_spec(mod_map, k_shift, D)]
    args = [x, g.reshape(1, D), modexp, modexp]
    h_spec = pl.BlockSpec((ROW_TILE, D), lambda i: (i, 0))
    h_shape = jax.ShapeDtypeStruct((T, D), BF16)
    if router is None:
        return pl.pallas_call(
            _norm_mod_kernel, grid=(T // ROW_TILE,), in_specs=specs, out_specs=h_spec, out_shape=h_shape,
            compiler_params=_cp("parallel"), name="norm_mod")(*args)
    rw_hi, rw_lo, rb = router
    specs += [pl.BlockSpec((D, LANE), lambda i: (0, 0)), pl.BlockSpec((D, LANE), lambda i: (0, 0)),
              pl.BlockSpec((1, LANE), lambda i: (0, 0))]
    return pl.pallas_call(
        _norm_mod_router_kernel, grid=(T // ROW_TILE,), in_specs=specs,
        out_specs=[h_spec, pl.BlockSpec((ROW_TILE, LANE), lambda i: (i, 0))],
        out_shape=[h_shape, jax.ShapeDtypeStruct((T, LANE), F32)],
        compiler_params=_cp("parallel"), name="norm_mod_router")(*args, rw_hi, rw_lo, rb)


def _mm_kernel(a_ref, b_ref, o_ref):
    o_ref[...] = jnp.dot(a_ref[...], b_ref[...], preferred_element_type=F32).astype(o_ref.dtype)


def _mm(a, b, out_dtype, tm=ROW_TILE, tn=512, name="mm"):
    M, K = a.shape
    N = b.shape[1]
    return pl.pallas_call(
        _mm_kernel, grid=(M // tm, N // tn),
        in_specs=[pl.BlockSpec((tm, K), lambda i, j: (i, 0)), pl.BlockSpec((K, tn), lambda i, j: (0, j))],
        out_specs=pl.BlockSpec((tm, tn), lambda i, j: (i, j)),
        out_shape=jax.ShapeDtypeStruct((M, N), out_dtype),
        compiler_params=_cp("parallel", "parallel"), name=name)(a, b)


def _mm_res_kernel(a_ref, b_ref, x_ref, g_ref, o_ref, acc_ref):
    k = pl.program_id(2)

    @pl.when(k == 0)
    def _():
        acc_ref[...] = jnp.zeros_like(acc_ref)

    acc_ref[...] += jnp.dot(a_ref[...], b_ref[...], preferred_element_type=F32)

    @pl.when(k == pl.num_programs(2) - 1)
    def _():
        o_ref[...] = x_ref[...] + g_ref[...] * acc_ref[...]


def _mm_res(a, b, x, modexp, mod_map, k_gate, tk, tn=512, name="mm_res"):
    M, K = a.shape
    N = b.shape[1]
    tm = ROW_TILE
    return pl.pallas_call(
        _mm_res_kernel, grid=(M // tm, N // tn, K // tk),
        in_specs=[pl.BlockSpec((tm, tk), lambda i, j, k: (i, k)),
                  pl.BlockSpec((tk, tn), lambda i, j, k: (k, j)),
                  pl.BlockSpec((tm, tn), lambda i, j, k: (i, j)),
                  pl.BlockSpec((None, None, tm, tn), lambda i, j, k: (k_gate, mod_map(i), 0, j))],
        out_specs=pl.BlockSpec((tm, tn), lambda i, j, k: (i, j)),
        out_shape=jax.ShapeDtypeStruct((M, N), F32),
        scratch_shapes=[pltpu.VMEM((tm, tn), F32)],
        compiler_params=_cp("parallel", "parallel", "arbitrary"), name=name)(a, b, x, modexp)


def _gmlp_kernel(gu_ref, gv_ref, lng_ref, lnb_ref, w_ref, bias_ref, o_ref, v_ref, *, c):
    u = jax.nn.gelu(gu_ref[...])
    gv = jax.nn.gelu(gv_ref[...])
    mu = jnp.mean(gv, axis=-1, keepdims=True)
    var = jnp.mean(jnp.square(gv - mu), axis=-1, keepdims=True)
    v = (gv - mu) * lax.rsqrt(var + NORM_EPS) * lng_ref[...] + lnb_ref[...]
    v_ref[...] = v
    vb = v.astype(BF16)
    row = lax.broadcasted_iota(I32, (c, c), 0)
    col = lax.broadcasted_iota(I32, (c, c), 1)
    bias = bias_ref[...]
    for g in range(GM_GROUPS):
        w = jnp.where(row >= col, w_ref[g], 0.0).astype(BF16)
        lo, hi = g * GM_GROUP_DIM, (g + 1) * GM_GROUP_DIM
        for n in range(ROW_TILE // c):
            r0, r1 = n * c, (n + 1) * c
            sp = jnp.dot(w, vb[r0:r1, lo:hi], preferred_element_type=F32) + bias[:, g:g + 1]
            o_ref[r0:r1, lo:hi] = (u[r0:r1, lo:hi] * sp).astype(BF16)


def _gmlp(hw, row0, rows, c, ln_g, ln_b, ws, bs):
    t0 = row0 // ROW_TILE
    gu_blk = _DST['gu'][0] // GM_WIDTH
    gv_blk = _DST['gv'][0] // GM_WIDTH
    w = ws[:, :c, :c]
    bias = bs[:, :c].T
    return pl.pallas_call(
        functools.partial(_gmlp_kernel, c=c), grid=(rows // ROW_TILE,),
        in_specs=[pl.BlockSpec((ROW_TILE, GM_WIDTH), lambda i: (t0 + i, gu_blk)),
                  pl.BlockSpec((ROW_TILE, GM_WIDTH), lambda i: (t0 + i, gv_blk)),
                  pl.BlockSpec((1, GM_WIDTH), lambda i: (0, 0)),
                  pl.BlockSpec((1, GM_WIDTH), lambda i: (0, 0)),
                  pl.BlockSpec((GM_GROUPS, c, c), lambda i: (0, 0, 0)),
                  pl.BlockSpec((c, GM_GROUPS), lambda i: (0, 0))],
        out_specs=[pl.BlockSpec((ROW_TILE, GM_WIDTH), lambda i: (i, 0)),
                   pl.BlockSpec((ROW_TILE, GM_WIDTH), lambda i: (i, 0))],
        out_shape=[jax.ShapeDtypeStruct((rows, GM_WIDTH), BF16), jax.ShapeDtypeStruct((rows, GM_WIDTH), F32)],
        compiler_params=_cp("parallel"), name="gmlp")(
            hw, hw, ln_g.reshape(1, -1), ln_b.reshape(1, -1), w, bias)


def _rope_tables(pos, rot, period):
    half = rot // 2
    inv = ROPE_THETA ** (-jnp.arange(half, dtype=F32) / half)
    ang = pos.astype(F32)[:, None] * inv[None, :]
    cos, sin = jnp.cos(ang), jnp.sin(ang)
    T = pos.shape[0]
    pad = jnp.zeros((T, period - rot), F32)
    c = jnp.concatenate([cos, cos, pad + 1.0], axis=1)
    s1 = jnp.concatenate([-sin, jnp.zeros_like(sin), pad], axis=1)
    s2 = jnp.concatenate([jnp.zeros_like(sin), sin, pad], axis=1)
    rep = LANE // period
    return jnp.stack([jnp.tile(c, (1, rep)), jnp.tile(s1, (1, rep)), jnp.tile(s2, (1, rep))], axis=0)


def _rope(x, tab_ref, half):
    w = x.shape[1]
    rep = w // LANE

    def wide(t):
        return t if rep == 1 else jnp.concatenate([t] * rep, axis=1)

    xl = pltpu.roll(x, w - half, axis=1)
    xr = pltpu.roll(x, half, axis=1)
    return x * wide(tab_ref[0]) + xl * wide(tab_ref[1]) + xr * wide(tab_ref[2])


def _state_kernel(cq_ref, ckv_ref, kr_ref, dq_ref, dk_ref, qi_ref, ki_ref, fq_ref, fk_ref,
                  qg_ref, kvg_ref, wqn_ref, wqp_ref, t64_ref, t32_ref, tidx_ref,
                  ckv_o, kpe_o, dsak_o, ki_o, diffk_o, qn_o, qp_o, dsaq_o, qi_o, diffq_o):
    cqn = _rms(cq_ref[...], qg_ref[...]).astype(BF16)
    qn_o[...] = (jnp.dot(cqn, wqn_ref[...], preferred_element_type=F32) * MLA_QSCALE).astype(BF16)
    qp = jnp.dot(cqn, wqp_ref[...], preferred_element_type=F32)
    qp_o[...] = (_rope(qp, t32_ref, MLA_ROPE // 2) * MLA_QSCALE).astype(BF16)
    ckv_o[...] = _rms(ckv_ref[...], kvg_ref[...])
    kpe_o[...] = _rope(kr_ref[...], t32_ref, MLA_ROPE // 2)
    dsaq_o[...] = (_rope(dq_ref[...], t64_ref, DSA_ROT // 2) * DSA_QSCALE).astype(BF16)
    dsak_o[...] = _rope(dk_ref[...], t64_ref, DSA_ROT // 2)
    qi_o[...] = _rope(qi_ref[...], tidx_ref, IDX_ROT // 2).astype(BF16)
    ki_o[...] = _rope(ki_ref[...], tidx_ref, IDX_ROT // 2)
    diffq_o[...] = (_rope(fq_ref[...], t64_ref, DIFF_ROT // 2) * DIFF_QSCALE).astype(BF16)
    diffk_o[...] = _rope(fk_ref[...], t64_ref, DIFF_ROT // 2)


def _state(hw, q_norm_g, kv_norm_g, w_uq_nope, w_uq_pe, t64, t32, tidx):
    T = hw.shape[0]

    def col(name):
        off, w = _DST[name]
        return pl.BlockSpec((ROW_TILE, w), lambda i: (i, off // w))

    def full(shape):
        return pl.BlockSpec(shape, lambda i: (0,) * len(shape))

    def tab():
        return pl.BlockSpec((3, ROW_TILE, LANE), lambda i: (0, i, 0))

    def out(w, dt):
        return pl.BlockSpec((ROW_TILE, w), lambda i: (i, 0)), jax.ShapeDtypeStruct((T, w), dt)

    outs = [out(MLA_KV_LORA, F32), out(LANE, F32), out(512, F32), out(LANE, F32), out(512, F32),
            out(512, BF16), out(MLA_HEADS * MLA_ROPE, BF16), out(512, BF16), out(IDX_HEADS * IDX_DIM, BF16),
            out(512, BF16)]
    return pl.pallas_call(
        _state_kernel, grid=(T // ROW_TILE,),
        in_specs=[col('cq'), col('ckv'), col('krope'), col('dq'), col('dk'), col('qi'), col('ki'),
                  col('fq'), col('fk'),
                  full((1, MLA_Q_LORA)), full((1, MLA_KV_LORA)),
                  full((MLA_Q_LORA, MLA_HEADS * MLA_NOPE)), full((MLA_Q_LORA, MLA_HEADS * MLA_ROPE)),
                  tab(), tab(), tab()],
        out_specs=[o[0] for o in outs], out_shape=[o[1] for o in outs],
        compiler_params=_cp("parallel"), name="mixer_state")(
            hw, hw, hw, hw, hw, hw, hw, hw, hw,
            q_norm_g.reshape(1, -1), kv_norm_g.reshape(1, -1), w_uq_nope, w_uq_pe, t64, t32, tidx)


def _block_offset(j):
    return j * KV_BLOCK if isinstance(j, int) else pl.multiple_of(j * KV_BLOCK, KV_BLOCK)


def _softmax_steps(scores, values, carries):
    stats = []
    for s, (m, l, _) in zip(scores, carries):
        m_new = jnp.maximum(m, jnp.max(s, axis=0, keepdims=True))
        alpha = jnp.exp2(m - m_new)
        p = jnp.exp2(s - m_new)
        stats.append((m_new, alpha * l + jnp.sum(p, axis=0, keepdims=True), alpha, p.astype(BF16)))
    pvs = [jnp.dot(vt, st[3], preferred_element_type=F32) for vt, st in zip(values, stats)]
    return tuple((st[0], st[1], st[2] * c[2] + pv) for st, c, pv in zip(stats, carries, pvs))


def _softmax_init(dv, tq):
    return jnp.full((1, tq), NEG_INF, F32), jnp.zeros((1, tq), F32), jnp.zeros((dv, tq), F32)


def _attend_blocks(n_blocks, qk, values, carries, mask_last):
    carries = lax.fori_loop(0, n_blocks - 1, lambda j, c: _softmax_steps(qk(j), values(j), c), carries)
    scores = qk(n_blocks - 1)
    if mask_last is not None:
        scores = mask_last(scores)
    return _softmax_steps(scores, values(n_blocks - 1), carries)


def _pair_loop(lo, hi, step, carry):
    n2 = (hi - lo) // 2

    def two(t, c):
        j = lo + 2 * t
        return step(j + 1, step(j, c))

    carry = lax.fori_loop(0, n2, two, carry)
    return lax.fori_loop(lo + 2 * n2, hi, step, carry)


def _flash_kernel(*refs, tq, nkb, causal, s_valid, k_lanes, v_rows, dv, diff_layer):
    if diff_layer is None:
        qt_ref, k_ref, vt_ref, o_ref = refs
    else:
        qt_ref, k_ref, vt_ref, lam_ref, sg_ref, o_ref = refs
    i = pl.program_id(2)
    padded = s_valid < nkb * KV_BLOCK

    hg = len(k_lanes)
    n_blocks = i * tq // KV_BLOCK + 1 if causal else nkb

    def qk(j):
        off = _block_offset(j)
        return tuple(jnp.dot(k_ref[pl.ds(off, KV_BLOCK), k0:k0 + LANE], qt_ref[hl], preferred_element_type=F32)
                     for hl, k0 in enumerate(k_lanes))

    def values(j):
        return [vt_ref[j, v0:v0 + dv, :] for v0 in v_rows]

    def mask_last(scores):
        kpos = (n_blocks - 1) * KV_BLOCK + lax.broadcasted_iota(I32, (KV_BLOCK, tq), 0)
        if causal:
            qpos = i * tq + lax.broadcasted_iota(I32, (KV_BLOCK, tq), 1)
            ok = kpos // CHUNK <= qpos // CHUNK
        else:
            ok = kpos < s_valid
        return [jnp.where(ok, s, NEG_INF) for s in scores]

    carries = _attend_blocks(n_blocks, qk, values, tuple(_softmax_init(dv, tq) for _ in range(hg)),
                             mask_last if (causal or padded) else None)
    heads = [acc / l for _, l, acc in carries]

    if diff_layer is None:
        for hl in range(hg):
            o_ref[v_rows[hl]:v_rows[hl] + dv, :] = heads[hl].astype(o_ref.dtype)
    else:
        lam_init = 0.8 - 0.6 * math.exp(-0.3 * diff_layer)
        lp = lam_ref[...]
        lam = (jnp.exp(jnp.sum(lp[0:1] * lp[1:2], axis=-1, keepdims=True))
               - jnp.exp(jnp.sum(lp[2:3] * lp[3:4], axis=-1, keepdims=True)) + lam_init)
        for h2 in range(hg // 2):
            o = heads[2 * h2] - lam * heads[2 * h2 + 1]
            y = o * lax.rsqrt(jnp.mean(o * o, axis=0, keepdims=True) + NORM_EPS) * sg_ref[...]
            o_ref[v_rows[2 * h2]:v_rows[2 * h2] + dv, :] = (y * (1.0 - lam_init)).astype(o_ref.dtype)


def _flash(qt, k, vt, *, tq, causal, s_valid, dv, k_width, k_lanes, diff=None, name="flash"):
    B, G, _, Tq = qt.shape
    nkb = vt.shape[1]
    S_pad = nkb * KV_BLOCK
    assert not causal or (tq <= KV_BLOCK and KV_BLOCK % tq == 0)
    hg = len(k_lanes)
    n_grp = G // hg
    share = hg * dv // MXU_WIDTH
    v_rows = tuple(hl // share * dv for hl in range(hg))
    in_specs = [pl.BlockSpec((None, hg, LANE, tq), lambda b, g, i: (b, g, 0, i)),
                pl.BlockSpec((None, S_pad, k_width), lambda b, g, i: (b, 0, g)),
                pl.BlockSpec((None, nkb, MXU_WIDTH, KV_BLOCK), lambda b, g, i: (b, 0, g, 0))]
    args = [qt, k, vt]
    diff_layer = None
    if diff is not None:
        lam_params, subln_g, diff_layer = diff
        in_specs += [pl.BlockSpec(lam_params.shape, lambda b, g, i: (0, 0)),
                     pl.BlockSpec((dv, tq), lambda b, g, i: (0, 0))]
        args += [lam_params, jnp.broadcast_to(subln_g[:, None], (dv, tq))]
    kern = functools.partial(_flash_kernel, tq=tq, nkb=nkb, causal=causal, s_valid=s_valid,
                             k_lanes=k_lanes, v_rows=v_rows, dv=dv, diff_layer=diff_layer)
    return pl.pallas_call(
        kern, grid=(B, n_grp, Tq // tq), in_specs=in_specs,
        out_specs=pl.BlockSpec((None, MXU_WIDTH, tq), lambda b, g, i: (b, g, i)),
        out_shape=jax.ShapeDtypeStruct((B, n_grp * MXU_WIDTH, Tq), BF16),
        compiler_params=_cp("parallel", "parallel", "parallel"), name=name)(*args)


def _dsa_kernel(qit_ref, wit_ref, ki_ref, qt_ref, k_ref, vt_ref, o_ref, key_ref,
                *, tq, nkb, causal, s_valid, n_sel):
    i = pl.program_id(1)
    tkb = KV_BLOCK
    padded = s_valid < nkb * tkb
    nvis = jnp.minimum((i * tq + tq + tkb - 1) // tkb, nkb) if causal else nkb
    wit = wit_ref[...] * IDX_SCALE

    def kpos(j):
        return j * tkb + lax.broadcasted_iota(I32, (tkb, tq), 0)

    def score_block(j, masked):
        kib = ki_ref[pl.ds(_block_offset(j), tkb), :]
        rs = [jnp.dot(kib, qit_ref[h], preferred_element_type=F32) for h in range(IDX_HEADS)]
        sc = jnp.zeros((tkb, tq), F32)
        for h in range(IDX_HEADS):
            sc = sc + jnp.maximum(rs[h], 0.0) * wit[h:h + 1, :]
        sc = sc + 0.0
        bits = pltpu.bitcast(sc, I32)
        key = jnp.where(bits < 0, bits ^ 0x7FFFFFFF, bits)
        if masked and causal:
            qpos = i * tq + lax.broadcasted_iota(I32, (tkb, tq), 1)
            key = jnp.where(kpos(j) // CHUNK <= qpos // CHUNK, key, INT_MIN)
        elif masked:
            key = jnp.where(kpos(j) < s_valid, key, INT_MIN)
        key_ref[j] = key

    lax.fori_loop(0, nvis - 1, lambda j, c: (score_block(j, False), c)[1], 0)
    score_block(nvis - 1, causal or padded)

    def count(pred):
        def body(j, cnt):
            hit = jnp.where(pred(key_ref[j], j), 1.0, 0.0)
            part = hit[0:8]
            for r in range(1, tkb // 8):
                part = part + hit[r * 8:(r + 1) * 8]
            return cnt + part
        cnt = _pair_loop(0, nvis, body, jnp.zeros((8, tq), F32))
        return jnp.sum(cnt, axis=0, keepdims=True)

    def bit_step(it, t):
        cand = t + lax.shift_left(jnp.int32(1), 31 - it)
        cnt = count(lambda k, j: k >= cand)
        return jnp.where(cnt >= n_sel, cand, t)

    thr = lax.fori_loop(0, 32, bit_step, jnp.full((1, tq), INT_MIN, I32))
    cnt_ge = count(lambda k, j: k >= thr)
    need_tie = jnp.logical_and(cnt_ge > n_sel, thr > INT_MIN)
    any_tie = jnp.max(jnp.where(need_tie, 1.0, 0.0)) > 0.0

    def write_bias(sel_of):
        def wr(j, _):
            key_ref[j] = pltpu.bitcast(jnp.where(sel_of(key_ref[j], j), 0.0, NEG_INF), I32)
            return 0
        lax.fori_loop(0, nvis, wr, 0)

    @pl.when(jnp.logical_not(any_tie))
    def _():
        tsel = jnp.maximum(thr, INT_MIN + 1)
        write_bias(lambda k, j: k >= tsel)

    @pl.when(any_tie)
    def _():
        cnt_gt = count(lambda k, j: k > thr)

        def idx_step(it, x):
            cand = x + lax.shift_left(jnp.int32(1), 14 - it)
            g = cnt_gt + count(lambda k, j: jnp.logical_and(k == thr, kpos(j) < cand))
            return jnp.where(g < n_sel, cand, x)

        xj = lax.fori_loop(0, 15, idx_step, jnp.zeros((1, tq), I32))
        jmax = jnp.where(need_tie, xj, jnp.where(thr == INT_MIN, -1, 2 ** 30))
        write_bias(lambda k, j: jnp.logical_or(k > thr, jnp.logical_and(k == thr, kpos(j) <= jmax)))

    def qk(j):
        off = _block_offset(j)
        bias = pltpu.bitcast(key_ref[j], F32)
        return tuple(jnp.dot(k_ref[pl.ds(off, tkb), hl // 2 * LANE:(hl // 2 + 1) * LANE], qt_ref[hl],
                             preferred_element_type=F32) + bias for hl in range(DSA_HEADS))

    def values(j):
        return [vt_ref[j, hl * DSA_HEAD_DIM:(hl + 1) * DSA_HEAD_DIM, :] for hl in range(DSA_HEADS)]

    carries = _attend_blocks(nvis, qk, values,
                             tuple(_softmax_init(DSA_HEAD_DIM, tq) for _ in range(DSA_HEADS)), None)
    for hl, (_, l, acc) in enumerate(carries):
        o_ref[hl * DSA_HEAD_DIM:(hl + 1) * DSA_HEAD_DIM, :] = (acc / l).astype(o_ref.dtype)


def _dsa(qit, wit, ki, qt, k, vt, *, tq, causal, s_valid, n_sel, name="dsa"):
    B, H, _, Tq = qt.shape
    nkb = vt.shape[1]
    S_pad = nkb * KV_BLOCK
    assert not causal or (tq <= KV_BLOCK and KV_BLOCK % tq == 0)
    W = H * DSA_HEAD_DIM
    kern = functools.partial(_dsa_kernel, tq=tq, nkb=nkb, causal=causal, s_valid=s_valid, n_sel=n_sel)
    return pl.pallas_call(
        kern, grid=(B, Tq // tq),
        in_specs=[pl.BlockSpec((None, IDX_HEADS, LANE, tq), lambda b, i: (b, 0, 0, i)),
                  pl.BlockSpec((None, IDX_HEADS, tq), lambda b, i: (b, 0, i)),
                  pl.BlockSpec((None, S_pad, LANE), lambda b, i: (b, 0, 0)),
                  pl.BlockSpec((None, H, LANE, tq), lambda b, i: (b, 0, 0, i)),
                  pl.BlockSpec((None, S_pad, W), lambda b, i: (b, 0, 0)),
                  pl.BlockSpec((None, nkb, W, KV_BLOCK), lambda b, i: (b, 0, 0, 0))],
        out_specs=pl.BlockSpec((None, W, tq), lambda b, i: (b, 0, i)),
        out_shape=jax.ShapeDtypeStruct((B, W, Tq), BF16),
        scratch_shapes=[pltpu.VMEM((nkb, KV_BLOCK, tq), I32)],
        compiler_params=_cp("parallel", "parallel"), name=name)(qit, wit, ki, qt, k, vt)


def _merge_kernel(h_ref, oa_ref, ob_ref, oc_ref, od_ref, wg0, wg1, wg2, wg3, bg0, bg1, bg2, bg3, wb_ref, o_ref):
    h = h_ref[...]
    acc = None
    for n, (o_r, wg, bg) in enumerate(((oa_ref, wg0, bg0), (ob_ref, wg1, bg1), (oc_ref, wg2, bg2),
                                       (od_ref, wg3, bg3))):
        gate = jax.nn.sigmoid(jnp.dot(h, wg[...], preferred_element_type=F32) + bg[...])
        term = gate * jnp.dot(o_r[...], wb_ref[n], preferred_element_type=F32)
        acc = term if acc is None else acc + term
    o_ref[...] = acc.astype(BF16)


def _merge(h, branches, w_gate, b_gate, w_branch, tn=512):
    T, D = h.shape
    nd = D // tn
    b_gate = b_gate.reshape(1, -1)
    specs = [pl.BlockSpec((ROW_TILE, D), lambda i, j: (i, 0))]
    specs += [pl.BlockSpec((ROW_TILE, BRANCH_WIDTH), lambda i, j: (i, 0))] * N_BRANCH
    specs += [pl.BlockSpec((D, tn), lambda i, j, n=n: (0, n * nd + j)) for n in range(N_BRANCH)]
    specs += [pl.BlockSpec((1, tn), lambda i, j, n=n: (0, n * nd + j)) for n in range(N_BRANCH)]
    specs += [pl.BlockSpec((N_BRANCH, BRANCH_WIDTH, tn), lambda i, j: (0, 0, j))]
    return pl.pallas_call(
        _merge_kernel, grid=(T // ROW_TILE, nd), in_specs=specs,
        out_specs=pl.BlockSpec((ROW_TILE, tn), lambda i, j: (i, j)),
        out_shape=jax.ShapeDtypeStruct((T, D), BF16),
        compiler_params=_cp("parallel", "parallel"), name="merge")(
            h, *branches, w_gate, w_gate, w_gate, w_gate, b_gate, b_gate, b_gate, b_gate, w_branch)


def _ffn_up_kernel(h_ref, w1_ref, w3_ref, o_ref):
    h = h_ref[...]
    a = jnp.dot(h, w1_ref[...], preferred_element_type=F32)
    b = jnp.dot(h, w3_ref[...], preferred_element_type=F32)
    o_ref[...] = (a * jax.nn.sigmoid(a) * b).astype(BF16)


def _moe_up_kernel(h_ref, w1_ref, w3_ref, gate_ref, o_ref):
    e = pl.program_id(1)
    h = h_ref[...]
    a = jnp.dot(h, w1_ref[...], preferred_element_type=F32)
    b = jnp.dot(h, w3_ref[...], preferred_element_type=F32)
    gate = gate_ref[...]
    lane = lax.broadcasted_iota(I32, gate.shape, 1)
    ge = jnp.sum(jnp.where(lane == e, gate, 0.0), axis=-1, keepdims=True)
    o_ref[...] = (a * jax.nn.sigmoid(a) * b * ge).astype(BF16)


def _ffn_up(h, w1, w3, tn=512):
    T, D = h.shape
    F = w1.shape[1]
    return pl.pallas_call(
        _ffn_up_kernel, grid=(T // ROW_TILE, F // tn),
        in_specs=[pl.BlockSpec((ROW_TILE, D), lambda i, j: (i, 0)),
                  pl.BlockSpec((D, tn), lambda i, j: (0, j)), pl.BlockSpec((D, tn), lambda i, j: (0, j))],
        out_specs=pl.BlockSpec((ROW_TILE, tn), lambda i, j: (i, j)),
        out_shape=jax.ShapeDtypeStruct((T, F), BF16),
        compiler_params=_cp("parallel", "parallel"), name="ffn_up")(h, w1, w3)


def _moe_up(h, w1, w3, gate):
    T, D = h.shape
    E, _, F = w1.shape
    return pl.pallas_call(
        _moe_up_kernel, grid=(T // ROW_TILE, E),
        in_specs=[pl.BlockSpec((ROW_TILE, D), lambda i, e: (i, 0)),
                  pl.BlockSpec((None, D, F), lambda i, e: (e, 0, 0)),
                  pl.BlockSpec((None, D, F), lambda i, e: (e, 0, 0)),
                  pl.BlockSpec((ROW_TILE, LANE), lambda i, e: (i, 0))],
        out_specs=pl.BlockSpec((ROW_TILE, F), lambda i, e: (i, e)),
        out_shape=jax.ShapeDtypeStruct((T, E * F), BF16),
        compiler_params=_cp("parallel", "parallel"), name="moe_up")(h, w1, w3, gate)


def _final_norm_kernel(x_ref, g_ref, o_ref):
    o_ref[...] = _rms(x_ref[...], g_ref[...])


def _final_norm(x, g):
    T, D = x.shape
    return pl.pallas_call(
        _final_norm_kernel, grid=(T // ROW_TILE,),
        in_specs=[pl.BlockSpec((ROW_TILE, D), lambda i: (i, 0)), pl.BlockSpec((1, D), lambda i: (0, 0))],
        out_specs=pl.BlockSpec((ROW_TILE, D), lambda i: (i, 0)),
        out_shape=jax.ShapeDtypeStruct((T, D), F32),
        compiler_params=_cp("parallel"), name="final_norm")(x, g.reshape(1, D))


def _pad_rows(a, s_pad):
    pad = s_pad - a.shape[1]
    if pad == 0:
        return a
    return jnp.pad(a, ((0, 0), (0, pad)) + ((0, 0),) * (a.ndim - 2))


def _queries_t(q, n_heads, slots, tq_pad):
    B, T, W = q.shape
    d = W // n_heads
    qt = q.reshape(B, T, n_heads, d).transpose(0, 2, 3, 1)
    out = jnp.pad(qt, ((0, 0), (0, 0), (0, LANE - d), (0, 0)))
    for s in range(1, slots):
        shifted = jnp.pad(qt, ((0, 0), (0, 0), (s * d, LANE - d - s * d), (0, 0)))
        out = jnp.where((jnp.arange(n_heads) % slots == s)[None, :, None, None], shifted, out)
    reps = -(-tq_pad // T)
    return jnp.concatenate([out] * reps, axis=-1)[..., :tq_pad] if reps > 1 else out


def _values_t(v):
    B, S, W = v.shape
    return v.reshape(B, S // KV_BLOCK, KV_BLOCK, W).transpose(0, 1, 3, 2)


def _rows_out(ot, T):
    B, W, _ = ot.shape
    return ot[:, :, :T].transpose(0, 2, 1).reshape(B * T, W)


def _pad_in_weight(w_in):
    D = w_in.shape[0]
    out = jnp.zeros((D, IN_PAD), BF16)
    for name in _ORDER:
        s0, sw = _SRC[name]
        d0, _ = _DST[name]
        out = lax.dynamic_update_slice(out, w_in[:, s0:s0 + sw].astype(BF16), (0, d0))
    return out


def _col(hw, name, width=None):
    off, w = _DST[name]
    return hw[:, off:off + (width or w)]


def _mixers(l, hw, st, caches, lp, geom):
    n_p, B_p, T_p, B_s, T_s, past = geom
    ckv_n, kpe, dsa_k, ki, diff_k, q_nope, q_pe, dsa_q, qi, diff_q = st
    wi = _col(hw, 'wi')
    dsa_v = _col(hw, 'dv')
    diff_v = _col(hw, 'fv')

    def split(a):
        W = a.shape[1]
        return a[:n_p].reshape(B_p, T_p, W), a[n_p:].reshape(B_s, T_s, W)

    o_a_p, v_p = _gmlp(hw, 0, n_p, min(T_p, GM_CHUNK), lp['gm_ln_g'], lp['gm_ln_b'], lp['gm_ws'], lp['gm_bs'])
    o_a_s, v_s = _gmlp(hw, n_p, hw.shape[0] - n_p, min(T_s, GM_CHUNK), lp['gm_ln_g'], lp['gm_ln_b'],
                       lp['gm_ws'], lp['gm_bs'])
    c_p, c_s = min(T_p, GM_CHUNK), min(T_s, GM_CHUNK)
    gm_v_p = v_p.reshape(B_p, T_p, GM_WIDTH)[:, T_p - c_p:]
    gm_v_s = v_s.reshape(B_s, T_s, GM_WIDTH)[:, T_s - c_s:]

    outs = {}
    for grp in ('p', 's'):
        pi = 0 if grp == 'p' else 1
        B, T = (B_p, T_p) if grp == 'p' else (B_s, T_s)

        def g(a, pi=pi):
            return split(a)[pi]

        if grp == 'p':
            S = s_pad = T
            tq = min(Q_TILE, T)
            tq_pad = T
            causal = True

            def ext(new, cache):
                return new
        else:
            S = past + T
            s_pad = -(-S // KV_BLOCK) * KV_BLOCK
            tq = tq_pad = -(-T // LANE) * LANE
            causal = False

            def ext(new, cache, s_pad=s_pad):
                return _pad_rows(jnp.concatenate([cache.reshape(cache.shape[:2] + (-1,)), new], axis=1), s_pad)

        cm_ckv, cm_kr, cd_k, cd_v, cd_ki, cf_k, cf_v = caches

        all_c = ext(g(ckv_n), cm_ckv).astype(BF16)
        all_pe = ext(g(kpe)[..., :MLA_ROPE], cm_kr).astype(BF16)
        kv = _mm(all_c.reshape(B * s_pad, MLA_KV_LORA), lp['w_ukv'], BF16,
                 tm=ROW_TILE if (B * s_pad) % ROW_TILE == 0 else LANE, name="mla_kv")
        kv = kv.reshape(B, s_pad, 2 * MLA_HEADS * MLA_NOPE)
        k_nope = kv[..., :MLA_HEADS * MLA_NOPE].reshape(B, s_pad, MLA_HEADS, MLA_NOPE)
        k_mla = jnp.concatenate(
            [k_nope, jnp.broadcast_to(all_pe[:, :, None, :], (B, s_pad, MLA_HEADS, MLA_ROPE)),
             jnp.zeros((B, s_pad, MLA_HEADS, LANE - MLA_NOPE - MLA_ROPE), BF16)], axis=-1)
        q_mla = jnp.concatenate([g(q_nope).reshape(B, T, MLA_HEADS, MLA_NOPE),
                                 g(q_pe).reshape(B, T, MLA_HEADS, MLA_ROPE)], axis=-1)
        o_b = _flash(_queries_t(q_mla.reshape(B, T, -1), MLA_HEADS, 1, tq_pad),
                     k_mla.reshape(B, s_pad, MLA_HEADS * LANE), _values_t(kv[..., MLA_HEADS * MLA_NOPE:]),
                     tq=tq, causal=causal, s_valid=S, dv=MLA_V, k_width=HEAD_GROUP * LANE,
                     k_lanes=tuple(h * LANE for h in range(HEAD_GROUP)), name="mla_attn")

        k_all = ext(g(dsa_k), cd_k).astype(BF16)
        v_all = ext(g(dsa_v), cd_v).astype(BF16)
        ki_all = ext(g(ki)[..., :IDX_DIM], cd_ki).astype(BF16)
        ki_all = jnp.pad(ki_all, ((0, 0), (0, 0), (0, LANE - IDX_DIM)))
        wit = _queries_t(g(wi)[..., :IDX_HEADS], IDX_HEADS, 1, tq_pad)[:, :, 0, :]
        o_c = _dsa(_queries_t(g(qi), IDX_HEADS, 1, tq_pad), wit, ki_all,
                   _queries_t(g(dsa_q), DSA_HEADS, 2, tq_pad), k_all, _values_t(v_all),
                   tq=tq, causal=causal, s_valid=S, n_sel=min(DSA_TOPK, S // 4))

        fk_all = ext(g(diff_k), cf_k).astype(BF16)
        fv_all = ext(g(diff_v), cf_v).astype(BF16)
        o_d = _flash(_queries_t(g(diff_q), 2 * DIFF_HEADS, 2, tq_pad), fk_all, _values_t(fv_all),
                     tq=tq, causal=causal, s_valid=S, dv=DIFF_V_DIM, k_width=MXU_WIDTH,
                     k_lanes=(0, 0, LANE, LANE),
                     diff=(lp['diff_lambda'], lp['diff_subln_g'], l), name="diff_attn")
        outs[grp] = (_rows_out(o_b, T), _rows_out(o_c, T), _rows_out(o_d, T))

    o_a = jnp.concatenate([o_a_p, o_a_s], axis=0)
    o_b, o_c, o_d = (jnp.concatenate([outs['p'][k], outs['s'][k]], axis=0) for k in range(3))

    def state(grp):
        pi = 0 if grp == 'p' else 1
        B, T = (B_p, T_p) if grp == 'p' else (B_s, T_s)

        def g(a):
            return split(a)[pi]
        return (gm_v_p if grp == 'p' else gm_v_s,
                g(ckv_n), g(kpe)[..., :MLA_ROPE],
                g(dsa_k).reshape(B, T, DSA_HEADS, DSA_HEAD_DIM), g(dsa_v).reshape(B, T, DSA_HEADS, DSA_HEAD_DIM),
                g(ki)[..., :IDX_DIM],
                g(diff_k).reshape(B, T, DIFF_HEADS, 2, DIFF_QK_DIM), g(diff_v).reshape(B, T, DIFF_HEADS, DIFF_V_DIM))

    return (o_a, o_b, o_c, o_d), state('p'), state('s')


def kernel(x_prompt, x_sample, c_prompt, c_sample, cache_mla_ckv, cache_mla_krope, cache_dsa_k, cache_dsa_v, cache_dsa_kidx, cache_diff_k, cache_diff_v, ada_w, ada_b, norm_mix_g, norm_ff_g, w_in, w_gate, b_gate, gm_ln_g, gm_ln_b, gm_ws, gm_bs, mla_q_norm_g, mla_kv_norm_g, mla_w_uq, mla_w_ukv, diff_lambda, diff_subln_g, w_branch, w_out, ffn_w1, ffn_w3, ffn_w2, moe_router_w, moe_router_b, moe_w1, moe_w3, moe_w2, final_norm_g):
    B_p, T_p, D = x_prompt.shape
    B_s, T_s, _ = x_sample.shape
    depth = ada_w.shape[0]
    past = cache_mla_ckv.shape[2]
    n_p, n_s = B_p * T_p, B_s * T_s
    assert T_p % ROW_TILE == 0 and n_s % ROW_TILE == 0 and ROW_TILE % T_s == 0
    assert past % CHUNK == 0 and T_s <= CHUNK
    n_p_tiles, tiles_per_b = n_p // ROW_TILE, T_p // ROW_TILE
    geom = (n_p, B_p, T_p, B_s, T_s, past)

    def mod_map(i):
        return jnp.where(i < n_p_tiles, i // tiles_per_b, B_p + i - n_p_tiles)

    x = jnp.concatenate([x_prompt.reshape(n_p, D), x_sample.reshape(n_s, D)], axis=0)

    n_c = B_p + B_s
    c_rows = jnp.concatenate([c_prompt, c_sample, jnp.zeros((-n_c % 8, D), F32)], axis=0)
    mod = _ada(c_rows, ada_w, ada_b)
    mod = mod[:, :n_c].reshape(depth, n_c, 6, D).transpose(0, 2, 1, 3)
    mod_p = jnp.broadcast_to(mod[:, :, :B_p, None, :], (depth, 6, B_p, ROW_TILE, D))
    mod_s = jnp.broadcast_to(mod[:, :, B_p:, None, :], (depth, 6, B_s, T_s, D)).reshape(
        depth, 6, n_s // ROW_TILE, ROW_TILE, D)
    modexp = jnp.concatenate([mod_p, mod_s], axis=2)

    pos = jnp.concatenate([jnp.tile(jnp.arange(T_p, dtype=I32), B_p),
                           jnp.tile(past + jnp.arange(T_s, dtype=I32), B_s)])
    t64 = _rope_tables(pos, DSA_ROT, DSA_HEAD_DIM)
    t32 = _rope_tables(pos, MLA_ROPE, MLA_ROPE)
    tidx = _rope_tables(pos, IDX_ROT, IDX_DIM)

    states_p, states_s = [], []
    for l in range(depth):
        uq = mla_w_uq[l].reshape(MLA_Q_LORA, MLA_HEADS, MLA_NOPE + MLA_ROPE)
        ukv = mla_w_ukv[l].reshape(MLA_KV_LORA, MLA_HEADS, MLA_NOPE + MLA_V)
        lp = {
            'gm_ln_g': gm_ln_g[l], 'gm_ln_b': gm_ln_b[l], 'gm_ws': gm_ws[l], 'gm_bs': gm_bs[l],
            'w_ukv': jnp.concatenate([ukv[..., :MLA_NOPE].reshape(MLA_KV_LORA, -1),
                                      ukv[..., MLA_NOPE:].reshape(MLA_KV_LORA, -1)], axis=1).astype(BF16),
            'diff_lambda': diff_lambda[l], 'diff_subln_g': diff_subln_g[l],
        }
        h = _norm_mod(x, norm_mix_g[l], modexp[l], mod_map, 1, 0)
        hw = _mm(h, _pad_in_weight(w_in[l]), F32, name="in_proj")
        st = _state(hw, mla_q_norm_g[l], mla_kv_norm_g[l],
                    uq[..., :MLA_NOPE].reshape(MLA_Q_LORA, -1).astype(BF16),
                    uq[..., MLA_NOPE:].reshape(MLA_Q_LORA, -1).astype(BF16), t64, t32, tidx)
        caches = (cache_mla_ckv[l], cache_mla_krope[l], cache_dsa_k[l], cache_dsa_v[l], cache_dsa_kidx[l],
                  cache_diff_k[l], cache_diff_v[l])
        branches, st_p, st_s = _mixers(l, hw, st, caches, lp, geom)
        states_p.append(st_p)
        states_s.append(st_s)
        merged = _merge(h, branches, w_gate[l].astype(BF16), b_gate[l], w_branch[l].astype(BF16))
        x = _mm_res(merged, w_out[l].astype(BF16), x, modexp[l], mod_map, 2, tk=D, name="out_proj")

        j = l // 2
        if l % 2 == 0:
            h2 = _norm_mod(x, norm_ff_g[l], modexp[l], mod_map, 4, 3)
            act = _ffn_up(h2, ffn_w1[j].astype(BF16), ffn_w3[j].astype(BF16))
            w2 = ffn_w2[j].astype(BF16)
        else:
            rw = jnp.zeros((D, LANE), F32).at[:, :N_EXPERTS].set(moe_router_w[j])
            rw_hi = rw.astype(BF16)
            rw_lo = (rw - rw_hi.astype(F32)).astype(BF16)
            rb = jnp.zeros((1, LANE), F32).at[0, :N_EXPERTS].set(moe_router_b[j])
            h2, gate = _norm_mod(x, norm_ff_g[l], modexp[l], mod_map, 4, 3, router=(rw_hi, rw_lo, rb))
            act = _moe_up(h2, moe_w1[j].astype(BF16), moe_w3[j].astype(BF16), gate)
            w2 = moe_w2[j].astype(BF16).reshape(-1, D)
        x = _mm_res(act, w2, x, modexp[l], mod_map, 5, tk=w2.shape[0] // 4, name="ffn_down")

    y = _final_norm(x, final_norm_g)
    outs = [y[:n_p].reshape(B_p, T_p, D), y[n_p:].reshape(B_s, T_s, D)]
    for k in range(8):
        outs.append(jnp.stack([s[k] for s in states_p], axis=0))
        outs.append(jnp.stack([s[k] for s in states_s], axis=0))
    return tuple(outs)
```

```python
import functools
import math

import jax
import jax.numpy as jnp
from jax import lax
from jax.experimental import pallas as pl
from jax.experimental.pallas import tpu as pltpu

F32 = jnp.float32
BF16 = jnp.bfloat16
I32 = jnp.int32

CHUNK = 64
ROPE_THETA = 500000.0
NORM_EPS = 1e-6
NEG_INF = -1e30
N_BRANCH = 4
BRANCH_WIDTH = 512
GM_CHUNK = 128
GM_GROUPS = 4
GM_WIDTH = 512
GM_GROUP_DIM = GM_WIDTH // GM_GROUPS
MLA_HEADS = 8
MLA_Q_LORA = 512
MLA_KV_LORA = 256
MLA_NOPE = 64
MLA_ROPE = 32
MLA_V = 64
DSA_HEADS = 8
DSA_HEAD_DIM = 64
DSA_ROT = DSA_HEAD_DIM // 4
IDX_HEADS = 8
IDX_DIM = 32
IDX_ROT = IDX_DIM // 4
IDX_SCALE = (IDX_DIM ** -0.5) * (IDX_HEADS ** -0.5)
DSA_TOPK = 256
DIFF_HEADS = 4
DIFF_QK_DIM = 64
DIFF_V_DIM = 128
DIFF_ROT = DIFF_QK_DIM // 4
N_EXPERTS = 8

LANE = 128
MXU_WIDTH = 256
VMEM_LIMIT_BYTES = 56 * 1024 * 1024

ROW_TILE = 512
HEAD_GROUP = MXU_WIDTH // DSA_HEAD_DIM
LOG2E = 1.4426950408889634
INT_MIN = -2 ** 31
MLA_QSCALE = (MLA_NOPE + MLA_ROPE) ** -0.5 * LOG2E
DSA_QSCALE = DSA_HEAD_DIM ** -0.5 * LOG2E
DIFF_QSCALE = DIFF_QK_DIM ** -0.5 * LOG2E
KV_BLOCK = 256
Q_TILE = MXU_WIDTH

_SPLIT = (
    ('gu', GM_WIDTH), ('gv', GM_WIDTH), ('cq', MLA_Q_LORA), ('ckv', MLA_KV_LORA), ('krope', MLA_ROPE),
    ('dq', 512), ('dk', 512), ('dv', 512), ('qi', IDX_HEADS * IDX_DIM), ('ki', IDX_DIM), ('wi', IDX_HEADS),
    ('fq', 512), ('fk', 512), ('fv', 512),
)
_ORDER = ('gu', 'gv', 'cq', 'dq', 'dk', 'dv', 'fq', 'fk', 'fv', 'ckv', 'qi', 'krope', 'ki', 'wi')


def _in_layout():
    src, acc = {}, 0
    for name, w in _SPLIT:
        src[name] = (acc, w)
        acc += w
    dst, off = {}, 0
    for name in _ORDER:
        w = src[name][1]
        pw = -(-w // LANE) * LANE
        dst[name] = (off, pw)
        off += pw
    total = -(-off // ROW_TILE) * ROW_TILE
    return src, dst, total


_SRC, _DST, IN_PAD = _in_layout()


def _cp(*sem):
    return pltpu.CompilerParams(dimension_semantics=sem, vmem_limit_bytes=VMEM_LIMIT_BYTES)


def _rms(x, g):
    return x * lax.rsqrt(jnp.mean(x * x, axis=-1, keepdims=True) + NORM_EPS) * g


def _ada_kernel(c_ref, w_ref, b_ref, o_ref):
    c = c_ref[...]
    a = (c * jax.nn.sigmoid(c)).astype(BF16)
    o_ref[...] = jnp.dot(a, w_ref[...].astype(BF16), preferred_element_type=F32) + b_ref[...]


def _ada(c_rows, ada_w, ada_b):
    L, D, N = ada_w.shape
    R = c_rows.shape[0]
    tn = 512
    return pl.pallas_call(
        _ada_kernel,
        grid=(L, N // tn),
        in_specs=[pl.BlockSpec((R, D), lambda l, j: (0, 0)),
                  pl.BlockSpec((None, D, tn), lambda l, j: (l, 0, j)),
                  pl.BlockSpec((None, 1, tn), lambda l, j: (l, 0, j))],
        out_specs=pl.BlockSpec((None, R, tn), lambda l, j: (l, 0, j)),
        out_shape=jax.ShapeDtypeStruct((L, R, N), F32),
        compiler_params=_cp("parallel", "parallel"),
        name="ada_mod",
    )(c_rows, ada_w, ada_b.reshape(L, 1, N))


def _norm_mod_kernel(x_ref, g_ref, sc_ref, sh_ref, h_ref):
    y = _rms(x_ref[...], g_ref[...])
    h_ref[...] = (y * (1.0 + sc_ref[...]) + sh_ref[...]).astype(BF16)


def _norm_mod_router_kernel(x_ref, g_ref, sc_ref, sh_ref, rwh_ref, rwl_ref, rb_ref, h_ref, gate_ref):
    y = _rms(x_ref[...], g_ref[...])
    h = y * (1.0 + sc_ref[...]) + sh_ref[...]
    h_hi = h.astype(BF16)
    h_ref[...] = h_hi
    h_lo = (h - h_hi.astype(F32)).astype(BF16)
    logits = (jnp.dot(h_hi, rwh_ref[...], preferred_element_type=F32)
              + jnp.dot(h_lo, rwh_ref[...], preferred_element_type=F32)
              + jnp.dot(h_hi, rwl_ref[...], preferred_element_type=F32)) + rb_ref[...]
    lane = lax.broadcasted_iota(I32, logits.shape, 1)
    lg = jnp.where(lane < N_EXPERTS, logits, -jnp.inf)
    m1 = jnp.max(lg, axis=-1, keepdims=True)
    i1 = jnp.min(jnp.where(lg == m1, lane, LANE), axis=-1, keepdims=True)
    lg2 = jnp.where(lane == i1, -jnp.inf, lg)
    m2 = jnp.max(lg2, axis=-1, keepdims=True)
    i2 = jnp.min(jnp.where(lg2 == m2, lane, LANE), axis=-1, keepdims=True)
    e2 = jnp.exp(m2 - m1)
    w1 = 1.0 / (1.0 + e2)
    gate_ref[...] = jnp.where(lane == i1, w1, 0.0) + jnp.where(lane == i2, e2 * w1, 0.0)


def _mod_spec(mod_map, k, D):
    return pl.BlockSpec((None, None, ROW_TILE, D), lambda i, *_: (k, mod_map(i), 0, 0))


def _norm_mod(x, g, modexp, mod_map, k_scale, k_shift, router=None):
    T, D = x.shape
    specs = [pl.BlockSpec((ROW_TILE, D), lambda i: (i, 0)),
             pl.BlockSpec((1, D), lambda i: (0, 0)),
             _mod_spec(mod_map, k_scale, D), _mod_spec(mod_map, k_shift, D)]
    args = [x, g.reshape(1, D), modexp, modexp]
    h_spec = pl.BlockSpec((ROW_TILE, D), lambda i: (i, 0))
    h_shape = jax.ShapeDtypeStruct((T, D), BF16)
    if router is None:
        return pl.pallas_call(
            _norm_mod_kernel, grid=(T // ROW_TILE,), in_specs=specs, out_specs=h_spec, out_shape=h_shape,
            compiler_params=_cp("parallel"), name="norm_mod")(*args)
    rw_hi, rw_lo, rb = router
    specs += [pl.BlockSpec((D, LANE), lambda i: (0, 0)), pl.BlockSpec((D, LANE), lambda i: (0, 0)),
              pl.BlockSpec((1, LANE), lambda i: (0, 0))]
    return pl.pallas_call(
        _norm_mod_router_kernel, grid=(T // ROW_TILE,), in_specs=specs,
        out_specs=[h_spec, pl.BlockSpec((ROW_TILE, LANE), lambda i: (i, 0))],
        out_shape=[h_shape, jax.ShapeDtypeStruct((T, LANE), F32)],
        compiler_params=_cp("parallel"), name="norm_mod_router")(*args, rw_hi, rw_lo, rb)


def _mm_kernel(a_ref, b_ref, o_ref):
    o_ref[...] = jnp.dot(a_ref[...], b_ref[...], preferred_element_type=F32).astype(o_ref.dtype)


def _mm(a, b, out_dtype, tm=ROW_TILE, tn=512, name="mm"):
    M, K = a.shape
    N = b.shape[1]
    return pl.pallas_call(
        _mm_kernel, grid=(M // tm, N // tn),
        in_specs=[pl.BlockSpec((tm, K), lambda i, j: (i, 0)), pl.BlockSpec((K, tn), lambda i, j: (0, j))],
        out_specs=pl.BlockSpec((tm, tn), lambda i, j: (i, j)),
        out_shape=jax.ShapeDtypeStruct((M, N), out_dtype),
        compiler_params=_cp("parallel", "parallel"), name=name)(a, b)


def _stage_weights(w_refs, wb_refs):
    @pl.when(pl.program_id(1) == 0)
    def _():
        for w_ref, wb_ref in zip(w_refs, wb_refs):
            wb_ref[...] = w_ref[...].astype(BF16)


def _mm_res_kernel(a_ref, b_ref, x_ref, g_ref, o_ref):
    o_ref[...] = x_ref[...] + g_ref[...] * jnp.dot(a_ref[...], b_ref[...], preferred_element_type=F32)


def _mm_res_ws_kernel(a_ref, w_ref, x_ref, g_ref, o_ref, wb_ref):
    _stage_weights([w_ref], [wb_ref])
    o_ref[...] = x_ref[...] + g_ref[...] * jnp.dot(a_ref[...], wb_ref[...], preferred_element_type=F32)


def _mm_res(a, b, x, modexp, mod_map, k_gate, tn, name="mm_res"):
    M, K = a.shape
    N = b.shape[1]
    tm = ROW_TILE
    out_shape = jax.ShapeDtypeStruct((M, N), F32)
    if b.dtype == BF16:
        return pl.pallas_call(
            _mm_res_kernel, grid=(M // tm, N // tn),
            in_specs=[pl.BlockSpec((tm, K), lambda i, j: (i, 0)),
                      pl.BlockSpec((K, tn), lambda i, j: (0, j)),
                      pl.BlockSpec((tm, tn), lambda i, j: (i, j)),
                      pl.BlockSpec((None, None, tm, tn), lambda i, j: (k_gate, mod_map(i), 0, j))],
            out_specs=pl.BlockSpec((tm, tn), lambda i, j: (i, j)), out_shape=out_shape,
            compiler_params=_cp("parallel", "parallel"), name=name)(a, b, x, modexp)
    return pl.pallas_call(
        _mm_res_ws_kernel, grid=(N // tn, M // tm),
        in_specs=[pl.BlockSpec((tm, K), lambda j, i: (i, 0)),
                  pl.BlockSpec((K, tn), lambda j, i: (0, j)),
                  pl.BlockSpec((tm, tn), lambda j, i: (i, j)),
                  pl.BlockSpec((None, None, tm, tn), lambda j, i: (k_gate, mod_map(i), 0, j))],
        out_specs=pl.BlockSpec((tm, tn), lambda j, i: (i, j)), out_shape=out_shape,
        scratch_shapes=[pltpu.VMEM((K, tn), BF16)],
        compiler_params=_cp("arbitrary", "arbitrary"), name=name)(a, b, x, modexp)


def _gmlp_kernel(gu_ref, gv_ref, lng_ref, lnb_ref, w_ref, bias_ref, o_ref, v_ref, *, c):
    u = jax.nn.gelu(gu_ref[...])
    gv = jax.nn.gelu(gv_ref[...])
    mu = jnp.mean(gv, axis=-1, keepdims=True)
    var = jnp.mean(jnp.square(gv - mu), axis=-1, keepdims=True)
    v = (gv - mu) * lax.rsqrt(var + NORM_EPS) * lng_ref[...] + lnb_ref[...]
    v_ref[...] = v
    vb = v.astype(BF16)
    row = lax.broadcasted_iota(I32, (c, c), 0)
    col = lax.broadcasted_iota(I32, (c, c), 1)
    bias = bias_ref[...]
    for g in range(GM_GROUPS):
        w = jnp.where(row >= col, w_ref[g], 0.0).astype(BF16)
        lo, hi = g * GM_GROUP_DIM, (g + 1) * GM_GROUP_DIM
        for n in range(ROW_TILE // c):
            r0, r1 = n * c, (n + 1) * c
            sp = jnp.dot(w, vb[r0:r1, lo:hi], preferred_element_type=F32) + bias[:, g:g + 1]
            o_ref[r0:r1, lo:hi] = (u[r0:r1, lo:hi] * sp).astype(BF16)


def _gmlp(hw, row0, rows, c, ln_g, ln_b, ws, bs):
    t0 = row0 // ROW_TILE
    gu_blk = _DST['gu'][0] // GM_WIDTH
    gv_blk = _DST['gv'][0] // GM_WIDTH
    w = ws[:, :c, :c]
    bias = bs[:, :c].T
    return pl.pallas_call(
        functools.partial(_gmlp_kernel, c=c), grid=(rows // ROW_TILE,),
        in_specs=[pl.BlockSpec((ROW_TILE, GM_WIDTH), lambda i: (t0 + i, gu_blk)),
                  pl.BlockSpec((ROW_TILE, GM_WIDTH), lambda i: (t0 + i, gv_blk)),
                  pl.BlockSpec((1, GM_WIDTH), lambda i: (0, 0)),
                  pl.BlockSpec((1, GM_WIDTH), lambda i: (0, 0)),
                  pl.BlockSpec((GM_GROUPS, c, c), lambda i: (0, 0, 0)),
                  pl.BlockSpec((c, GM_GROUPS), lambda i: (0, 0))],
        out_specs=[pl.BlockSpec((ROW_TILE, GM_WIDTH), lambda i: (i, 0)),
                   pl.BlockSpec((ROW_TILE, GM_WIDTH), lambda i: (i, 0))],
        out_shape=[jax.ShapeDtypeStruct((rows, GM_WIDTH), BF16), jax.ShapeDtypeStruct((rows, GM_WIDTH), F32)],
        compiler_params=_cp("parallel"), name="gmlp")(
            hw, hw, ln_g.reshape(1, -1), ln_b.reshape(1, -1), w, bias)


def _rope_tables(pos, rot, period):
    half = rot // 2
    inv = ROPE_THETA ** (-jnp.arange(half, dtype=F32) / half)
    ang = pos.astype(F32)[:, None] * inv[None, :]
    cos, sin = jnp.cos(ang), jnp.sin(ang)
    T = pos.shape[0]
    pad = jnp.zeros((T, period - rot), F32)
    c = jnp.concatenate([cos, cos, pad + 1.0], axis=1)
    s1 = jnp.concatenate([-sin, jnp.zeros_like(sin), pad], axis=1)
    s2 = jnp.concatenate([jnp.zeros_like(sin), sin, pad], axis=1)
    rep = LANE // period
    return jnp.stack([jnp.tile(c, (1, rep)), jnp.tile(s1, (1, rep)), jnp.tile(s2, (1, rep))], axis=0)


def _rope(x, tab_ref, half):
    w = x.shape[1]
    rep = w // LANE

    def wide(t):
        return t if rep == 1 else jnp.concatenate([t] * rep, axis=1)

    xl = pltpu.roll(x, w - half, axis=1)
    xr = pltpu.roll(x, half, axis=1)
    return x * wide(tab_ref[0]) + xl * wide(tab_ref[1]) + xr * wide(tab_ref[2])


def _state_kernel(cq_ref, ckv_ref, kr_ref, dq_ref, dk_ref, qi_ref, ki_ref, fq_ref, fk_ref,
                  qg_ref, kvg_ref, wqn_ref, wqp_ref, t64_ref, t32_ref, tidx_ref,
                  ckv_o, kpe_o, dsak_o, ki_o, diffk_o, qn_o, qp_o, dsaq_o, qi_o, diffq_o):
    cqn = _rms(cq_ref[...], qg_ref[...]).astype(BF16)
    qn_o[...] = (jnp.dot(cqn, wqn_ref[...], preferred_element_type=F32) * MLA_QSCALE).astype(BF16)
    qp = jnp.dot(cqn, wqp_ref[...], preferred_element_type=F32)
    qp_o[...] = (_rope(qp, t32_ref, MLA_ROPE // 2) * MLA_QSCALE).astype(BF16)
    ckv_o[...] = _rms(ckv_ref[...], kvg_ref[...])
    kpe_o[...] = _rope(kr_ref[...], t32_ref, MLA_ROPE // 2)
    dsaq_o[...] = (_rope(dq_ref[...], t64_ref, DSA_ROT // 2) * DSA_QSCALE).astype(BF16)
    dsak_o[...] = _rope(dk_ref[...], t64_ref, DSA_ROT // 2)
    qi_o[...] = _rope(qi_ref[...], tidx_ref, IDX_ROT // 2).astype(BF16)
    ki_o[...] = _rope(ki_ref[...], tidx_ref, IDX_ROT // 2)
    diffq_o[...] = (_rope(fq_ref[...], t64_ref, DIFF_ROT // 2) * DIFF_QSCALE).astype(BF16)
    diffk_o[...] = _rope(fk_ref[...], t64_ref, DIFF_ROT // 2)


def _state(hw, q_norm_g, kv_norm_g, w_uq_nope, w_uq_pe, t64, t32, tidx):
    T = hw.shape[0]

    def col(name):
        off, w = _DST[name]
        return pl.BlockSpec((ROW_TILE, w), lambda i: (i, off // w))

    def full(shape):
        return pl.BlockSpec(shape, lambda i: (0,) * len(shape))

    def tab():
        return pl.BlockSpec((3, ROW_TILE, LANE), lambda i: (0, i, 0))

    def out(w, dt):
        return pl.BlockSpec((ROW_TILE, w), lambda i: (i, 0)), jax.ShapeDtypeStruct((T, w), dt)

    outs = [out(MLA_KV_LORA, F32), out(LANE, F32), out(512, F32), out(LANE, F32), out(512, F32),
            out(512, BF16), out(MLA_HEADS * MLA_ROPE, BF16), out(512, BF16), out(IDX_HEADS * IDX_DIM, BF16),
            out(512, BF16)]
    return pl.pallas_call(
        _state_kernel, grid=(T // ROW_TILE,),
        in_specs=[col('cq'), col('ckv'), col('krope'), col('dq'), col('dk'), col('qi'), col('ki'),
                  col('fq'), col('fk'),
                  full((1, MLA_Q_LORA)), full((1, MLA_KV_LORA)),
                  full((MLA_Q_LORA, MLA_HEADS * MLA_NOPE)), full((MLA_Q_LORA, MLA_HEADS * MLA_ROPE)),
                  tab(), tab(), tab()],
        out_specs=[o[0] for o in outs], out_shape=[o[1] for o in outs],
        compiler_params=_cp("parallel"), name="mixer_state")(
            hw, hw, hw, hw, hw, hw, hw, hw, hw,
            q_norm_g.reshape(1, -1), kv_norm_g.reshape(1, -1), w_uq_nope, w_uq_pe, t64, t32, tidx)


def _block_offset(j):
    return j * KV_BLOCK if isinstance(j, int) else pl.multiple_of(j * KV_BLOCK, KV_BLOCK)


def _softmax_steps(scores, values, carries):
    stats = []
    for s, (m, l, _) in zip(scores, carries):
        m_new = jnp.maximum(m, jnp.max(s, axis=0, keepdims=True))
        alpha = jnp.exp2(m - m_new)
        p = jnp.exp2(s - m_new)
        stats.append((m_new, alpha * l + jnp.sum(p, axis=0, keepdims=True), alpha, p.astype(BF16)))
    pvs = [lax.dot_general(v, st[3], (((0,), (0,)), ((), ())), preferred_element_type=F32)
           for v, st in zip(values, stats)]
    return tuple((st[0], st[1], st[2] * c[2] + pv) for st, c, pv in zip(stats, carries, pvs))


def _softmax_init(dv, tq):
    return jnp.full((1, tq), NEG_INF, F32), jnp.zeros((1, tq), F32), jnp.zeros((dv, tq), F32)


def _attend_blocks(n_blocks, qk, values, carries, mask_last):
    carries = lax.fori_loop(0, n_blocks - 1, lambda j, c: _softmax_steps(qk(j), values(j), c), carries)
    scores = qk(n_blocks - 1)
    if mask_last is not None:
        scores = mask_last(scores)
    return _softmax_steps(scores, values(n_blocks - 1), carries)


def _pair_loop(lo, hi, step, carry):
    n2 = (hi - lo) // 2

    def two(t, c):
        j = lo + 2 * t
        return step(j + 1, step(j, c))

    carry = lax.fori_loop(0, n2, two, carry)
    return lax.fori_loop(lo + 2 * n2, hi, step, carry)


def _flash_kernel(*refs, tq, nkb, causal, s_valid, k_lanes, v_rows, dv, diff_layer):
    if diff_layer is None:
        qt_ref, k_ref, vt_ref, o_ref = refs
    else:
        qt_ref, k_ref, vt_ref, lam_ref, sg_ref, o_ref = refs
    i = pl.program_id(2)
    padded = s_valid < nkb * KV_BLOCK

    hg = len(k_lanes)
    n_blocks = i * tq // KV_BLOCK + 1 if causal else nkb
    qts = [_slot_query(qt_ref, hl, k_lanes.count(k_lanes[hl])) for hl in range(hg)]

    def qk(j):
        off = _block_offset(j)
        return tuple(jnp.dot(k_ref[pl.ds(off, KV_BLOCK), k0:k0 + LANE], qts[hl], preferred_element_type=F32)
                     for hl, k0 in enumerate(k_lanes))

    def values(j):
        return [vt_ref[pl.ds(_block_offset(j), KV_BLOCK), v0:v0 + dv] for v0 in v_rows]

    def mask_last(scores):
        kpos = (n_blocks - 1) * KV_BLOCK + lax.broadcasted_iota(I32, (KV_BLOCK, tq), 0)
        if causal:
            qpos = i * tq + lax.broadcasted_iota(I32, (KV_BLOCK, tq), 1)
            ok = kpos // CHUNK <= qpos // CHUNK
        else:
            ok = kpos < s_valid
        return [jnp.where(ok, s, NEG_INF) for s in scores]

    carries = _attend_blocks(n_blocks, qk, values, tuple(_softmax_init(dv, tq) for _ in range(hg)),
                             mask_last if (causal or padded) else None)
    heads = [acc / l for _, l, acc in carries]

    if diff_layer is None:
        for hl in range(hg):
            o_ref[v_rows[hl]:v_rows[hl] + dv, :] = heads[hl].astype(o_ref.dtype)
    else:
        lam_init = 0.8 - 0.6 * math.exp(-0.3 * diff_layer)
        lp = lam_ref[...]
        lam = (jnp.exp(jnp.sum(lp[0:1] * lp[1:2], axis=-1, keepdims=True))
               - jnp.exp(jnp.sum(lp[2:3] * lp[3:4], axis=-1, keepdims=True)) + lam_init)
        for h2 in range(hg // 2):
            o = heads[2 * h2] - lam * heads[2 * h2 + 1]
            y = o * lax.rsqrt(jnp.mean(o * o, axis=0, keepdims=True) + NORM_EPS) * sg_ref[...]
            o_ref[v_rows[2 * h2]:v_rows[2 * h2] + dv, :] = (y * (1.0 - lam_init)).astype(o_ref.dtype)


def _flash(qt, k, vt, *, tq, causal, s_valid, dv, k_width, k_lanes, diff=None, name="flash"):
    B, n_slots, _, Tq = qt.shape
    nkb = vt.shape[1] // KV_BLOCK
    S_pad = nkb * KV_BLOCK
    assert not causal or (tq <= KV_BLOCK and KV_BLOCK % tq == 0)
    hg = len(k_lanes)
    share = k_lanes.count(k_lanes[0])
    n_grp = n_slots * share // hg
    v_rows = tuple(hl // share * dv for hl in range(hg))
    in_specs = [pl.BlockSpec((None, hg // share, LANE, tq), lambda b, g, i: (b, g, 0, i)),
                pl.BlockSpec((None, S_pad, k_width), lambda b, g, i: (b, 0, g)),
                pl.BlockSpec((None, S_pad, MXU_WIDTH), lambda b, g, i: (b, 0, g))]
    args = [qt, k, vt]
    diff_layer = None
    if diff is not None:
        lam_params, subln_g, diff_layer = diff
        in_specs += [pl.BlockSpec(lam_params.shape, lambda b, g, i: (0, 0)),
                     pl.BlockSpec((dv, tq), lambda b, g, i: (0, 0))]
        args += [lam_params, jnp.broadcast_to(subln_g[:, None], (dv, tq))]
    kern = functools.partial(_flash_kernel, tq=tq, nkb=nkb, causal=causal, s_valid=s_valid,
                             k_lanes=k_lanes, v_rows=v_rows, dv=dv, diff_layer=diff_layer)
    return pl.pallas_call(
        kern, grid=(B, n_grp, Tq // tq), in_specs=in_specs,
        out_specs=pl.BlockSpec((None, MXU_WIDTH, tq), lambda b, g, i: (b, g, i)),
        out_shape=jax.ShapeDtypeStruct((B, n_grp * MXU_WIDTH, Tq), BF16),
        compiler_params=_cp("parallel", "parallel", "parallel"), name=name)(*args)


def _dsa_kernel(qit_ref, wit_ref, ki_ref, qt_ref, k_ref, vt_ref, o_ref, key_ref,
                *, tq, nkb, causal, s_valid, n_sel):
    i = pl.program_id(1)
    tkb = KV_BLOCK
    padded = s_valid < nkb * tkb
    nvis = jnp.minimum((i * tq + tq + tkb - 1) // tkb, nkb) if causal else nkb
    wit = wit_ref[...] * IDX_SCALE

    def kpos(j):
        return j * tkb + lax.broadcasted_iota(I32, (tkb, tq), 0)

    def score_block(j, masked):
        kib = ki_ref[pl.ds(_block_offset(j), tkb), :]
        rs = [jnp.dot(kib, qit_ref[h], preferred_element_type=F32) for h in range(IDX_HEADS)]
        sc = jnp.zeros((tkb, tq), F32)
        for h in range(IDX_HEADS):
            sc = sc + jnp.maximum(rs[h], 0.0) * wit[h:h + 1, :]
        sc = sc + 0.0
        bits = pltpu.bitcast(sc, I32)
        key = jnp.where(bits < 0, bits ^ 0x7FFFFFFF, bits)
        if masked and causal:
            qpos = i * tq + lax.broadcasted_iota(I32, (tkb, tq), 1)
            key = jnp.where(kpos(j) // CHUNK <= qpos // CHUNK, key, INT_MIN)
        elif masked:
            key = jnp.where(kpos(j) < s_valid, key, INT_MIN)
        key_ref[j] = key

    lax.fori_loop(0, nvis - 1, lambda j, c: (score_block(j, False), c)[1], 0)
    score_block(nvis - 1, causal or padded)

    def count(pred):
        def body(j, cnt):
            hit = jnp.where(pred(key_ref[j], j), 1.0, 0.0)
            part = hit[0:8]
            for r in range(1, tkb // 8):
                part = part + hit[r * 8:(r + 1) * 8]
            return cnt + part
        cnt = _pair_loop(0, nvis, body, jnp.zeros((8, tq), F32))
        return jnp.sum(cnt, axis=0, keepdims=True)

    def bit_step(it, t):
        cand = t + lax.shift_left(jnp.int32(1), 31 - it)
        cnt = count(lambda k, j: k >= cand)
        return jnp.where(cnt >= n_sel, cand, t)

    thr = lax.fori_loop(0, 32, bit_step, jnp.full((1, tq), INT_MIN, I32))
    cnt_ge = count(lambda k, j: k >= thr)
    need_tie = jnp.logical_and(cnt_ge > n_sel, thr > INT_MIN)
    any_tie = jnp.max(jnp.where(need_tie, 1.0, 0.0)) > 0.0

    def write_bias(sel_of):
        def wr(j, _):
            key_ref[j] = pltpu.bitcast(jnp.where(sel_of(key_ref[j], j), 0.0, NEG_INF), I32)
            return 0
        lax.fori_loop(0, nvis, wr, 0)

    @pl.when(jnp.logical_not(any_tie))
    def _():
        tsel = jnp.maximum(thr, INT_MIN + 1)
        write_bias(lambda k, j: k >= tsel)

    @pl.when(any_tie)
    def _():
        cnt_gt = count(lambda k, j: k > thr)

        def idx_step(it, x):
            cand = x + lax.shift_left(jnp.int32(1), 14 - it)
            g = cnt_gt + count(lambda k, j: jnp.logical_and(k == thr, kpos(j) < cand))
            return jnp.where(g < n_sel, cand, x)

        xj = lax.fori_loop(0, 15, idx_step, jnp.zeros((1, tq), I32))
        jmax = jnp.where(need_tie, xj, jnp.where(thr == INT_MIN, -1, 2 ** 30))
        write_bias(lambda k, j: jnp.logical_or(k > thr, jnp.logical_and(k == thr, kpos(j) <= jmax)))

    qts = [_slot_query(qt_ref, hl, 2) for hl in range(DSA_HEADS)]

    def qk(j):
        off = _block_offset(j)
        bias = pltpu.bitcast(key_ref[j], F32)
        return tuple(jnp.dot(k_ref[pl.ds(off, tkb), hl // 2 * LANE:(hl // 2 + 1) * LANE], qts[hl],
                             preferred_element_type=F32) + bias for hl in range(DSA_HEADS))

    def values(j):
        off = _block_offset(j)
        return [vt_ref[pl.ds(off, tkb), hl * DSA_HEAD_DIM:(hl + 1) * DSA_HEAD_DIM] for hl in range(DSA_HEADS)]

    carries = _attend_blocks(nvis, qk, values,
                             tuple(_softmax_init(DSA_HEAD_DIM, tq) for _ in range(DSA_HEADS)), None)
    for hl, (_, l, acc) in enumerate(carries):
        o_ref[hl * DSA_HEAD_DIM:(hl + 1) * DSA_HEAD_DIM, :] = (acc / l).astype(o_ref.dtype)


def _dsa(qit, wit, ki, qt, k, vt, *, tq, causal, s_valid, n_sel, name="dsa"):
    B, H, _, Tq = qt.shape
    nkb = vt.shape[1] // KV_BLOCK
    S_pad = nkb * KV_BLOCK
    assert not causal or (tq <= KV_BLOCK and KV_BLOCK % tq == 0)
    W = DSA_HEADS * DSA_HEAD_DIM
    kern = functools.partial(_dsa_kernel, tq=tq, nkb=nkb, causal=causal, s_valid=s_valid, n_sel=n_sel)
    return pl.pallas_call(
        kern, grid=(B, Tq // tq),
        in_specs=[pl.BlockSpec((None, IDX_HEADS, LANE, tq), lambda b, i: (b, 0, 0, i)),
                  pl.BlockSpec((None, IDX_HEADS, tq), lambda b, i: (b, 0, i)),
                  pl.BlockSpec((None, S_pad, LANE), lambda b, i: (b, 0, 0)),
                  pl.BlockSpec((None, H, LANE, tq), lambda b, i: (b, 0, 0, i)),
                  pl.BlockSpec((None, S_pad, W), lambda b, i: (b, 0, 0)),
                  pl.BlockSpec((None, S_pad, W), lambda b, i: (b, 0, 0))],
        out_specs=pl.BlockSpec((None, W, tq), lambda b, i: (b, 0, i)),
        out_shape=jax.ShapeDtypeStruct((B, W, Tq), BF16),
        scratch_shapes=[pltpu.VMEM((nkb, KV_BLOCK, tq), I32)],
        compiler_params=_cp("parallel", "parallel"), name=name)(qit, wit, ki, qt, k, vt)


def _merge_kernel(h_ref, oa_ref, ob_ref, oc_ref, od_ref, wg0, wg1, wg2, wg3, bg0, bg1, bg2, bg3, wb_ref, o_ref,
                  wgb0, wgb1, wgb2, wgb3, wbb_ref):
    _stage_weights([wg0, wg1, wg2, wg3, wb_ref], [wgb0, wgb1, wgb2, wgb3, wbb_ref])
    h = h_ref[...]
    acc = None
    for n, (o_r, wg, bg) in enumerate(((oa_ref, wgb0, bg0), (ob_ref, wgb1, bg1), (oc_ref, wgb2, bg2),
                                       (od_ref, wgb3, bg3))):
        gate = jax.nn.sigmoid(jnp.dot(h, wg[...], preferred_element_type=F32) + bg[...])
        term = gate * jnp.dot(o_r[...], wbb_ref[n], preferred_element_type=F32)
        acc = term if acc is None else acc + term
    o_ref[...] = acc.astype(BF16)


def _merge(h, branches, w_gate, b_gate, w_branch, tn=512):
    T, D = h.shape
    nd = D // tn
    b_gate = b_gate.reshape(1, -1)
    once = pl.Buffered(1)
    specs = [pl.BlockSpec((ROW_TILE, D), lambda j, i: (i, 0))]
    specs += [pl.BlockSpec((ROW_TILE, BRANCH_WIDTH), lambda j, i: (i, 0))] * N_BRANCH
    specs += [pl.BlockSpec((D, tn), lambda j, i, n=n: (0, n * nd + j), pipeline_mode=once) for n in range(N_BRANCH)]
    specs += [pl.BlockSpec((1, tn), lambda j, i, n=n: (0, n * nd + j)) for n in range(N_BRANCH)]
    specs += [pl.BlockSpec((N_BRANCH, BRANCH_WIDTH, tn), lambda j, i: (0, 0, j), pipeline_mode=once)]
    return pl.pallas_call(
        _merge_kernel, grid=(nd, T // ROW_TILE), in_specs=specs,
        out_specs=pl.BlockSpec((ROW_TILE, tn), lambda j, i: (i, j)),
        out_shape=jax.ShapeDtypeStruct((T, D), BF16),
        scratch_shapes=[pltpu.VMEM((D, tn), BF16)] * N_BRANCH + [pltpu.VMEM((N_BRANCH, BRANCH_WIDTH, tn), BF16)],
        compiler_params=_cp("arbitrary", "arbitrary"), name="merge")(
            h, *branches, w_gate, w_gate, w_gate, w_gate, b_gate, b_gate, b_gate, b_gate, w_branch)


def _ffn_up_kernel(h_ref, w1_ref, w3_ref, o_ref, w1b_ref, w3b_ref):
    _stage_weights([w1_ref, w3_ref], [w1b_ref, w3b_ref])
    h = h_ref[...]
    a = jnp.dot(h, w1b_ref[...], preferred_element_type=F32)
    b = jnp.dot(h, w3b_ref[...], preferred_element_type=F32)
    o_ref[...] = (a * jax.nn.sigmoid(a) * b).astype(BF16)


def _moe_up_kernel(h_ref, w1_ref, w3_ref, gate_ref, o_ref):
    e = pl.program_id(1)
    h = h_ref[...]
    a = jnp.dot(h, w1_ref[...], preferred_element_type=F32)
    b = jnp.dot(h, w3_ref[...], preferred_element_type=F32)
    gate = gate_ref[...]
    lane = lax.broadcasted_iota(I32, gate.shape, 1)
    ge = jnp.sum(jnp.where(lane == e, gate, 0.0), axis=-1, keepdims=True)
    o_ref[...] = (a * jax.nn.sigmoid(a) * b * ge).astype(BF16)


def _ffn_up(h, w1, w3, tn=512):
    T, D = h.shape
    F = w1.shape[1]
    return pl.pallas_call(
        _ffn_up_kernel, grid=(F // tn, T // ROW_TILE),
        in_specs=[pl.BlockSpec((ROW_TILE, D), lambda j, i: (i, 0)),
                  pl.BlockSpec((D, tn), lambda j, i: (0, j)), pl.BlockSpec((D, tn), lambda j, i: (0, j))],
        out_specs=pl.BlockSpec((ROW_TILE, tn), lambda j, i: (i, j)),
        out_shape=jax.ShapeDtypeStruct((T, F), BF16),
        scratch_shapes=[pltpu.VMEM((D, tn), BF16)] * 2,
        compiler_params=_cp("arbitrary", "arbitrary"), name="ffn_up")(h, w1, w3)


def _moe_up(h, w1, w3, gate):
    T, D = h.shape
    E, _, F = w1.shape
    return pl.pallas_call(
        _moe_up_kernel, grid=(T // ROW_TILE, E),
        in_specs=[pl.BlockSpec((ROW_TILE, D), lambda i, e: (i, 0)),
                  pl.BlockSpec((None, D, F), lambda i, e: (e, 0, 0)),
                  pl.BlockSpec((None, D, F), lambda i, e: (e, 0, 0)),
                  pl.BlockSpec((ROW_TILE, LANE), lambda i, e: (i, 0))],
        out_specs=pl.BlockSpec((ROW_TILE, F), lambda i, e: (i, e)),
        out_shape=jax.ShapeDtypeStruct((T, E * F), BF16),
        compiler_params=_cp("parallel", "parallel"), name="moe_up")(h, w1, w3, gate)


def _final_norm_kernel(x_ref, g_ref, o_ref):
    o_ref[...] = _rms(x_ref[...], g_ref[...])


def _final_norm(x, g):
    T, D = x.shape
    return pl.pallas_call(
        _final_norm_kernel, grid=(T // ROW_TILE,),
        in_specs=[pl.BlockSpec((ROW_TILE, D), lambda i: (i, 0)), pl.BlockSpec((1, D), lambda i: (0, 0))],
        out_specs=pl.BlockSpec((ROW_TILE, D), lambda i: (i, 0)),
        out_shape=jax.ShapeDtypeStruct((T, D), F32),
        compiler_params=_cp("parallel"), name="final_norm")(x, g.reshape(1, D))


def _pad_rows(a, s_pad):
    pad = s_pad - a.shape[1]
    if pad == 0:
        return a
    return jnp.pad(a, ((0, 0), (0, pad)) + ((0, 0),) * (a.ndim - 2))


def _queries_t(q, n_slots, tq_pad):
    B, T, W = q.shape
    d = W // n_slots
    qt = q.reshape(B, T, n_slots, d).transpose(0, 2, 3, 1)
    if d < LANE:
        qt = jnp.pad(qt, ((0, 0), (0, 0), (0, LANE - d), (0, 0)))
    reps = -(-tq_pad // T)
    return jnp.concatenate([qt] * reps, axis=-1)[..., :tq_pad] if reps > 1 else qt


def _slot_query(qt_ref, hl, heads_per_slot):
    if heads_per_slot == 1:
        return qt_ref[hl]
    qp = qt_ref[hl // heads_per_slot]
    d = LANE // heads_per_slot
    r = hl % heads_per_slot
    parts = [qp[s * d:(s + 1) * d] if s == r else jnp.zeros((d, qp.shape[1]), qp.dtype)
             for s in range(heads_per_slot)]
    return jnp.concatenate(parts, axis=0)


def _rows_out(ot, T):
    B, W, _ = ot.shape
    return ot[:, :, :T].transpose(0, 2, 1).reshape(B * T, W)


def _pad_in_weight(w_in):
    D = w_in.shape[0]
    out = jnp.zeros((D, IN_PAD), BF16)
    for name in _ORDER:
        s0, sw = _SRC[name]
        d0, _ = _DST[name]
        out = lax.dynamic_update_slice(out, w_in[:, s0:s0 + sw].astype(BF16), (0, d0))
    return out


def _col(hw, name, width=None):
    off, w = _DST[name]
    return hw[:, off:off + (width or w)]


def _mixers(l, hw, st, caches, lp, geom):
    n_p, B_p, T_p, B_s, T_s, past = geom
    ckv_n, kpe, dsa_k, ki, diff_k, q_nope, q_pe, dsa_q, qi, diff_q = st
    wi = _col(hw, 'wi')
    dsa_v = _col(hw, 'dv')
    diff_v = _col(hw, 'fv')

    def split(a):
        W = a.shape[1]
        return a[:n_p].reshape(B_p, T_p, W), a[n_p:].reshape(B_s, T_s, W)

    o_a_p, v_p = _gmlp(hw, 0, n_p, min(T_p, GM_CHUNK), lp['gm_ln_g'], lp['gm_ln_b'], lp['gm_ws'], lp['gm_bs'])
    o_a_s, v_s = _gmlp(hw, n_p, hw.shape[0] - n_p, min(T_s, GM_CHUNK), lp['gm_ln_g'], lp['gm_ln_b'],
                       lp['gm_ws'], lp['gm_bs'])
    c_p, c_s = min(T_p, GM_CHUNK), min(T_s, GM_CHUNK)
    gm_v_p = v_p.reshape(B_p, T_p, GM_WIDTH)[:, T_p - c_p:]
    gm_v_s = v_s.reshape(B_s, T_s, GM_WIDTH)[:, T_s - c_s:]

    outs = {}
    for grp in ('p', 's'):
        pi = 0 if grp == 'p' else 1
        B, T = (B_p, T_p) if grp == 'p' else (B_s, T_s)

        def g(a, pi=pi):
            return split(a)[pi]

        if grp == 'p':
            S = s_pad = T
            tq = min(Q_TILE, T)
            tq_pad = T
            causal = True

            def ext(new, cache):
                return new
        else:
            S = past + T
            s_pad = -(-S // KV_BLOCK) * KV_BLOCK
            tq = tq_pad = -(-T // LANE) * LANE
            causal = False

            def ext(new, cache, s_pad=s_pad):
                return _pad_rows(jnp.concatenate([cache.reshape(cache.shape[:2] + (-1,)), new], axis=1), s_pad)

        cm_ckv, cm_kr, cd_k, cd_v, cd_ki, cf_k, cf_v = caches

        all_c = ext(g(ckv_n), cm_ckv).astype(BF16)
        all_pe = ext(g(kpe)[..., :MLA_ROPE], cm_kr).astype(BF16)
        kv = _mm(all_c.reshape(B * s_pad, MLA_KV_LORA), lp['w_ukv'], BF16,
                 tm=ROW_TILE if (B * s_pad) % ROW_TILE == 0 else LANE, name="mla_kv")
        kv = kv.reshape(B, s_pad, 2 * MLA_HEADS * MLA_NOPE)
        k_nope = kv[..., :MLA_HEADS * MLA_NOPE].reshape(B, s_pad, MLA_HEADS, MLA_NOPE)
        k_mla = jnp.concatenate(
            [k_nope, jnp.broadcast_to(all_pe[:, :, None, :], (B, s_pad, MLA_HEADS, MLA_ROPE)),
             jnp.zeros((B, s_pad, MLA_HEADS, LANE - MLA_NOPE - MLA_ROPE), BF16)], axis=-1)
        q_mla = jnp.concatenate([g(q_nope).reshape(B, T, MLA_HEADS, MLA_NOPE),
                                 g(q_pe).reshape(B, T, MLA_HEADS, MLA_ROPE)], axis=-1)
        o_b = _flash(_queries_t(q_mla.reshape(B, T, -1), MLA_HEADS, tq_pad),
                     k_mla.reshape(B, s_pad, MLA_HEADS * LANE), kv[..., MLA_HEADS * MLA_NOPE:],
                     tq=tq, causal=causal, s_valid=S, dv=MLA_V, k_width=HEAD_GROUP * LANE,
                     k_lanes=tuple(h * LANE for h in range(HEAD_GROUP)), name="mla_attn")

        k_all = ext(g(dsa_k), cd_k).astype(BF16)
        v_all = ext(g(dsa_v), cd_v).astype(BF16)
        ki_all = ext(g(ki)[..., :IDX_DIM], cd_ki).astype(BF16)
        ki_all = jnp.pad(ki_all, ((0, 0), (0, 0), (0, LANE - IDX_DIM)))
        wit = _queries_t(g(wi)[..., :IDX_HEADS], IDX_HEADS, tq_pad)[:, :, 0, :]
        o_c = _dsa(_queries_t(g(qi), IDX_HEADS, tq_pad), wit, ki_all,
                   _queries_t(g(dsa_q), DSA_HEADS // 2, tq_pad), k_all, v_all,
                   tq=tq, causal=causal, s_valid=S, n_sel=min(DSA_TOPK, S // 4))

        fk_all = ext(g(diff_k), cf_k).astype(BF16)
        fv_all = ext(g(diff_v), cf_v).astype(BF16)
        o_d = _flash(_queries_t(g(diff_q), DIFF_HEADS, tq_pad), fk_all, fv_all,
                     tq=tq, causal=causal, s_valid=S, dv=DIFF_V_DIM, k_width=MXU_WIDTH,
                     k_lanes=(0, 0, LANE, LANE),
                     diff=(lp['diff_lambda'], lp['diff_subln_g'], l), name="diff_attn")
        outs[grp] = (_rows_out(o_b, T), _rows_out(o_c, T), _rows_out(o_d, T))

    o_a = jnp.concatenate([o_a_p, o_a_s], axis=0)
    o_b, o_c, o_d = (jnp.concatenate([outs['p'][k], outs['s'][k]], axis=0) for k in range(3))

    def state(grp):
        pi = 0 if grp == 'p' else 1
        B, T = (B_p, T_p) if grp == 'p' else (B_s, T_s)

        def g(a):
            return split(a)[pi]
        return (gm_v_p if grp == 'p' else gm_v_s,
                g(ckv_n), g(kpe)[..., :MLA_ROPE],
                g(dsa_k).reshape(B, T, DSA_HEADS, DSA_HEAD_DIM), g(dsa_v).reshape(B, T, DSA_HEADS, DSA_HEAD_DIM),
                g(ki)[..., :IDX_DIM],
                g(diff_k).reshape(B, T, DIFF_HEADS, 2, DIFF_QK_DIM), g(diff_v).reshape(B, T, DIFF_HEADS, DIFF_V_DIM))

    return (o_a, o_b, o_c, o_d), state('p'), state('s')


def kernel(x_prompt, x_sample, c_prompt, c_sample, cache_mla_ckv, cache_mla_krope, cache_dsa_k, cache_dsa_v, cache_dsa_kidx, cache_diff_k, cache_diff_v, ada_w, ada_b, norm_mix_g, norm_ff_g, w_in, w_gate, b_gate, gm_ln_g, gm_ln_b, gm_ws, gm_bs, mla_q_norm_g, mla_kv_norm_g, mla_w_uq, mla_w_ukv, diff_lambda, diff_subln_g, w_branch, w_out, ffn_w1, ffn_w3, ffn_w2, moe_router_w, moe_router_b, moe_w1, moe_w3, moe_w2, final_norm_g):
    B_p, T_p, D = x_prompt.shape
    B_s, T_s, _ = x_sample.shape
    depth = ada_w.shape[0]
    past = cache_mla_ckv.shape[2]
    n_p, n_s = B_p * T_p, B_s * T_s
    assert T_p % ROW_TILE == 0 and n_s % ROW_TILE == 0 and ROW_TILE % T_s == 0
    assert past % CHUNK == 0 and T_s <= CHUNK
    n_p_tiles, tiles_per_b = n_p // ROW_TILE, T_p // ROW_TILE
    geom = (n_p, B_p, T_p, B_s, T_s, past)

    def mod_map(i):
        return jnp.where(i < n_p_tiles, i // tiles_per_b, B_p + i - n_p_tiles)

    x = jnp.concatenate([x_prompt.reshape(n_p, D), x_sample.reshape(n_s, D)], axis=0)

    n_c = B_p + B_s
    c_rows = jnp.concatenate([c_prompt, c_sample, jnp.zeros((-n_c % 8, D), F32)], axis=0)
    mod = _ada(c_rows, ada_w, ada_b)
    mod = mod[:, :n_c].reshape(depth, n_c, 6, D).transpose(0, 2, 1, 3)
    mod_p = jnp.broadcast_to(mod[:, :, :B_p, None, :], (depth, 6, B_p, ROW_TILE, D))
    mod_s = jnp.broadcast_to(mod[:, :, B_p:, None, :], (depth, 6, B_s, T_s, D)).reshape(
        depth, 6, n_s // ROW_TILE, ROW_TILE, D)
    modexp = jnp.concatenate([mod_p, mod_s], axis=2)

    pos = jnp.concatenate([jnp.tile(jnp.arange(T_p, dtype=I32), B_p),
                           jnp.tile(past + jnp.arange(T_s, dtype=I32), B_s)])
    t64 = _rope_tables(pos, DSA_ROT, DSA_HEAD_DIM)
    t32 = _rope_tables(pos, MLA_ROPE, MLA_ROPE)
    tidx = _rope_tables(pos, IDX_ROT, IDX_DIM)

    states_p, states_s = [], []
    for l in range(depth):
        uq = mla_w_uq[l].reshape(MLA_Q_LORA, MLA_HEADS, MLA_NOPE + MLA_ROPE)
        ukv = mla_w_ukv[l].reshape(MLA_KV_LORA, MLA_HEADS, MLA_NOPE + MLA_V)
        lp = {
            'gm_ln_g': gm_ln_g[l], 'gm_ln_b': gm_ln_b[l], 'gm_ws': gm_ws[l], 'gm_bs': gm_bs[l],
            'w_ukv': jnp.concatenate([ukv[..., :MLA_NOPE].reshape(MLA_KV_LORA, -1),
                                      ukv[..., MLA_NOPE:].reshape(MLA_KV_LORA, -1)], axis=1).astype(BF16),
            'diff_lambda': diff_lambda[l], 'diff_subln_g': diff_subln_g[l],
        }
        h = _norm_mod(x, norm_mix_g[l], modexp[l], mod_map, 1, 0)
        hw = _mm(h, _pad_in_weight(w_in[l]), F32, name="in_proj")
        st = _state(hw, mla_q_norm_g[l], mla_kv_norm_g[l],
                    uq[..., :MLA_NOPE].reshape(MLA_Q_LORA, -1).astype(BF16),
                    uq[..., MLA_NOPE:].reshape(MLA_Q_LORA, -1).astype(BF16), t64, t32, tidx)
        caches = (cache_mla_ckv[l], cache_mla_krope[l], cache_dsa_k[l], cache_dsa_v[l], cache_dsa_kidx[l],
                  cache_diff_k[l], cache_diff_v[l])
        branches, st_p, st_s = _mixers(l, hw, st, caches, lp, geom)
        states_p.append(st_p)
        states_s.append(st_s)
        merged = _merge(h, branches, w_gate[l], b_gate[l], w_branch[l])
        x = _mm_res(merged, w_out[l], x, modexp[l], mod_map, 2, tn=512, name="out_proj")

        j = l // 2
        if l % 2 == 0:
            h2 = _norm_mod(x, norm_ff_g[l], modexp[l], mod_map, 4, 3)
            act = _ffn_up(h2, ffn_w1[j], ffn_w3[j])
            w2 = ffn_w2[j]
        else:
            rw = jnp.zeros((D, LANE), F32).at[:, :N_EXPERTS].set(moe_router_w[j])
            rw_hi = rw.astype(BF16)
            rw_lo = (rw - rw_hi.astype(F32)).astype(BF16)
            rb = jnp.zeros((1, LANE), F32).at[0, :N_EXPERTS].set(moe_router_b[j])
            h2, gate = _norm_mod(x, norm_ff_g[l], modexp[l], mod_map, 4, 3, router=(rw_hi, rw_lo, rb))
            act = _moe_up(h2, moe_w1[j].astype(BF16), moe_w3[j].astype(BF16), gate)
            w2 = moe_w2[j].astype(BF16).reshape(-1, D)
        x = _mm_res(act, w2, x, modexp[l], mod_map, 5, tn=512 if w2.dtype == F32 else MXU_WIDTH, name="ffn_down")

    y = _final_norm(x, final_norm_g)
    outs = [y[:n_p].reshape(B_p, T_p, D), y[n_p:].reshape(B_s, T_s, D)]
    for k in range(8):
        outs.append(jnp.stack([s[k] for s in states_p], axis=0))
        outs.append(jnp.stack([s[k] for s in states_s], axis=0))
    return tuple(outs)
```

```python
import functools
import math

import jax
import jax.numpy as jnp
from jax import lax
from jax.experimental import pallas as pl
from jax.experimental.pallas import tpu as pltpu

F32 = jnp.float32
BF16 = jnp.bfloat16
I32 = jnp.int32

CHUNK = 64
ROPE_THETA = 500000.0
NORM_EPS = 1e-6
NEG_INF = -1e30
N_BRANCH = 4
BRANCH_WIDTH = 512
GM_CHUNK = 128
GM_GROUPS = 4
GM_WIDTH = 512
GM_GROUP_DIM = GM_WIDTH // GM_GROUPS
MLA_HEADS = 8
MLA_Q_LORA = 512
MLA_KV_LORA = 256
MLA_NOPE = 64
MLA_ROPE = 32
MLA_V = 64
DSA_HEADS = 8
DSA_HEAD_DIM = 64
DSA_ROT = DSA_HEAD_DIM // 4
IDX_HEADS = 8
IDX_DIM = 32
IDX_ROT = IDX_DIM // 4
IDX_SCALE = (IDX_DIM ** -0.5) * (IDX_HEADS ** -0.5)
DSA_TOPK = 256
DIFF_HEADS = 4
DIFF_QK_DIM = 64
DIFF_V_DIM = 128
DIFF_ROT = DIFF_QK_DIM // 4
N_EXPERTS = 8

LANE = 128
MXU_WIDTH = 256
VMEM_LIMIT_BYTES = 56 * 1024 * 1024

ROW_TILE = 512
HEAD_GROUP = MXU_WIDTH // DSA_HEAD_DIM
LOG2E = 1.4426950408889634
INT_MIN = -2 ** 31
MLA_QSCALE = (MLA_NOPE + MLA_ROPE) ** -0.5 * LOG2E
DSA_QSCALE = DSA_HEAD_DIM ** -0.5 * LOG2E
DIFF_QSCALE = DIFF_QK_DIM ** -0.5 * LOG2E
KV_BLOCK = 512
Q_TILE = 512

_SPLIT = (
    ('gu', GM_WIDTH), ('gv', GM_WIDTH), ('cq', MLA_Q_LORA), ('ckv', MLA_KV_LORA), ('krope', MLA_ROPE),
    ('dq', 512), ('dk', 512), ('dv', 512), ('qi', IDX_HEADS * IDX_DIM), ('ki', IDX_DIM), ('wi', IDX_HEADS),
    ('fq', 512), ('fk', 512), ('fv', 512),
)
_ORDER = ('gu', 'gv', 'cq', 'dq', 'dk', 'dv', 'fq', 'fk', 'fv', 'ckv', 'qi', 'krope', 'ki', 'wi')


def _in_layout():
    src, acc = {}, 0
    for name, w in _SPLIT:
        src[name] = (acc, w)
        acc += w
    dst, off = {}, 0
    for name in _ORDER:
        w = src[name][1]
        pw = -(-w // LANE) * LANE
        dst[name] = (off, pw)
        off += pw
    total = -(-off // ROW_TILE) * ROW_TILE
    return src, dst, total


_SRC, _DST, IN_PAD = _in_layout()


def _cp(*sem):
    return pltpu.CompilerParams(dimension_semantics=sem, vmem_limit_bytes=VMEM_LIMIT_BYTES)


def _rms(x, g):
    return x * lax.rsqrt(jnp.mean(x * x, axis=-1, keepdims=True) + NORM_EPS) * g


def _ada_kernel(c_ref, w_ref, b_ref, o_ref):
    c = c_ref[...]
    a = (c * jax.nn.sigmoid(c)).astype(BF16)
    o_ref[...] = jnp.dot(a, w_ref[...].astype(BF16), preferred_element_type=F32) + b_ref[...]


def _ada(c_rows, ada_w, ada_b):
    L, D, N = ada_w.shape
    R = c_rows.shape[0]
    tn = 512
    return pl.pallas_call(
        _ada_kernel,
        grid=(L, N // tn),
        in_specs=[pl.BlockSpec((R, D), lambda l, j: (0, 0)),
                  pl.BlockSpec((None, D, tn), lambda l, j: (l, 0, j)),
                  pl.BlockSpec((None, 1, tn), lambda l, j: (l, 0, j))],
        out_specs=pl.BlockSpec((None, R, tn), lambda l, j: (l, 0, j)),
        out_shape=jax.ShapeDtypeStruct((L, R, N), F32),
        compiler_params=_cp("parallel", "parallel"),
        name="ada_mod",
    )(c_rows, ada_w, ada_b.reshape(L, 1, N))


def _norm_mod_kernel(x_ref, g_ref, sc_ref, sh_ref, h_ref):
    y = _rms(x_ref[...], g_ref[...])
    h_ref[...] = (y * (1.0 + sc_ref[...]) + sh_ref[...]).astype(BF16)


def _norm_mod_router_kernel(x_ref, g_ref, sc_ref, sh_ref, rwh_ref, rwl_ref, rb_ref, h_ref, gate_ref):
    y = _rms(x_ref[...], g_ref[...])
    h = y * (1.0 + sc_ref[...]) + sh_ref[...]
    h_hi = h.astype(BF16)
    h_ref[...] = h_hi
    h_lo = (h - h_hi.astype(F32)).astype(BF16)
    logits = (jnp.dot(h_hi, rwh_ref[...], preferred_element_type=F32)
              + jnp.dot(h_lo, rwh_ref[...], preferred_element_type=F32)
              + jnp.dot(h_hi, rwl_ref[...], preferred_element_type=F32)) + rb_ref[...]
    lane = lax.broadcasted_iota(I32, logits.shape, 1)
    lg = jnp.where(lane < N_EXPERTS, logits, -jnp.inf)
    m1 = jnp.max(lg, axis=-1, keepdims=True)
    i1 = jnp.min(jnp.where(lg == m1, lane, LANE), axis=-1, keepdims=True)
    lg2 = jnp.where(lane == i1, -jnp.inf, lg)
    m2 = jnp.max(lg2, axis=-1, keepdims=True)
    i2 = jnp.min(jnp.where(lg2 == m2, lane, LANE), axis=-1, keepdims=True)
    e2 = jnp.exp(m2 - m1)
    w1 = 1.0 / (1.0 + e2)
    gate_ref[...] = jnp.where(lane == i1, w1, 0.0) + jnp.where(lane == i2, e2 * w1, 0.0)


def _mod_spec(mod_map, k, D):
    return pl.BlockSpec((None, None, ROW_TILE, D), lambda i, *_: (k, mod_map(i), 0, 0))


def _norm_mod(x, g, modexp, mod_map, k_scale, k_shift, router=None):
    T, D = x.shape
    specs = [pl.BlockSpec((ROW_TILE, D), lambda i: (i, 0)),
             pl.BlockSpec((1, D), lambda i: (0, 0)),
             _mod_spec(mod_map, k_scale, D), _mod_spec(mod_map, k_shift, D)]
    args = [x, g.reshape(1, D), modexp, modexp]
    h_spec = pl.BlockSpec((ROW_TILE, D), lambda i: (i, 0))
    h_shape = jax.ShapeDtypeStruct((T, D), BF16)
    if router is None:
        return pl.pallas_call(
            _norm_mod_kernel, grid=(T // ROW_TILE,), in_specs=specs, out_specs=h_spec, out_shape=h_shape,
            compiler_params=_cp("parallel"), name="norm_mod")(*args)
    rw_hi, rw_lo, rb = router
    specs += [pl.BlockSpec((D, LANE), lambda i: (0, 0)), pl.BlockSpec((D, LANE), lambda i: (0, 0)),
              pl.BlockSpec((1, LANE), lambda i: (0, 0))]
    return pl.pallas_call(
        _norm_mod_router_kernel, grid=(T // ROW_TILE,), in_specs=specs,
        out_specs=[h_spec, pl.BlockSpec((ROW_TILE, LANE), lambda i: (i, 0))],
        out_shape=[h_shape, jax.ShapeDtypeStruct((T, LANE), F32)],
        compiler_params=_cp("parallel"), name="norm_mod_router")(*args, rw_hi, rw_lo, rb)


def _mm_kernel(a_ref, b_ref, o_ref):
    o_ref[...] = jnp.dot(a_ref[...], b_ref[...], preferred_element_type=F32).astype(o_ref.dtype)


def _mm(a, b, out_dtype, tm=ROW_TILE, tn=512, name="mm"):
    M, K = a.shape
    N = b.shape[1]
    return pl.pallas_call(
        _mm_kernel, grid=(M // tm, N // tn),
        in_specs=[pl.BlockSpec((tm, K), lambda i, j: (i, 0)), pl.BlockSpec((K, tn), lambda i, j: (0, j))],
        out_specs=pl.BlockSpec((tm, tn), lambda i, j: (i, j)),
        out_shape=jax.ShapeDtypeStruct((M, N), out_dtype),
        compiler_params=_cp("parallel", "parallel"), name=name)(a, b)


def _stage_weights(w_refs, wb_refs):
    @pl.when(pl.program_id(1) == 0)
    def _():
        for w_ref, wb_ref in zip(w_refs, wb_refs):
            wb_ref[...] = w_ref[...].astype(BF16)


def _mm_res_kernel(a_ref, b_ref, x_ref, g_ref, o_ref):
    o_ref[...] = x_ref[...] + g_ref[...] * jnp.dot(a_ref[...], b_ref[...], preferred_element_type=F32)


def _mm_res_ws_kernel(a_ref, w_ref, x_ref, g_ref, o_ref, wb_ref):
    _stage_weights([w_ref], [wb_ref])
    o_ref[...] = x_ref[...] + g_ref[...] * jnp.dot(a_ref[...], wb_ref[...], preferred_element_type=F32)


def _mm_res(a, b, x, modexp, mod_map, k_gate, tn, name="mm_res"):
    M, K = a.shape
    N = b.shape[1]
    tm = ROW_TILE
    out_shape = jax.ShapeDtypeStruct((M, N), F32)
    if b.dtype == BF16:
        return pl.pallas_call(
            _mm_res_kernel, grid=(M // tm, N // tn),
            in_specs=[pl.BlockSpec((tm, K), lambda i, j: (i, 0)),
                      pl.BlockSpec((K, tn), lambda i, j: (0, j)),
                      pl.BlockSpec((tm, tn), lambda i, j: (i, j)),
                      pl.BlockSpec((None, None, tm, tn), lambda i, j: (k_gate, mod_map(i), 0, j))],
            out_specs=pl.BlockSpec((tm, tn), lambda i, j: (i, j)), out_shape=out_shape,
            compiler_params=_cp("parallel", "parallel"), name=name)(a, b, x, modexp)
    return pl.pallas_call(
        _mm_res_ws_kernel, grid=(N // tn, M // tm),
        in_specs=[pl.BlockSpec((tm, K), lambda j, i: (i, 0)),
                  pl.BlockSpec((K, tn), lambda j, i: (0, j)),
                  pl.BlockSpec((tm, tn), lambda j, i: (i, j)),
                  pl.BlockSpec((None, None, tm, tn), lambda j, i: (k_gate, mod_map(i), 0, j))],
        out_specs=pl.BlockSpec((tm, tn), lambda j, i: (i, j)), out_shape=out_shape,
        scratch_shapes=[pltpu.VMEM((K, tn), BF16)],
        compiler_params=_cp("arbitrary", "arbitrary"), name=name)(a, b, x, modexp)


def _gmlp_kernel(gu_ref, gv_ref, lng_ref, lnb_ref, w_ref, bias_ref, o_ref, v_ref, *, c):
    u = jax.nn.gelu(gu_ref[...])
    gv = jax.nn.gelu(gv_ref[...])
    mu = jnp.mean(gv, axis=-1, keepdims=True)
    var = jnp.mean(jnp.square(gv - mu), axis=-1, keepdims=True)
    v = (gv - mu) * lax.rsqrt(var + NORM_EPS) * lng_ref[...] + lnb_ref[...]
    v_ref[...] = v
    vb = v.astype(BF16)
    row = lax.broadcasted_iota(I32, (c, c), 0)
    col = lax.broadcasted_iota(I32, (c, c), 1)
    bias = bias_ref[...]
    for g in range(GM_GROUPS):
        w = jnp.where(row >= col, w_ref[g], 0.0).astype(BF16)
        lo, hi = g * GM_GROUP_DIM, (g + 1) * GM_GROUP_DIM
        for n in range(ROW_TILE // c):
            r0, r1 = n * c, (n + 1) * c
            sp = jnp.dot(w, vb[r0:r1, lo:hi], preferred_element_type=F32) + bias[:, g:g + 1]
            o_ref[r0:r1, lo:hi] = (u[r0:r1, lo:hi] * sp).astype(BF16)


def _gmlp(hw, row0, rows, c, ln_g, ln_b, ws, bs):
    t0 = row0 // ROW_TILE
    gu_blk = _DST['gu'][0] // GM_WIDTH
    gv_blk = _DST['gv'][0] // GM_WIDTH
    w = ws[:, :c, :c]
    bias = bs[:, :c].T
    return pl.pallas_call(
        functools.partial(_gmlp_kernel, c=c), grid=(rows // ROW_TILE,),
        in_specs=[pl.BlockSpec((ROW_TILE, GM_WIDTH), lambda i: (t0 + i, gu_blk)),
                  pl.BlockSpec((ROW_TILE, GM_WIDTH), lambda i: (t0 + i, gv_blk)),
                  pl.BlockSpec((1, GM_WIDTH), lambda i: (0, 0)),
                  pl.BlockSpec((1, GM_WIDTH), lambda i: (0, 0)),
                  pl.BlockSpec((GM_GROUPS, c, c), lambda i: (0, 0, 0)),
                  pl.BlockSpec((c, GM_GROUPS), lambda i: (0, 0))],
        out_specs=[pl.BlockSpec((ROW_TILE, GM_WIDTH), lambda i: (i, 0)),
                   pl.BlockSpec((ROW_TILE, GM_WIDTH), lambda i: (i, 0))],
        out_shape=[jax.ShapeDtypeStruct((rows, GM_WIDTH), BF16), jax.ShapeDtypeStruct((rows, GM_WIDTH), F32)],
        compiler_params=_cp("parallel"), name="gmlp")(
            hw, hw, ln_g.reshape(1, -1), ln_b.reshape(1, -1), w, bias)


def _rope_tables(pos, rot, period):
    half = rot // 2
    inv = ROPE_THETA ** (-jnp.arange(half, dtype=F32) / half)
    ang = pos.astype(F32)[:, None] * inv[None, :]
    cos, sin = jnp.cos(ang), jnp.sin(ang)
    T = pos.shape[0]
    pad = jnp.zeros((T, period - rot), F32)
    c = jnp.concatenate([cos, cos, pad + 1.0], axis=1)
    s1 = jnp.concatenate([-sin, jnp.zeros_like(sin), pad], axis=1)
    s2 = jnp.concatenate([jnp.zeros_like(sin), sin, pad], axis=1)
    rep = LANE // period
    return jnp.stack([jnp.tile(c, (1, rep)), jnp.tile(s1, (1, rep)), jnp.tile(s2, (1, rep))], axis=0)


def _rope(x, tab_ref, half):
    w = x.shape[1]
    rep = w // LANE

    def wide(t):
        return t if rep == 1 else jnp.concatenate([t] * rep, axis=1)

    xl = pltpu.roll(x, w - half, axis=1)
    xr = pltpu.roll(x, half, axis=1)
    return x * wide(tab_ref[0]) + xl * wide(tab_ref[1]) + xr * wide(tab_ref[2])


def _state_kernel(cq_ref, ckv_ref, kr_ref, dq_ref, dk_ref, qi_ref, ki_ref, fq_ref, fk_ref,
                  qg_ref, kvg_ref, wqn_ref, wqp_ref, t64_ref, t32_ref, tidx_ref,
                  ckv_o, kpe_o, dsak_o, ki_o, diffk_o, qn_o, qp_o, dsaq_o, qi_o, diffq_o):
    cqn = _rms(cq_ref[...], qg_ref[...]).astype(BF16)
    qn_o[...] = (jnp.dot(cqn, wqn_ref[...], preferred_element_type=F32) * MLA_QSCALE).astype(BF16)
    qp = jnp.dot(cqn, wqp_ref[...], preferred_element_type=F32)
    qp_o[...] = (_rope(qp, t32_ref, MLA_ROPE // 2) * MLA_QSCALE).astype(BF16)
    ckv_o[...] = _rms(ckv_ref[...], kvg_ref[...])
    kpe_o[...] = _rope(kr_ref[...], t32_ref, MLA_ROPE // 2)
    dsaq_o[...] = (_rope(dq_ref[...], t64_ref, DSA_ROT // 2) * DSA_QSCALE).astype(BF16)
    dsak_o[...] = _rope(dk_ref[...], t64_ref, DSA_ROT // 2)
    qi_o[...] = _rope(qi_ref[...], tidx_ref, IDX_ROT // 2).astype(BF16)
    ki_o[...] = _rope(ki_ref[...], tidx_ref, IDX_ROT // 2)
    diffq_o[...] = (_rope(fq_ref[...], t64_ref, DIFF_ROT // 2) * DIFF_QSCALE).astype(BF16)
    diffk_o[...] = _rope(fk_ref[...], t64_ref, DIFF_ROT // 2)


def _state(hw, q_norm_g, kv_norm_g, w_uq_nope, w_uq_pe, t64, t32, tidx):
    T = hw.shape[0]

    def col(name):
        off, w = _DST[name]
        return pl.BlockSpec((ROW_TILE, w), lambda i: (i, off // w))

    def full(shape):
        return pl.BlockSpec(shape, lambda i: (0,) * len(shape))

    def tab():
        return pl.BlockSpec((3, ROW_TILE, LANE), lambda i: (0, i, 0))

    def out(w, dt):
        return pl.BlockSpec((ROW_TILE, w), lambda i: (i, 0)), jax.ShapeDtypeStruct((T, w), dt)

    outs = [out(MLA_KV_LORA, F32), out(LANE, F32), out(512, F32), out(LANE, F32), out(512, F32),
            out(512, BF16), out(MLA_HEADS * MLA_ROPE, BF16), out(512, BF16), out(IDX_HEADS * IDX_DIM, BF16),
            out(512, BF16)]
    return pl.pallas_call(
        _state_kernel, grid=(T // ROW_TILE,),
        in_specs=[col('cq'), col('ckv'), col('krope'), col('dq'), col('dk'), col('qi'), col('ki'),
                  col('fq'), col('fk'),
                  full((1, MLA_Q_LORA)), full((1, MLA_KV_LORA)),
                  full((MLA_Q_LORA, MLA_HEADS * MLA_NOPE)), full((MLA_Q_LORA, MLA_HEADS * MLA_ROPE)),
                  tab(), tab(), tab()],
        out_specs=[o[0] for o in outs], out_shape=[o[1] for o in outs],
        compiler_params=_cp("parallel"), name="mixer_state")(
            hw, hw, hw, hw, hw, hw, hw, hw, hw,
            q_norm_g.reshape(1, -1), kv_norm_g.reshape(1, -1), w_uq_nope, w_uq_pe, t64, t32, tidx)


def _block_offset(j):
    return j * KV_BLOCK if isinstance(j, int) else pl.multiple_of(j * KV_BLOCK, KV_BLOCK)


def _softmax_steps(scores, values, carries):
    stats = []
    for s, (m, l, _) in zip(scores, carries):
        m_new = jnp.maximum(m, jnp.max(s, axis=0, keepdims=True))
        alpha = jnp.exp2(m - m_new)
        p = jnp.exp2(s - m_new)
        stats.append((m_new, alpha * l + jnp.sum(p, axis=0, keepdims=True), alpha, p.astype(BF16)))
    pvs = [lax.dot_general(v, st[3], (((0,), (0,)), ((), ())), preferred_element_type=F32)
           for v, st in zip(values, stats)]
    return tuple((st[0], st[1], st[2] * c[2] + pv) for st, c, pv in zip(stats, carries, pvs))


def _softmax_init(dv, tq):
    return jnp.full((1, tq), NEG_INF, F32), jnp.zeros((1, tq), F32), jnp.zeros((dv, tq), F32)


def _attend_blocks(n_blocks, qk, values, carries, mask_last):
    carries = lax.fori_loop(0, n_blocks - 1, lambda j, c: _softmax_steps(qk(j), values(j), c), carries)
    scores = qk(n_blocks - 1)
    if mask_last is not None:
        scores = mask_last(scores)
    return _softmax_steps(scores, values(n_blocks - 1), carries)


def _pair_loop(lo, hi, step, carry):
    n2 = (hi - lo) // 2

    def two(t, c):
        j = lo + 2 * t
        return step(j + 1, step(j, c))

    carry = lax.fori_loop(0, n2, two, carry)
    return lax.fori_loop(lo + 2 * n2, hi, step, carry)


def _flash_kernel(*refs, tq, nkb, causal, s_valid, k_lanes, v_rows, dv, diff_layer):
    if diff_layer is None:
        qt_ref, k_ref, vt_ref, o_ref = refs
    else:
        qt_ref, k_ref, vt_ref, lam_ref, sg_ref, o_ref = refs
    i = pl.program_id(2)
    padded = s_valid < nkb * KV_BLOCK

    hg = len(k_lanes)
    n_blocks = i * tq // KV_BLOCK + 1 if causal else nkb
    qts = [_slot_query(qt_ref, hl, k_lanes.count(k_lanes[hl])) for hl in range(hg)]

    def qk(j):
        off = _block_offset(j)
        return tuple(jnp.dot(k_ref[pl.ds(off, KV_BLOCK), k0:k0 + LANE], qts[hl], preferred_element_type=F32)
                     for hl, k0 in enumerate(k_lanes))

    def values(j):
        return [vt_ref[pl.ds(_block_offset(j), KV_BLOCK), v0:v0 + dv] for v0 in v_rows]

    def mask_last(scores):
        kpos = (n_blocks - 1) * KV_BLOCK + lax.broadcasted_iota(I32, (KV_BLOCK, tq), 0)
        if causal:
            qpos = i * tq + lax.broadcasted_iota(I32, (KV_BLOCK, tq), 1)
            ok = kpos // CHUNK <= qpos // CHUNK
        else:
            ok = kpos < s_valid
        return [jnp.where(ok, s, NEG_INF) for s in scores]

    carries = _attend_blocks(n_blocks, qk, values, tuple(_softmax_init(dv, tq) for _ in range(hg)),
                             mask_last if (causal or padded) else None)
    heads = [acc / l for _, l, acc in carries]

    if diff_layer is None:
        for hl in range(hg):
            o_ref[v_rows[hl]:v_rows[hl] + dv, :] = heads[hl].astype(o_ref.dtype)
    else:
        lam_init = 0.8 - 0.6 * math.exp(-0.3 * diff_layer)
        lp = lam_ref[...]
        lam = (jnp.exp(jnp.sum(lp[0:1] * lp[1:2], axis=-1, keepdims=True))
               - jnp.exp(jnp.sum(lp[2:3] * lp[3:4], axis=-1, keepdims=True)) + lam_init)
        for h2 in range(hg // 2):
            o = heads[2 * h2] - lam * heads[2 * h2 + 1]
            y = o * lax.rsqrt(jnp.mean(o * o, axis=0, keepdims=True) + NORM_EPS) * sg_ref[...]
            o_ref[v_rows[2 * h2]:v_rows[2 * h2] + dv, :] = (y * (1.0 - lam_init)).astype(o_ref.dtype)


def _flash(qt, k, vt, *, tq, causal, s_valid, dv, k_width, k_lanes, diff=None, name="flash"):
    B, n_slots, _, Tq = qt.shape
    nkb = vt.shape[1] // KV_BLOCK
    S_pad = nkb * KV_BLOCK
    assert not causal or (tq <= KV_BLOCK and KV_BLOCK % tq == 0)
    hg = len(k_lanes)
    share = k_lanes.count(k_lanes[0])
    n_grp = n_slots * share // hg
    v_rows = tuple(hl // share * dv for hl in range(hg))
    in_specs = [pl.BlockSpec((None, hg // share, LANE, tq), lambda b, g, i: (b, g, 0, i)),
                pl.BlockSpec((None, S_pad, k_width), lambda b, g, i: (b, 0, g)),
                pl.BlockSpec((None, S_pad, MXU_WIDTH), lambda b, g, i: (b, 0, g))]
    args = [qt, k, vt]
    diff_layer = None
    if diff is not None:
        lam_params, subln_g, diff_layer = diff
        in_specs += [pl.BlockSpec(lam_params.shape, lambda b, g, i: (0, 0)),
                     pl.BlockSpec((dv, tq), lambda b, g, i: (0, 0))]
        args += [lam_params, jnp.broadcast_to(subln_g[:, None], (dv, tq))]
    kern = functools.partial(_flash_kernel, tq=tq, nkb=nkb, causal=causal, s_valid=s_valid,
                             k_lanes=k_lanes, v_rows=v_rows, dv=dv, diff_layer=diff_layer)
    return pl.pallas_call(
        kern, grid=(B, n_grp, Tq // tq), in_specs=in_specs,
        out_specs=pl.BlockSpec((None, MXU_WIDTH, tq), lambda b, g, i: (b, g, i)),
        out_shape=jax.ShapeDtypeStruct((B, n_grp * MXU_WIDTH, Tq), BF16),
        compiler_params=_cp("parallel", "parallel", "parallel"), name=name)(*args)


def _dsa_kernel(qit_ref, wit_ref, ki_ref, qt_ref, k_ref, vt_ref, o_ref, key_ref,
                *, tq, nkb, causal, s_valid, n_sel):
    i = pl.program_id(1)
    tkb = KV_BLOCK
    padded = s_valid < nkb * tkb
    nvis = jnp.minimum((i * tq + tq + tkb - 1) // tkb, nkb) if causal else nkb
    wit = wit_ref[...] * IDX_SCALE

    def kpos(j):
        return j * tkb + lax.broadcasted_iota(I32, (tkb, tq), 0)

    def score_block(j, masked):
        kib = ki_ref[pl.ds(_block_offset(j), tkb), :]
        rs = [jnp.dot(kib, qit_ref[h], preferred_element_type=F32) for h in range(IDX_HEADS)]
        sc = jnp.zeros((tkb, tq), F32)
        for h in range(IDX_HEADS):
            sc = sc + jnp.maximum(rs[h], 0.0) * wit[h:h + 1, :]
        sc = sc + 0.0
        bits = pltpu.bitcast(sc, I32)
        key = jnp.where(bits < 0, bits ^ 0x7FFFFFFF, bits)
        if masked and causal:
            qpos = i * tq + lax.broadcasted_iota(I32, (tkb, tq), 1)
            key = jnp.where(kpos(j) // CHUNK <= qpos // CHUNK, key, INT_MIN)
        elif masked:
            key = jnp.where(kpos(j) < s_valid, key, INT_MIN)
        key_ref[j] = key

    lax.fori_loop(0, nvis - 1, lambda j, c: (score_block(j, False), c)[1], 0)
    score_block(nvis - 1, causal or padded)

    def count(pred):
        def body(j, cnt):
            hit = jnp.where(pred(key_ref[j], j), 1.0, 0.0)
            part = hit[0:8]
            for r in range(1, tkb // 8):
                part = part + hit[r * 8:(r + 1) * 8]
            return cnt + part
        cnt = _pair_loop(0, nvis, body, jnp.zeros((8, tq), F32))
        return jnp.sum(cnt, axis=0, keepdims=True)

    gmax = lax.fori_loop(0, nvis, lambda j, m: jnp.maximum(m, key_ref[j]), jnp.full((tkb, tq), INT_MIN, I32))
    lo = jnp.min(gmax, axis=0, keepdims=True)
    hi = jnp.max(gmax, axis=0, keepdims=True)
    nbits = jnp.max((32 - lax.clz(lo ^ hi)).astype(F32)).astype(I32)
    low_mask = lax.shift_left(jnp.int32(1), jnp.minimum(nbits, 31)) - 1
    t0 = jnp.where(nbits >= 32, INT_MIN, lo & ~low_mask)

    def bit_step(it, t):
        cand = t + lax.shift_left(jnp.int32(1), nbits - 1 - it)
        cnt = count(lambda k, j: k >= cand)
        return jnp.where(cnt >= n_sel, cand, t)

    thr = lax.fori_loop(0, nbits, bit_step, t0)
    cnt_ge = count(lambda k, j: k >= thr)
    need_tie = jnp.logical_and(cnt_ge > n_sel, thr > INT_MIN)
    any_tie = jnp.max(jnp.where(need_tie, 1.0, 0.0)) > 0.0

    def write_bias(sel_of):
        def wr(j, _):
            key_ref[j] = pltpu.bitcast(jnp.where(sel_of(key_ref[j], j), 0.0, NEG_INF), I32)
            return 0
        lax.fori_loop(0, nvis, wr, 0)

    @pl.when(jnp.logical_not(any_tie))
    def _():
        tsel = jnp.maximum(thr, INT_MIN + 1)
        write_bias(lambda k, j: k >= tsel)

    @pl.when(any_tie)
    def _():
        cnt_gt = count(lambda k, j: k > thr)

        def idx_step(it, x):
            cand = x + lax.shift_left(jnp.int32(1), 14 - it)
            g = cnt_gt + count(lambda k, j: jnp.logical_and(k == thr, kpos(j) < cand))
            return jnp.where(g < n_sel, cand, x)

        xj = lax.fori_loop(0, 15, idx_step, jnp.zeros((1, tq), I32))
        jmax = jnp.where(need_tie, xj, jnp.where(thr == INT_MIN, -1, 2 ** 30))
        write_bias(lambda k, j: jnp.logical_or(k > thr, jnp.logical_and(k == thr, kpos(j) <= jmax)))

    qts = [_slot_query(qt_ref, hl, 2) for hl in range(DSA_HEADS)]

    def qk(j):
        off = _block_offset(j)
        bias = pltpu.bitcast(key_ref[j], F32)
        return tuple(jnp.dot(k_ref[pl.ds(off, tkb), hl // 2 * LANE:(hl // 2 + 1) * LANE], qts[hl],
                             preferred_element_type=F32) + bias for hl in range(DSA_HEADS))

    def values(j):
        off = _block_offset(j)
        return [vt_ref[pl.ds(off, tkb), hl * DSA_HEAD_DIM:(hl + 1) * DSA_HEAD_DIM] for hl in range(DSA_HEADS)]

    carries = _attend_blocks(nvis, qk, values,
                             tuple(_softmax_init(DSA_HEAD_DIM, tq) for _ in range(DSA_HEADS)), None)
    for hl, (_, l, acc) in enumerate(carries):
        o_ref[hl * DSA_HEAD_DIM:(hl + 1) * DSA_HEAD_DIM, :] = (acc / l).astype(o_ref.dtype)


def _dsa(qit, wit, ki, qt, k, vt, *, tq, causal, s_valid, n_sel, name="dsa"):
    B, H, _, Tq = qt.shape
    nkb = vt.shape[1] // KV_BLOCK
    S_pad = nkb * KV_BLOCK
    assert not causal or (tq <= KV_BLOCK and KV_BLOCK % tq == 0)
    assert KV_BLOCK >= n_sel
    W = DSA_HEADS * DSA_HEAD_DIM
    kern = functools.partial(_dsa_kernel, tq=tq, nkb=nkb, causal=causal, s_valid=s_valid, n_sel=n_sel)
    once = pl.Buffered(1)
    return pl.pallas_call(
        kern, grid=(B, Tq // tq),
        in_specs=[pl.BlockSpec((None, IDX_HEADS, LANE, tq), lambda b, i: (b, 0, 0, i)),
                  pl.BlockSpec((None, IDX_HEADS, tq), lambda b, i: (b, 0, i)),
                  pl.BlockSpec((None, S_pad, LANE), lambda b, i: (b, 0, 0), pipeline_mode=once),
                  pl.BlockSpec((None, H, LANE, tq), lambda b, i: (b, 0, 0, i)),
                  pl.BlockSpec((None, S_pad, W), lambda b, i: (b, 0, 0), pipeline_mode=once),
                  pl.BlockSpec((None, S_pad, W), lambda b, i: (b, 0, 0), pipeline_mode=once)],
        out_specs=pl.BlockSpec((None, W, tq), lambda b, i: (b, 0, i)),
        out_shape=jax.ShapeDtypeStruct((B, W, Tq), BF16),
        scratch_shapes=[pltpu.VMEM((nkb, KV_BLOCK, tq), I32)],
        compiler_params=_cp("parallel", "parallel"), name=name)(qit, wit, ki, qt, k, vt)


def _merge_kernel(h_ref, oa_ref, ob_ref, oc_ref, od_ref, wg0, wg1, wg2, wg3, bg0, bg1, bg2, bg3, wb_ref, o_ref,
                  wgb0, wgb1, wgb2, wgb3, wbb_ref):
    _stage_weights([wg0, wg1, wg2, wg3, wb_ref], [wgb0, wgb1, wgb2, wgb3, wbb_ref])
    h = h_ref[...]
    acc = None
    for n, (o_r, wg, bg) in enumerate(((oa_ref, wgb0, bg0), (ob_ref, wgb1, bg1), (oc_ref, wgb2, bg2),
                                       (od_ref, wgb3, bg3))):
        gate = jax.nn.sigmoid(jnp.dot(h, wg[...], preferred_element_type=F32) + bg[...])
        term = gate * jnp.dot(o_r[...], wbb_ref[n], preferred_element_type=F32)
        acc = term if acc is None else acc + term
    o_ref[...] = acc.astype(BF16)


def _merge(h, branches, w_gate, b_gate, w_branch, tn=512):
    T, D = h.shape
    nd = D // tn
    b_gate = b_gate.reshape(1, -1)
    once = pl.Buffered(1)
    specs = [pl.BlockSpec((ROW_TILE, D), lambda j, i: (i, 0))]
    specs += [pl.BlockSpec((ROW_TILE, BRANCH_WIDTH), lambda j, i: (i, 0))] * N_BRANCH
    specs += [pl.BlockSpec((D, tn), lambda j, i, n=n: (0, n * nd + j), pipeline_mode=once) for n in range(N_BRANCH)]
    specs += [pl.BlockSpec((1, tn), lambda j, i, n=n: (0, n * nd + j)) for n in range(N_BRANCH)]
    specs += [pl.BlockSpec((N_BRANCH, BRANCH_WIDTH, tn), lambda j, i: (0, 0, j), pipeline_mode=once)]
    return pl.pallas_call(
        _merge_kernel, grid=(nd, T // ROW_TILE), in_specs=specs,
        out_specs=pl.BlockSpec((ROW_TILE, tn), lambda j, i: (i, j)),
        out_shape=jax.ShapeDtypeStruct((T, D), BF16),
        scratch_shapes=[pltpu.VMEM((D, tn), BF16)] * N_BRANCH + [pltpu.VMEM((N_BRANCH, BRANCH_WIDTH, tn), BF16)],
        compiler_params=_cp("arbitrary", "arbitrary"), name="merge")(
            h, *branches, w_gate, w_gate, w_gate, w_gate, b_gate, b_gate, b_gate, b_gate, w_branch)


def _ffn_up_kernel(h_ref, w1_ref, w3_ref, o_ref, w1b_ref, w3b_ref):
    _stage_weights([w1_ref, w3_ref], [w1b_ref, w3b_ref])
    h = h_ref[...]
    a = jnp.dot(h, w1b_ref[...], preferred_element_type=F32)
    b = jnp.dot(h, w3b_ref[...], preferred_element_type=F32)
    o_ref[...] = (a * jax.nn.sigmoid(a) * b).astype(BF16)


def _moe_up_kernel(h_ref, w1_ref, w3_ref, gate_ref, o_ref, w1b_ref, w3b_ref):
    _stage_weights([w1_ref, w3_ref], [w1b_ref, w3b_ref])
    e = pl.program_id(0)
    h = h_ref[...]
    a = jnp.dot(h, w1b_ref[...], preferred_element_type=F32)
    b = jnp.dot(h, w3b_ref[...], preferred_element_type=F32)
    gate = gate_ref[...]
    lane = lax.broadcasted_iota(I32, gate.shape, 1)
    ge = jnp.sum(jnp.where(lane == e, gate, 0.0), axis=-1, keepdims=True)
    o_ref[...] = (a * jax.nn.sigmoid(a) * b * ge).astype(BF16)


def _ffn_up(h, w1, w3, tn=512):
    T, D = h.shape
    F = w1.shape[1]
    return pl.pallas_call(
        _ffn_up_kernel, grid=(F // tn, T // ROW_TILE),
        in_specs=[pl.BlockSpec((ROW_TILE, D), lambda j, i: (i, 0)),
                  pl.BlockSpec((D, tn), lambda j, i: (0, j)), pl.BlockSpec((D, tn), lambda j, i: (0, j))],
        out_specs=pl.BlockSpec((ROW_TILE, tn), lambda j, i: (i, j)),
        out_shape=jax.ShapeDtypeStruct((T, F), BF16),
        scratch_shapes=[pltpu.VMEM((D, tn), BF16)] * 2,
        compiler_params=_cp("arbitrary", "arbitrary"), name="ffn_up")(h, w1, w3)


def _moe_up(h, w1, w3, gate):
    T, D = h.shape
    E, _, F = w1.shape
    once = pl.Buffered(1)
    return pl.pallas_call(
        _moe_up_kernel, grid=(E, T // ROW_TILE),
        in_specs=[pl.BlockSpec((ROW_TILE, D), lambda e, i: (i, 0)),
                  pl.BlockSpec((None, D, F), lambda e, i: (e, 0, 0), pipeline_mode=once),
                  pl.BlockSpec((None, D, F), lambda e, i: (e, 0, 0), pipeline_mode=once),
                  pl.BlockSpec((ROW_TILE, LANE), lambda e, i: (i, 0))],
        out_specs=pl.BlockSpec((ROW_TILE, F), lambda e, i: (i, e)),
        out_shape=jax.ShapeDtypeStruct((T, E * F), BF16),
        scratch_shapes=[pltpu.VMEM((D, F), BF16)] * 2,
        compiler_params=_cp("arbitrary", "arbitrary"), name="moe_up")(h, w1, w3, gate)


def _final_norm_kernel(x_ref, g_ref, o_ref):
    o_ref[...] = _rms(x_ref[...], g_ref[...])


def _final_norm(x, g):
    T, D = x.shape
    return pl.pallas_call(
        _final_norm_kernel, grid=(T // ROW_TILE,),
        in_specs=[pl.BlockSpec((ROW_TILE, D), lambda i: (i, 0)), pl.BlockSpec((1, D), lambda i: (0, 0))],
        out_specs=pl.BlockSpec((ROW_TILE, D), lambda i: (i, 0)),
        out_shape=jax.ShapeDtypeStruct((T, D), F32),
        compiler_params=_cp("parallel"), name="final_norm")(x, g.reshape(1, D))


def _pad_rows(a, s_pad):
    pad = s_pad - a.shape[1]
    if pad == 0:
        return a
    return jnp.pad(a, ((0, 0), (0, pad)) + ((0, 0),) * (a.ndim - 2))


def _queries_t(q, n_slots, tq_pad):
    B, T, W = q.shape
    d = W // n_slots
    qt = q.reshape(B, T, n_slots, d).transpose(0, 2, 3, 1)
    if d < LANE:
        qt = jnp.pad(qt, ((0, 0), (0, 0), (0, LANE - d), (0, 0)))
    reps = -(-tq_pad // T)
    return jnp.concatenate([qt] * reps, axis=-1)[..., :tq_pad] if reps > 1 else qt


def _slot_query(qt_ref, hl, heads_per_slot):
    if heads_per_slot == 1:
        return qt_ref[hl]
    qp = qt_ref[hl // heads_per_slot]
    d = LANE // heads_per_slot
    r = hl % heads_per_slot
    parts = [qp[s * d:(s + 1) * d] if s == r else jnp.zeros((d, qp.shape[1]), qp.dtype)
             for s in range(heads_per_slot)]
    return jnp.concatenate(parts, axis=0)


def _rows_out(ot, T):
    B, W, _ = ot.shape
    return ot[:, :, :T].transpose(0, 2, 1).reshape(B * T, W)


def _pad_in_weight(w_in):
    D = w_in.shape[0]
    out = jnp.zeros((D, IN_PAD), BF16)
    for name in _ORDER:
        s0, sw = _SRC[name]
        d0, _ = _DST[name]
        out = lax.dynamic_update_slice(out, w_in[:, s0:s0 + sw].astype(BF16), (0, d0))
    return out


def _col(hw, name, width=None):
    off, w = _DST[name]
    return hw[:, off:off + (width or w)]


def _mixers(l, hw, st, caches, lp, geom):
    n_p, B_p, T_p, B_s, T_s, past = geom
    ckv_n, kpe, dsa_k, ki, diff_k, q_nope, q_pe, dsa_q, qi, diff_q = st
    wi = _col(hw, 'wi')
    dsa_v = _col(hw, 'dv')
    diff_v = _col(hw, 'fv')

    def split(a):
        W = a.shape[1]
        return a[:n_p].reshape(B_p, T_p, W), a[n_p:].reshape(B_s, T_s, W)

    o_a_p, v_p = _gmlp(hw, 0, n_p, min(T_p, GM_CHUNK), lp['gm_ln_g'], lp['gm_ln_b'], lp['gm_ws'], lp['gm_bs'])
    o_a_s, v_s = _gmlp(hw, n_p, hw.shape[0] - n_p, min(T_s, GM_CHUNK), lp['gm_ln_g'], lp['gm_ln_b'],
                       lp['gm_ws'], lp['gm_bs'])
    c_p, c_s = min(T_p, GM_CHUNK), min(T_s, GM_CHUNK)
    gm_v_p = v_p.reshape(B_p, T_p, GM_WIDTH)[:, T_p - c_p:]
    gm_v_s = v_s.reshape(B_s, T_s, GM_WIDTH)[:, T_s - c_s:]

    outs = {}
    for grp in ('p', 's'):
        pi = 0 if grp == 'p' else 1
        B, T = (B_p, T_p) if grp == 'p' else (B_s, T_s)

        def g(a, pi=pi):
            return split(a)[pi]

        if grp == 'p':
            S = s_pad = T
            tq = min(Q_TILE, T)
            tq_pad = T
            causal = True

            def ext(new, cache):
                return new
        else:
            S = past + T
            s_pad = -(-S // KV_BLOCK) * KV_BLOCK
            tq = tq_pad = -(-T // LANE) * LANE
            causal = False

            def ext(new, cache, s_pad=s_pad):
                return _pad_rows(jnp.concatenate([cache.reshape(cache.shape[:2] + (-1,)), new], axis=1), s_pad)

        cm_ckv, cm_kr, cd_k, cd_v, cd_ki, cf_k, cf_v = caches

        all_c = ext(g(ckv_n), cm_ckv).astype(BF16)
        all_pe = ext(g(kpe)[..., :MLA_ROPE], cm_kr).astype(BF16)
        kv = _mm(all_c.reshape(B * s_pad, MLA_KV_LORA), lp['w_ukv'], BF16,
                 tm=ROW_TILE if (B * s_pad) % ROW_TILE == 0 else LANE, name="mla_kv")
        kv = kv.reshape(B, s_pad, 2 * MLA_HEADS * MLA_NOPE)
        k_nope = kv[..., :MLA_HEADS * MLA_NOPE].reshape(B, s_pad, MLA_HEADS, MLA_NOPE)
        k_mla = jnp.concatenate(
            [k_nope, jnp.broadcast_to(all_pe[:, :, None, :], (B, s_pad, MLA_HEADS, MLA_ROPE)),
             jnp.zeros((B, s_pad, MLA_HEADS, LANE - MLA_NOPE - MLA_ROPE), BF16)], axis=-1)
        q_mla = jnp.concatenate([g(q_nope).reshape(B, T, MLA_HEADS, MLA_NOPE),
                                 g(q_pe).reshape(B, T, MLA_HEADS, MLA_ROPE)], axis=-1)
        o_b = _flash(_queries_t(q_mla.reshape(B, T, -1), MLA_HEADS, tq_pad),
                     k_mla.reshape(B, s_pad, MLA_HEADS * LANE), kv[..., MLA_HEADS * MLA_NOPE:],
                     tq=tq, causal=causal, s_valid=S, dv=MLA_V, k_width=HEAD_GROUP * LANE,
                     k_lanes=tuple(h * LANE for h in range(HEAD_GROUP)), name="mla_attn")

        k_all = ext(g(dsa_k), cd_k).astype(BF16)
        v_all = ext(g(dsa_v), cd_v).astype(BF16)
        ki_all = ext(g(ki)[..., :IDX_DIM], cd_ki).astype(BF16)
        ki_all = jnp.pad(ki_all, ((0, 0), (0, 0), (0, LANE - IDX_DIM)))
        wit = _queries_t(g(wi)[..., :IDX_HEADS], IDX_HEADS, tq_pad)[:, :, 0, :]
        o_c = _dsa(_queries_t(g(qi), IDX_HEADS, tq_pad), wit, ki_all,
                   _queries_t(g(dsa_q), DSA_HEADS // 2, tq_pad), k_all, v_all,
                   tq=tq, causal=causal, s_valid=S, n_sel=min(DSA_TOPK, S // 4))

        fk_all = ext(g(diff_k), cf_k).astype(BF16)
        fv_all = ext(g(diff_v), cf_v).astype(BF16)
        o_d = _flash(_queries_t(g(diff_q), DIFF_HEADS, tq_pad), fk_all, fv_all,
                     tq=tq, causal=causal, s_valid=S, dv=DIFF_V_DIM, k_width=MXU_WIDTH,
                     k_lanes=(0, 0, LANE, LANE),
                     diff=(lp['diff_lambda'], lp['diff_subln_g'], l), name="diff_attn")
        outs[grp] = (_rows_out(o_b, T), _rows_out(o_c, T), _rows_out(o_d, T))

    o_a = jnp.concatenate([o_a_p, o_a_s], axis=0)
    o_b, o_c, o_d = (jnp.concatenate([outs['p'][k], outs['s'][k]], axis=0) for k in range(3))

    def state(grp):
        pi = 0 if grp == 'p' else 1
        B, T = (B_p, T_p) if grp == 'p' else (B_s, T_s)

        def g(a):
            return split(a)[pi]
        return (gm_v_p if grp == 'p' else gm_v_s,
                g(ckv_n), g(kpe)[..., :MLA_ROPE],
                g(dsa_k).reshape(B, T, DSA_HEADS, DSA_HEAD_DIM), g(dsa_v).reshape(B, T, DSA_HEADS, DSA_HEAD_DIM),
                g(ki)[..., :IDX_DIM],
                g(diff_k).reshape(B, T, DIFF_HEADS, 2, DIFF_QK_DIM), g(diff_v).reshape(B, T, DIFF_HEADS, DIFF_V_DIM))

    return (o_a, o_b, o_c, o_d), state('p'), state('s')


def kernel(x_prompt, x_sample, c_prompt, c_sample, cache_mla_ckv, cache_mla_krope, cache_dsa_k, cache_dsa_v, cache_dsa_kidx, cache_diff_k, cache_diff_v, ada_w, ada_b, norm_mix_g, norm_ff_g, w_in, w_gate, b_gate, gm_ln_g, gm_ln_b, gm_ws, gm_bs, mla_q_norm_g, mla_kv_norm_g, mla_w_uq, mla_w_ukv, diff_lambda, diff_subln_g, w_branch, w_out, ffn_w1, ffn_w3, ffn_w2, moe_router_w, moe_router_b, moe_w1, moe_w3, moe_w2, final_norm_g):
    B_p, T_p, D = x_prompt.shape
    B_s, T_s, _ = x_sample.shape
    depth = ada_w.shape[0]
    past = cache_mla_ckv.shape[2]
    n_p, n_s = B_p * T_p, B_s * T_s
    assert T_p % ROW_TILE == 0 and n_s % ROW_TILE == 0 and ROW_TILE % T_s == 0
    assert past % CHUNK == 0 and T_s <= CHUNK
    n_p_tiles, tiles_per_b = n_p // ROW_TILE, T_p // ROW_TILE
    geom = (n_p, B_p, T_p, B_s, T_s, past)

    def mod_map(i):
        return jnp.where(i < n_p_tiles, i // tiles_per_b, B_p + i - n_p_tiles)

    x = jnp.concatenate([x_prompt.reshape(n_p, D), x_sample.reshape(n_s, D)], axis=0)

    n_c = B_p + B_s
    c_rows = jnp.concatenate([c_prompt, c_sample, jnp.zeros((-n_c % 8, D), F32)], axis=0)
    mod = _ada(c_rows, ada_w, ada_b)
    mod = mod[:, :n_c].reshape(depth, n_c, 6, D).transpose(0, 2, 1, 3)
    mod_p = jnp.broadcast_to(mod[:, :, :B_p, None, :], (depth, 6, B_p, ROW_TILE, D))
    mod_s = jnp.broadcast_to(mod[:, :, B_p:, None, :], (depth, 6, B_s, T_s, D)).reshape(
        depth, 6, n_s // ROW_TILE, ROW_TILE, D)
    modexp = jnp.concatenate([mod_p, mod_s], axis=2)

    pos = jnp.concatenate([jnp.tile(jnp.arange(T_p, dtype=I32), B_p),
                           jnp.tile(past + jnp.arange(T_s, dtype=I32), B_s)])
    t64 = _rope_tables(pos, DSA_ROT, DSA_HEAD_DIM)
    t32 = _rope_tables(pos, MLA_ROPE, MLA_ROPE)
    tidx = _rope_tables(pos, IDX_ROT, IDX_DIM)

    states_p, states_s = [], []
    for l in range(depth):
        uq = mla_w_uq[l].reshape(MLA_Q_LORA, MLA_HEADS, MLA_NOPE + MLA_ROPE)
        ukv = mla_w_ukv[l].reshape(MLA_KV_LORA, MLA_HEADS, MLA_NOPE + MLA_V)
        lp = {
            'gm_ln_g': gm_ln_g[l], 'gm_ln_b': gm_ln_b[l], 'gm_ws': gm_ws[l], 'gm_bs': gm_bs[l],
            'w_ukv': jnp.concatenate([ukv[..., :MLA_NOPE].reshape(MLA_KV_LORA, -1),
                                      ukv[..., MLA_NOPE:].reshape(MLA_KV_LORA, -1)], axis=1).astype(BF16),
            'diff_lambda': diff_lambda[l], 'diff_subln_g': diff_subln_g[l],
        }
        h = _norm_mod(x, norm_mix_g[l], modexp[l], mod_map, 1, 0)
        hw = _mm(h, _pad_in_weight(w_in[l]), F32, name="in_proj")
        st = _state(hw, mla_q_norm_g[l], mla_kv_norm_g[l],
                    uq[..., :MLA_NOPE].reshape(MLA_Q_LORA, -1).astype(BF16),
                    uq[..., MLA_NOPE:].reshape(MLA_Q_LORA, -1).astype(BF16), t64, t32, tidx)
        caches = (cache_mla_ckv[l], cache_mla_krope[l], cache_dsa_k[l], cache_dsa_v[l], cache_dsa_kidx[l],
                  cache_diff_k[l], cache_diff_v[l])
        branches, st_p, st_s = _mixers(l, hw, st, caches, lp, geom)
        states_p.append(st_p)
        states_s.append(st_s)
        merged = _merge(h, branches, w_gate[l], b_gate[l], w_branch[l])
        x = _mm_res(merged, w_out[l], x, modexp[l], mod_map, 2, tn=512, name="out_proj")

        j = l // 2
        if l % 2 == 0:
            h2 = _norm_mod(x, norm_ff_g[l], modexp[l], mod_map, 4, 3)
            act = _ffn_up(h2, ffn_w1[j], ffn_w3[j])
            w2 = ffn_w2[j]
        else:
            rw = jnp.zeros((D, LANE), F32).at[:, :N_EXPERTS].set(moe_router_w[j])
            rw_hi = rw.astype(BF16)
            rw_lo = (rw - rw_hi.astype(F32)).astype(BF16)
            rb = jnp.zeros((1, LANE), F32).at[0, :N_EXPERTS].set(moe_router_b[j])
            h2, gate = _norm_mod(x, norm_ff_g[l], modexp[l], mod_map, 4, 3, router=(rw_hi, rw_lo, rb))
            act = _moe_up(h2, moe_w1[j], moe_w3[j], gate)
            w2 = moe_w2[j].astype(BF16).reshape(-1, D)
        x = _mm_res(act, w2, x, modexp[l], mod_map, 5, tn=512 if w2.dtype == F32 else MXU_WIDTH, name="ffn_down")

    y = _final_norm(x, final_norm_g)
    outs = [y[:n_p].reshape(B_p, T_p, D), y[n_p:].reshape(B_s, T_s, D)]
    for k in range(8):
        outs.append(jnp.stack([s[k] for s in states_p], axis=0))
        outs.append(jnp.stack([s[k] for s in states_s], axis=0))
    return tuple(outs)
```

```python
import functools
import math

import jax
import jax.numpy as jnp
from jax import lax
from jax.experimental import pallas as pl
from jax.experimental.pallas import tpu as pltpu

F32 = jnp.float32
BF16 = jnp.bfloat16
I32 = jnp.int32

CHUNK = 64
ROPE_THETA = 500000.0
NORM_EPS = 1e-6
NEG_INF = -1e30
N_BRANCH = 4
BRANCH_WIDTH = 512
GM_CHUNK = 128
GM_GROUPS = 4
GM_WIDTH = 512
GM_GROUP_DIM = GM_WIDTH // GM_GROUPS
MLA_HEADS = 8
MLA_Q_LORA = 512
MLA_KV_LORA = 256
MLA_NOPE = 64
MLA_ROPE = 32
MLA_V = 64
DSA_HEADS = 8
DSA_HEAD_DIM = 64
DSA_ROT = DSA_HEAD_DIM // 4
IDX_HEADS = 8
IDX_DIM = 32
IDX_ROT = IDX_DIM // 4
IDX_SCALE = (IDX_DIM ** -0.5) * (IDX_HEADS ** -0.5)
DSA_TOPK = 256
DIFF_HEADS = 4
DIFF_QK_DIM = 64
DIFF_V_DIM = 128
DIFF_ROT = DIFF_QK_DIM // 4
N_EXPERTS = 8

LANE = 128
MXU_WIDTH = 256
VMEM_LIMIT_BYTES = 56 * 1024 * 1024

ROW_TILE = 512
HEAD_GROUP = MXU_WIDTH // DSA_HEAD_DIM
LOG2E = 1.4426950408889634
INT_MIN = -2 ** 31
MLA_QSCALE = (MLA_NOPE + MLA_ROPE) ** -0.5 * LOG2E
DSA_QSCALE = DSA_HEAD_DIM ** -0.5 * LOG2E
DIFF_QSCALE = DIFF_QK_DIM ** -0.5 * LOG2E
KV_BLOCK = 512
Q_TILE = 512

_SPLIT = (
    ('gu', GM_WIDTH), ('gv', GM_WIDTH), ('cq', MLA_Q_LORA), ('ckv', MLA_KV_LORA), ('krope', MLA_ROPE),
    ('dq', 512), ('dk', 512), ('dv', 512), ('qi', IDX_HEADS * IDX_DIM), ('ki', IDX_DIM), ('wi', IDX_HEADS),
    ('fq', 512), ('fk', 512), ('fv', 512),
)
_ORDER = ('gu', 'gv', 'cq', 'dq', 'dk', 'dv', 'fq', 'fk', 'fv', 'ckv', 'qi', 'krope', 'ki', 'wi')


def _in_layout():
    src, acc = {}, 0
    for name, w in _SPLIT:
        src[name] = (acc, w)
        acc += w
    dst, off = {}, 0
    for name in _ORDER:
        w = src[name][1]
        pw = -(-w // LANE) * LANE
        dst[name] = (off, pw)
        off += pw
    total = -(-off // ROW_TILE) * ROW_TILE
    return src, dst, total


_SRC, _DST, IN_PAD = _in_layout()


def _cp(*sem):
    return pltpu.CompilerParams(dimension_semantics=sem, vmem_limit_bytes=VMEM_LIMIT_BYTES)


def _rms(x, g):
    return x * lax.rsqrt(jnp.mean(x * x, axis=-1, keepdims=True) + NORM_EPS) * g


def _ada_kernel(c_ref, w_ref, b_ref, o_ref):
    c = c_ref[...]
    a = (c * jax.nn.sigmoid(c)).astype(BF16)
    o_ref[...] = jnp.dot(a, w_ref[...].astype(BF16), preferred_element_type=F32) + b_ref[...]


def _ada(c_rows, ada_w, ada_b):
    L, D, N = ada_w.shape
    R = c_rows.shape[0]
    tn = 512
    return pl.pallas_call(
        _ada_kernel,
        grid=(L, N // tn),
        in_specs=[pl.BlockSpec((R, D), lambda l, j: (0, 0)),
                  pl.BlockSpec((None, D, tn), lambda l, j: (l, 0, j)),
                  pl.BlockSpec((None, 1, tn), lambda l, j: (l, 0, j))],
        out_specs=pl.BlockSpec((None, R, tn), lambda l, j: (l, 0, j)),
        out_shape=jax.ShapeDtypeStruct((L, R, N), F32),
        compiler_params=_cp("parallel", "parallel"),
        name="ada_mod",
    )(c_rows, ada_w, ada_b.reshape(L, 1, N))


def _norm_mod_kernel(x_ref, g_ref, sc_ref, sh_ref, h_ref):
    y = _rms(x_ref[...], g_ref[...])
    h_ref[...] = (y * (1.0 + sc_ref[...]) + sh_ref[...]).astype(BF16)


def _norm_mod_router_kernel(x_ref, g_ref, sc_ref, sh_ref, rwh_ref, rwl_ref, rb_ref, h_ref, gate_ref):
    y = _rms(x_ref[...], g_ref[...])
    h = y * (1.0 + sc_ref[...]) + sh_ref[...]
    h_hi = h.astype(BF16)
    h_ref[...] = h_hi
    h_lo = (h - h_hi.astype(F32)).astype(BF16)
    logits = (jnp.dot(h_hi, rwh_ref[...], preferred_element_type=F32)
              + jnp.dot(h_lo, rwh_ref[...], preferred_element_type=F32)
              + jnp.dot(h_hi, rwl_ref[...], preferred_element_type=F32)) + rb_ref[...]
    lane = lax.broadcasted_iota(I32, logits.shape, 1)
    lg = jnp.where(lane < N_EXPERTS, logits, -jnp.inf)
    m1 = jnp.max(lg, axis=-1, keepdims=True)
    i1 = jnp.min(jnp.where(lg == m1, lane, LANE), axis=-1, keepdims=True)
    lg2 = jnp.where(lane == i1, -jnp.inf, lg)
    m2 = jnp.max(lg2, axis=-1, keepdims=True)
    i2 = jnp.min(jnp.where(lg2 == m2, lane, LANE), axis=-1, keepdims=True)
    e2 = jnp.exp(m2 - m1)
    w1 = 1.0 / (1.0 + e2)
    gate_ref[...] = jnp.where(lane == i1, w1, 0.0) + jnp.where(lane == i2, e2 * w1, 0.0)


def _mod_spec(mod_map, k, D):
    return pl.BlockSpec((None, None, ROW_TILE, D), lambda i, *_: (k, mod_map(i), 0, 0))


def _norm_mod(x, g, modexp, mod_map, k_scale, k_shift, router=None):
    T, D = x.shape
    specs = [pl.BlockSpec((ROW_TILE, D), lambda i: (i, 0)),
             pl.BlockSpec((1, D), lambda i: (0, 0)),
             _mod_spec(mod_map, k_scale, D), _mod_spec(mod_map, k_shift, D)]
    args = [x, g.reshape(1, D), modexp, modexp]
    h_spec = pl.BlockSpec((ROW_TILE, D), lambda i: (i, 0))
    h_shape = jax.ShapeDtypeStruct((T, D), BF16)
    if router is None:
        return pl.pallas_call(
            _norm_mod_kernel, grid=(T // ROW_TILE,), in_specs=specs, out_specs=h_spec, out_shape=h_shape,
            compiler_params=_cp("parallel"), name="norm_mod")(*args)
    rw_hi, rw_lo, rb = router
    specs += [pl.BlockSpec((D, LANE), lambda i: (0, 0)), pl.BlockSpec((D, LANE), lambda i: (0, 0)),
              pl.BlockSpec((1, LANE), lambda i: (0, 0))]
    return pl.pallas_call(
        _norm_mod_router_kernel, grid=(T // ROW_TILE,), in_specs=specs,
        out_specs=[h_spec, pl.BlockSpec((ROW_TILE, LANE), lambda i: (i, 0))],
        out_shape=[h_shape, jax.ShapeDtypeStruct((T, LANE), F32)],
        compiler_params=_cp("parallel"), name="norm_mod_router")(*args, rw_hi, rw_lo, rb)


def _mm_kernel(a_ref, b_ref, o_ref):
    o_ref[...] = jnp.dot(a_ref[...], b_ref[...], preferred_element_type=F32).astype(o_ref.dtype)


def _mm(a, b, out_dtype, tm=ROW_TILE, tn=512, name="mm"):
    M, K = a.shape
    N = b.shape[1]
    return pl.pallas_call(
        _mm_kernel, grid=(M // tm, N // tn),
        in_specs=[pl.BlockSpec((tm, K), lambda i, j: (i, 0)), pl.BlockSpec((K, tn), lambda i, j: (0, j))],
        out_specs=pl.BlockSpec((tm, tn), lambda i, j: (i, j)),
        out_shape=jax.ShapeDtypeStruct((M, N), out_dtype),
        compiler_params=_cp("parallel", "parallel"), name=name)(a, b)


def _mla_kv_kernel(c_ref, pe_ref, wk_ref, wv_ref, k_ref, v_ref):
    c = c_ref[...]
    slot = pltpu.roll(pe_ref[...].astype(F32), MLA_NOPE, axis=1)
    k = jnp.dot(c, wk_ref[...], preferred_element_type=F32) + jnp.concatenate([slot] * MLA_HEADS, axis=1)
    k_ref[...] = k.astype(BF16)
    v_ref[...] = jnp.dot(c, wv_ref[...], preferred_element_type=F32).astype(BF16)


def _mla_kv(c, pe, w_uk_slots, w_uv):
    M, K = c.shape
    nk, nv = w_uk_slots.shape[1], w_uv.shape[1]
    tm = ROW_TILE
    return pl.pallas_call(
        _mla_kv_kernel, grid=(M // tm,),
        in_specs=[pl.BlockSpec((tm, K), lambda i: (i, 0)), pl.BlockSpec((tm, LANE), lambda i: (i, 0)),
                  pl.BlockSpec((K, nk), lambda i: (0, 0)), pl.BlockSpec((K, nv), lambda i: (0, 0))],
        out_specs=[pl.BlockSpec((tm, nk), lambda i: (i, 0)), pl.BlockSpec((tm, nv), lambda i: (i, 0))],
        out_shape=[jax.ShapeDtypeStruct((M, nk), BF16), jax.ShapeDtypeStruct((M, nv), BF16)],
        compiler_params=_cp("parallel"), name="mla_kv")(c, pe, w_uk_slots, w_uv)


def _stage_weights(w_refs, wb_refs):
    @pl.when(pl.program_id(1) == 0)
    def _():
        for w_ref, wb_ref in zip(w_refs, wb_refs):
            wb_ref[...] = w_ref[...].astype(BF16)


def _mm_res_kernel(a_ref, b_ref, x_ref, g_ref, o_ref):
    o_ref[...] = x_ref[...] + g_ref[...] * jnp.dot(a_ref[...], b_ref[...], preferred_element_type=F32)


def _mm_res_ws_kernel(a_ref, w_ref, x_ref, g_ref, o_ref, wb_ref):
    _stage_weights([w_ref], [wb_ref])
    o_ref[...] = x_ref[...] + g_ref[...] * jnp.dot(a_ref[...], wb_ref[...], preferred_element_type=F32)


def _mm_res(a, b, x, modexp, mod_map, k_gate, tn, name="mm_res"):
    M, K = a.shape
    N = b.shape[1]
    tm = ROW_TILE
    out_shape = jax.ShapeDtypeStruct((M, N), F32)
    if b.dtype == BF16:
        return pl.pallas_call(
            _mm_res_kernel, grid=(M // tm, N // tn),
            in_specs=[pl.BlockSpec((tm, K), lambda i, j: (i, 0)),
                      pl.BlockSpec((K, tn), lambda i, j: (0, j)),
                      pl.BlockSpec((tm, tn), lambda i, j: (i, j)),
                      pl.BlockSpec((None, None, tm, tn), lambda i, j: (k_gate, mod_map(i), 0, j))],
            out_specs=pl.BlockSpec((tm, tn), lambda i, j: (i, j)), out_shape=out_shape,
            compiler_params=_cp("parallel", "parallel"), name=name)(a, b, x, modexp)
    return pl.pallas_call(
        _mm_res_ws_kernel, grid=(N // tn, M // tm),
        in_specs=[pl.BlockSpec((tm, K), lambda j, i: (i, 0)),
                  pl.BlockSpec((K, tn), lambda j, i: (0, j)),
                  pl.BlockSpec((tm, tn), lambda j, i: (i, j)),
                  pl.BlockSpec((None, None, tm, tn), lambda j, i: (k_gate, mod_map(i), 0, j))],
        out_specs=pl.BlockSpec((tm, tn), lambda j, i: (i, j)), out_shape=out_shape,
        scratch_shapes=[pltpu.VMEM((K, tn), BF16)],
        compiler_params=_cp("arbitrary", "arbitrary"), name=name)(a, b, x, modexp)


def _gmlp_kernel(gu_ref, gv_ref, lng_ref, lnb_ref, w_ref, bias_ref, o_ref, v_ref, *, c):
    u = jax.nn.gelu(gu_ref[...])
    gv = jax.nn.gelu(gv_ref[...])
    mu = jnp.mean(gv, axis=-1, keepdims=True)
    var = jnp.mean(jnp.square(gv - mu), axis=-1, keepdims=True)
    v = (gv - mu) * lax.rsqrt(var + NORM_EPS) * lng_ref[...] + lnb_ref[...]
    v_ref[...] = v
    vb = v.astype(BF16)
    row = lax.broadcasted_iota(I32, (c, c), 0)
    col = lax.broadcasted_iota(I32, (c, c), 1)
    bias = bias_ref[...]
    for g in range(GM_GROUPS):
        w = jnp.where(row >= col, w_ref[g], 0.0).astype(BF16)
        lo, hi = g * GM_GROUP_DIM, (g + 1) * GM_GROUP_DIM
        for n in range(ROW_TILE // c):
            r0, r1 = n * c, (n + 1) * c
            sp = jnp.dot(w, vb[r0:r1, lo:hi], preferred_element_type=F32) + bias[:, g:g + 1]
            o_ref[r0:r1, lo:hi] = (u[r0:r1, lo:hi] * sp).astype(BF16)


def _gmlp(hw, row0, rows, c, ln_g, ln_b, ws, bs):
    t0 = row0 // ROW_TILE
    gu_blk = _DST['gu'][0] // GM_WIDTH
    gv_blk = _DST['gv'][0] // GM_WIDTH
    w = ws[:, :c, :c]
    bias = bs[:, :c].T
    return pl.pallas_call(
        functools.partial(_gmlp_kernel, c=c), grid=(rows // ROW_TILE,),
        in_specs=[pl.BlockSpec((ROW_TILE, GM_WIDTH), lambda i: (t0 + i, gu_blk)),
                  pl.BlockSpec((ROW_TILE, GM_WIDTH), lambda i: (t0 + i, gv_blk)),
                  pl.BlockSpec((1, GM_WIDTH), lambda i: (0, 0)),
                  pl.BlockSpec((1, GM_WIDTH), lambda i: (0, 0)),
                  pl.BlockSpec((GM_GROUPS, c, c), lambda i: (0, 0, 0)),
                  pl.BlockSpec((c, GM_GROUPS), lambda i: (0, 0))],
        out_specs=[pl.BlockSpec((ROW_TILE, GM_WIDTH), lambda i: (i, 0)),
                   pl.BlockSpec((ROW_TILE, GM_WIDTH), lambda i: (i, 0))],
        out_shape=[jax.ShapeDtypeStruct((rows, GM_WIDTH), BF16), jax.ShapeDtypeStruct((rows, GM_WIDTH), F32)],
        compiler_params=_cp("parallel"), name="gmlp")(
            hw, hw, ln_g.reshape(1, -1), ln_b.reshape(1, -1), w, bias)


def _rope_tables(pos, rot, period):
    half = rot // 2
    inv = ROPE_THETA ** (-jnp.arange(half, dtype=F32) / half)
    ang = pos.astype(F32)[:, None] * inv[None, :]
    cos, sin = jnp.cos(ang), jnp.sin(ang)
    T = pos.shape[0]
    pad = jnp.zeros((T, period - rot), F32)
    c = jnp.concatenate([cos, cos, pad + 1.0], axis=1)
    s1 = jnp.concatenate([-sin, jnp.zeros_like(sin), pad], axis=1)
    s2 = jnp.concatenate([jnp.zeros_like(sin), sin, pad], axis=1)
    rep = LANE // period
    return jnp.stack([jnp.tile(c, (1, rep)), jnp.tile(s1, (1, rep)), jnp.tile(s2, (1, rep))], axis=0)


def _rope(x, tab_ref, half):
    w = x.shape[1]
    rep = w // LANE

    def wide(t):
        return t if rep == 1 else jnp.concatenate([t] * rep, axis=1)

    xl = pltpu.roll(x, w - half, axis=1)
    xr = pltpu.roll(x, half, axis=1)
    return x * wide(tab_ref[0]) + xl * wide(tab_ref[1]) + xr * wide(tab_ref[2])


_STATE_WIDTHS = (MLA_KV_LORA, MLA_ROPE, 512, 512, IDX_DIM, 512, 512)
N_STATE = len(_STATE_WIDTHS)
STATE_TILE = 256


def _state_kernel(*refs, n_p_tiles):
    (cq_ref, ckv_ref, kr_ref, dq_ref, dk_ref, dv_ref, qi_ref, ki_ref, fq_ref, fk_ref, fv_ref,
     qg_ref, kvg_ref, wqn_ref, wqp_ref, t64_ref, t32_ref, tidx_ref) = refs[:18]
    outs = refs[18 + 2 * N_STATE:]
    st_p, st_s = outs[:N_STATE], outs[N_STATE:2 * N_STATE]
    (qn_o, qp_o, dsaq_o, qi_o, diffq_o, ckv_b, kpe_b, dsak_b, dsav_b, ki_b, diffk_b, diffv_b) = outs[2 * N_STATE:]
    cqn = _rms(cq_ref[...], qg_ref[...]).astype(BF16)
    qn_o[...] = (jnp.dot(cqn, wqn_ref[...], preferred_element_type=F32) * MLA_QSCALE).astype(BF16)
    qp = jnp.dot(cqn, wqp_ref[...], preferred_element_type=F32)
    qp_o[...] = (_rope(qp, t32_ref, MLA_ROPE // 2) * MLA_QSCALE).astype(BF16)
    dsaq_o[...] = (_rope(dq_ref[...], t64_ref, DSA_ROT // 2) * DSA_QSCALE).astype(BF16)
    qi_o[...] = _rope(qi_ref[...], tidx_ref, IDX_ROT // 2).astype(BF16)
    diffq_o[...] = (_rope(fq_ref[...], t64_ref, DIFF_ROT // 2) * DIFF_QSCALE).astype(BF16)
    new = (_rms(ckv_ref[...], kvg_ref[...]), _rope(kr_ref[...], t32_ref, MLA_ROPE // 2),
           _rope(dk_ref[...], t64_ref, DSA_ROT // 2), dv_ref[...],
           _rope(ki_ref[...], tidx_ref, IDX_ROT // 2), _rope(fk_ref[...], t64_ref, DIFF_ROT // 2), fv_ref[...])
    for val, b_ref in zip(new, (ckv_b, kpe_b, dsak_b, dsav_b, ki_b, diffk_b, diffv_b)):
        b_ref[...] = val.astype(BF16)
    i = pl.program_id(0)

    @pl.when(i < n_p_tiles)
    def _():
        for val, o_ref, w in zip(new, st_p, _STATE_WIDTHS):
            o_ref[...] = val[:, :w]

    @pl.when(i >= n_p_tiles)
    def _():
        for val, o_ref, w in zip(new, st_s, _STATE_WIDTHS):
            o_ref[...] = val[:, :w]


def _state(l, depth, n_p, prev, hw, q_norm_g, kv_norm_g, w_uq_nope, w_uq_pe, t64, t32, tidx):
    T = hw.shape[0]
    rt = STATE_TILE
    n_s = T - n_p
    npt = n_p // rt

    def col(name):
        off, w = _DST[name]
        return pl.BlockSpec((rt, w), lambda i: (i, off // w))

    def full(shape):
        return pl.BlockSpec(shape, lambda i: (0,) * len(shape))

    def tab():
        return pl.BlockSpec((3, rt, LANE), lambda i: (0, i, 0))

    def out(w, dt=BF16):
        return pl.BlockSpec((rt, w), lambda i: (i, 0)), jax.ShapeDtypeStruct((T, w), dt)

    outs = [(pl.BlockSpec((None, rt, w), lambda i: (l, jnp.minimum(i, npt - 1), 0)),
             jax.ShapeDtypeStruct((depth, n_p, w), F32)) for w in _STATE_WIDTHS]
    outs += [(pl.BlockSpec((None, rt, w), lambda i: (l, jnp.maximum(i - npt, 0), 0)),
              jax.ShapeDtypeStruct((depth, n_s, w), F32)) for w in _STATE_WIDTHS]
    outs += [out(512), out(MLA_HEADS * MLA_ROPE), out(512), out(IDX_HEADS * IDX_DIM), out(512)]
    outs += [out(MLA_KV_LORA), out(LANE), out(512), out(512), out(LANE), out(512), out(512)]
    in_specs = [col('cq'), col('ckv'), col('krope'), col('dq'), col('dk'), col('dv'), col('qi'), col('ki'),
                col('fq'), col('fk'), col('fv'),
                full((1, MLA_Q_LORA)), full((1, MLA_KV_LORA)),
                full((MLA_Q_LORA, MLA_HEADS * MLA_NOPE)), full((MLA_Q_LORA, MLA_HEADS * MLA_ROPE)),
                tab(), tab(), tab()]
    args = [hw] * 11 + [q_norm_g.reshape(1, -1), kv_norm_g.reshape(1, -1), w_uq_nope, w_uq_pe, t64, t32, tidx]
    in_specs += [pl.BlockSpec(memory_space=pl.ANY)] * (2 * N_STATE)
    args += list(prev)
    res = pl.pallas_call(
        functools.partial(_state_kernel, n_p_tiles=npt), grid=(T // rt,),
        in_specs=in_specs, out_specs=[o[0] for o in outs], out_shape=[o[1] for o in outs],
        input_output_aliases={18 + k: k for k in range(2 * N_STATE)},
        compiler_params=_cp("arbitrary"), name="mixer_state")(*args)
    return res[:2 * N_STATE], res[2 * N_STATE:2 * N_STATE + 5], res[2 * N_STATE + 5:]


def _block_offset(j):
    return j * KV_BLOCK if isinstance(j, int) else pl.multiple_of(j * KV_BLOCK, KV_BLOCK)


def _softmax_steps(scores, values, carries):
    stats = []
    for s, (m, l, _) in zip(scores, carries):
        m_new = jnp.maximum(m, jnp.max(s, axis=0, keepdims=True))
        alpha = jnp.exp2(m - m_new)
        p = jnp.exp2(s - m_new)
        stats.append((m_new, alpha * l + jnp.sum(p, axis=0, keepdims=True), alpha, p.astype(BF16)))
    pvs = [lax.dot_general(v, st[3], (((0,), (0,)), ((), ())), preferred_element_type=F32)
           for v, st in zip(values, stats)]
    return tuple((st[0], st[1], st[2] * c[2] + pv) for st, c, pv in zip(stats, carries, pvs))


def _softmax_init(dv, tq):
    return jnp.full((1, tq), NEG_INF, F32), jnp.zeros((1, tq), F32), jnp.zeros((dv, tq), F32)


def _attend_blocks(n_blocks, qk, values, carries, mask_last):
    carries = lax.fori_loop(0, n_blocks - 1, lambda j, c: _softmax_steps(qk(j), values(j), c), carries)
    scores = qk(n_blocks - 1)
    if mask_last is not None:
        scores = mask_last(scores)
    return _softmax_steps(scores, values(n_blocks - 1), carries)


def _pair_loop(lo, hi, step, carry):
    n2 = (hi - lo) // 2

    def two(t, c):
        j = lo + 2 * t
        return step(j + 1, step(j, c))

    carry = lax.fori_loop(0, n2, two, carry)
    return lax.fori_loop(lo + 2 * n2, hi, step, carry)


def _flash_kernel(*refs, tq, nkb, causal, s_valid, k_lanes, v_rows, dv, diff_layer):
    if diff_layer is None:
        qt_ref, k_ref, vt_ref, o_ref = refs
    else:
        qt_ref, k_ref, vt_ref, lam_ref, sg_ref, o_ref = refs
    i = pl.program_id(2)
    padded = s_valid < nkb * KV_BLOCK

    hg = len(k_lanes)
    n_blocks = i * tq // KV_BLOCK + 1 if causal else nkb
    qts = [_slot_query(qt_ref, hl, k_lanes.count(k_lanes[hl])) for hl in range(hg)]

    def qk(j):
        off = _block_offset(j)
        return tuple(jnp.dot(k_ref[pl.ds(off, KV_BLOCK), k0:k0 + LANE], qts[hl], preferred_element_type=F32)
                     for hl, k0 in enumerate(k_lanes))

    def values(j):
        return [vt_ref[pl.ds(_block_offset(j), KV_BLOCK), v0:v0 + dv] for v0 in v_rows]

    def mask_last(scores):
        kpos = (n_blocks - 1) * KV_BLOCK + lax.broadcasted_iota(I32, (KV_BLOCK, tq), 0)
        if causal:
            qpos = i * tq + lax.broadcasted_iota(I32, (KV_BLOCK, tq), 1)
            ok = kpos // CHUNK <= qpos // CHUNK
        else:
            ok = kpos < s_valid
        return [jnp.where(ok, s, NEG_INF) for s in scores]

    carries = _attend_blocks(n_blocks, qk, values, tuple(_softmax_init(dv, tq) for _ in range(hg)),
                             mask_last if (causal or padded) else None)
    heads = [acc / l for _, l, acc in carries]

    if diff_layer is None:
        for hl in range(hg):
            o_ref[v_rows[hl]:v_rows[hl] + dv, :] = heads[hl].astype(o_ref.dtype)
    else:
        lam_init = 0.8 - 0.6 * math.exp(-0.3 * diff_layer)
        lp = lam_ref[...]
        lam = (jnp.exp(jnp.sum(lp[0:1] * lp[1:2], axis=-1, keepdims=True))
               - jnp.exp(jnp.sum(lp[2:3] * lp[3:4], axis=-1, keepdims=True)) + lam_init)
        for h2 in range(hg // 2):
            o = heads[2 * h2] - lam * heads[2 * h2 + 1]
            y = o * lax.rsqrt(jnp.mean(o * o, axis=0, keepdims=True) + NORM_EPS) * sg_ref[...]
            o_ref[v_rows[2 * h2]:v_rows[2 * h2] + dv, :] = (y * (1.0 - lam_init)).astype(o_ref.dtype)


def _flash(qt, k, vt, *, s_pad, v_block0, tq, causal, s_valid, dv, k_width, k_lanes, diff=None, name="flash"):
    B, n_slots, _, Tq = qt.shape
    nkb = s_pad // KV_BLOCK
    S_pad = s_pad
    assert not causal or (tq <= KV_BLOCK and KV_BLOCK % tq == 0)
    hg = len(k_lanes)
    share = k_lanes.count(k_lanes[0])
    n_grp = n_slots * share // hg
    v_rows = tuple(hl // share * dv for hl in range(hg))
    in_specs = [pl.BlockSpec((None, hg // share, LANE, tq), lambda b, g, i: (b, g, 0, i)),
                pl.BlockSpec((None, S_pad, k_width), lambda b, g, i: (b, 0, g)),
                pl.BlockSpec((None, S_pad, MXU_WIDTH), lambda b, g, i: (b, 0, v_block0 + g))]
    args = [qt, k, vt]
    diff_layer = None
    if diff is not None:
        lam_params, subln_g, diff_layer = diff
        in_specs += [pl.BlockSpec(lam_params.shape, lambda b, g, i: (0, 0)),
                     pl.BlockSpec((dv, tq), lambda b, g, i: (0, 0))]
        args += [lam_params, jnp.broadcast_to(subln_g[:, None], (dv, tq))]
    kern = functools.partial(_flash_kernel, tq=tq, nkb=nkb, causal=causal, s_valid=s_valid,
                             k_lanes=k_lanes, v_rows=v_rows, dv=dv, diff_layer=diff_layer)
    return pl.pallas_call(
        kern, grid=(B, n_grp, Tq // tq), in_specs=in_specs,
        out_specs=pl.BlockSpec((None, MXU_WIDTH, tq), lambda b, g, i: (b, g, i)),
        out_shape=jax.ShapeDtypeStruct((B, n_grp * MXU_WIDTH, Tq), BF16),
        compiler_params=_cp("parallel", "parallel", "parallel"), name=name)(*args)


def _dsa_kernel(qit_ref, wit_ref, ki_ref, qt_ref, k_ref, vt_ref, o_ref, key_ref,
                *, tq, nkb, causal, s_valid, n_sel):
    i = pl.program_id(1)
    tkb = KV_BLOCK
    padded = s_valid < nkb * tkb
    nvis = jnp.minimum((i * tq + tq + tkb - 1) // tkb, nkb) if causal else nkb
    wit = wit_ref[...] * IDX_SCALE

    def kpos(j):
        return j * tkb + lax.broadcasted_iota(I32, (tkb, tq), 0)

    def score_block(j, masked):
        kib = ki_ref[pl.ds(_block_offset(j), tkb), :]
        rs = [jnp.dot(kib, qit_ref[h], preferred_element_type=F32) for h in range(IDX_HEADS)]
        sc = jnp.zeros((tkb, tq), F32)
        for h in range(IDX_HEADS):
            sc = sc + jnp.maximum(rs[h], 0.0) * wit[h:h + 1, :]
        sc = sc + 0.0
        bits = pltpu.bitcast(sc, I32)
        key = jnp.where(bits < 0, bits ^ 0x7FFFFFFF, bits)
        if masked and causal:
            qpos = i * tq + lax.broadcasted_iota(I32, (tkb, tq), 1)
            key = jnp.where(kpos(j) // CHUNK <= qpos // CHUNK, key, INT_MIN)
        elif masked:
            key = jnp.where(kpos(j) < s_valid, key, INT_MIN)
        key_ref[j] = key

    lax.fori_loop(0, nvis - 1, lambda j, c: (score_block(j, False), c)[1], 0)
    score_block(nvis - 1, causal or padded)

    def count(pred):
        def body(j, cnt):
            hit = jnp.where(pred(key_ref[j], j), 1.0, 0.0)
            part = hit[0:8]
            for r in range(1, tkb // 8):
                part = part + hit[r * 8:(r + 1) * 8]
            return cnt + part
        cnt = _pair_loop(0, nvis, body, jnp.zeros((8, tq), F32))
        return jnp.sum(cnt, axis=0, keepdims=True)

    gmax = lax.fori_loop(0, nvis, lambda j, m: jnp.maximum(m, key_ref[j]), jnp.full((tkb, tq), INT_MIN, I32))
    lo = jnp.min(gmax, axis=0, keepdims=True)
    hi = jnp.max(gmax, axis=0, keepdims=True)
    nbits = jnp.max((32 - lax.clz(lo ^ hi)).astype(F32)).astype(I32)
    low_mask = lax.shift_left(jnp.int32(1), jnp.minimum(nbits, 31)) - 1
    t0 = jnp.where(nbits >= 32, INT_MIN, lo & ~low_mask)

    def bit_step(it, t):
        cand = t + lax.shift_left(jnp.int32(1), nbits - 1 - it)
        cnt = count(lambda k, j: k >= cand)
        return jnp.where(cnt >= n_sel, cand, t)

    thr = lax.fori_loop(0, nbits, bit_step, t0)
    cnt_ge = count(lambda k, j: k >= thr)
    need_tie = jnp.logical_and(cnt_ge > n_sel, thr > INT_MIN)
    any_tie = jnp.max(jnp.where(need_tie, 1.0, 0.0)) > 0.0

    def write_bias(sel_of):
        def wr(j, _):
            key_ref[j] = pltpu.bitcast(jnp.where(sel_of(key_ref[j], j), 0.0, NEG_INF), I32)
            return 0
        lax.fori_loop(0, nvis, wr, 0)

    @pl.when(jnp.logical_not(any_tie))
    def _():
        tsel = jnp.maximum(thr, INT_MIN + 1)
        write_bias(lambda k, j: k >= tsel)

    @pl.when(any_tie)
    def _():
        cnt_gt = count(lambda k, j: k > thr)

        def idx_step(it, x):
            cand = x + lax.shift_left(jnp.int32(1), 14 - it)
            g = cnt_gt + count(lambda k, j: jnp.logical_and(k == thr, kpos(j) < cand))
            return jnp.where(g < n_sel, cand, x)

        xj = lax.fori_loop(0, 15, idx_step, jnp.zeros((1, tq), I32))
        jmax = jnp.where(need_tie, xj, jnp.where(thr == INT_MIN, -1, 2 ** 30))
        write_bias(lambda k, j: jnp.logical_or(k > thr, jnp.logical_and(k == thr, kpos(j) <= jmax)))

    qts = [_slot_query(qt_ref, hl, 2) for hl in range(DSA_HEADS)]

    def qk(j):
        off = _block_offset(j)
        bias = pltpu.bitcast(key_ref[j], F32)
        return tuple(jnp.dot(k_ref[pl.ds(off, tkb), hl // 2 * LANE:(hl // 2 + 1) * LANE], qts[hl],
                             preferred_element_type=F32) + bias for hl in range(DSA_HEADS))

    def values(j):
        off = _block_offset(j)
        return [vt_ref[pl.ds(off, tkb), hl * DSA_HEAD_DIM:(hl + 1) * DSA_HEAD_DIM] for hl in range(DSA_HEADS)]

    carries = _attend_blocks(nvis, qk, values,
                             tuple(_softmax_init(DSA_HEAD_DIM, tq) for _ in range(DSA_HEADS)), None)
    for hl, (_, l, acc) in enumerate(carries):
        o_ref[hl * DSA_HEAD_DIM:(hl + 1) * DSA_HEAD_DIM, :] = (acc / l).astype(o_ref.dtype)


def _dsa(qit, wit, ki, qt, k, vt, *, s_pad, tq, causal, s_valid, n_sel, name="dsa"):
    B, H, _, Tq = qt.shape
    nkb = s_pad // KV_BLOCK
    S_pad = s_pad
    assert not causal or (tq <= KV_BLOCK and KV_BLOCK % tq == 0)
    assert KV_BLOCK >= n_sel
    W = DSA_HEADS * DSA_HEAD_DIM
    kern = functools.partial(_dsa_kernel, tq=tq, nkb=nkb, causal=causal, s_valid=s_valid, n_sel=n_sel)
    once = pl.Buffered(1)
    return pl.pallas_call(
        kern, grid=(B, Tq // tq),
        in_specs=[pl.BlockSpec((None, IDX_HEADS, LANE, tq), lambda b, i: (b, 0, 0, i)),
                  pl.BlockSpec((None, IDX_HEADS, tq), lambda b, i: (b, 0, i)),
                  pl.BlockSpec((None, S_pad, LANE), lambda b, i: (b, 0, 0), pipeline_mode=once),
                  pl.BlockSpec((None, H, LANE, tq), lambda b, i: (b, 0, 0, i)),
                  pl.BlockSpec((None, S_pad, W), lambda b, i: (b, 0, 0), pipeline_mode=once),
                  pl.BlockSpec((None, S_pad, W), lambda b, i: (b, 0, 0), pipeline_mode=once)],
        out_specs=pl.BlockSpec((None, W, tq), lambda b, i: (b, 0, i)),
        out_shape=jax.ShapeDtypeStruct((B, W, Tq), BF16),
        scratch_shapes=[pltpu.VMEM((nkb, KV_BLOCK, tq), I32)],
        compiler_params=_cp("parallel", "parallel"), name=name)(qit, wit, ki, qt, k, vt)


def _merge_kernel(h_ref, oa_ref, ob_ref, oc_ref, od_ref, wg0, wg1, wg2, wg3, bg0, bg1, bg2, bg3, wb_ref, o_ref,
                  wgb0, wgb1, wgb2, wgb3, wbb_ref):
    _stage_weights([wg0, wg1, wg2, wg3, wb_ref], [wgb0, wgb1, wgb2, wgb3, wbb_ref])
    h = h_ref[...]
    acc = None
    for n, (o_r, wg, bg) in enumerate(((oa_ref, wgb0, bg0), (ob_ref, wgb1, bg1), (oc_ref, wgb2, bg2),
                                       (od_ref, wgb3, bg3))):
        gate = jax.nn.sigmoid(jnp.dot(h, wg[...], preferred_element_type=F32) + bg[...])
        term = gate * jnp.dot(o_r[...], wbb_ref[n], preferred_element_type=F32)
        acc = term if acc is None else acc + term
    o_ref[...] = acc.astype(BF16)


def _merge(h, branches, w_gate, b_gate, w_branch, tn=512):
    T, D = h.shape
    nd = D // tn
    b_gate = b_gate.reshape(1, -1)
    once = pl.Buffered(1)
    specs = [pl.BlockSpec((ROW_TILE, D), lambda j, i: (i, 0))]
    specs += [pl.BlockSpec((ROW_TILE, BRANCH_WIDTH), lambda j, i: (i, 0))] * N_BRANCH
    specs += [pl.BlockSpec((D, tn), lambda j, i, n=n: (0, n * nd + j), pipeline_mode=once) for n in range(N_BRANCH)]
    specs += [pl.BlockSpec((1, tn), lambda j, i, n=n: (0, n * nd + j)) for n in range(N_BRANCH)]
    specs += [pl.BlockSpec((N_BRANCH, BRANCH_WIDTH, tn), lambda j, i: (0, 0, j), pipeline_mode=once)]
    return pl.pallas_call(
        _merge_kernel, grid=(nd, T // ROW_TILE), in_specs=specs,
        out_specs=pl.BlockSpec((ROW_TILE, tn), lambda j, i: (i, j)),
        out_shape=jax.ShapeDtypeStruct((T, D), BF16),
        scratch_shapes=[pltpu.VMEM((D, tn), BF16)] * N_BRANCH + [pltpu.VMEM((N_BRANCH, BRANCH_WIDTH, tn), BF16)],
        compiler_params=_cp("arbitrary", "arbitrary"), name="merge")(
            h, *branches, w_gate, w_gate, w_gate, w_gate, b_gate, b_gate, b_gate, b_gate, w_branch)


def _ffn_up_kernel(h_ref, w1_ref, w3_ref, o_ref, w1b_ref, w3b_ref):
    _stage_weights([w1_ref, w3_ref], [w1b_ref, w3b_ref])
    h = h_ref[...]
    a = jnp.dot(h, w1b_ref[...], preferred_element_type=F32)
    b = jnp.dot(h, w3b_ref[...], preferred_element_type=F32)
    o_ref[...] = (a * jax.nn.sigmoid(a) * b).astype(BF16)


def _moe_up_kernel(h_ref, w1_ref, w3_ref, gate_ref, o_ref, w1b_ref, w3b_ref):
    _stage_weights([w1_ref, w3_ref], [w1b_ref, w3b_ref])
    e = pl.program_id(0)
    h = h_ref[...]
    a = jnp.dot(h, w1b_ref[...], preferred_element_type=F32)
    b = jnp.dot(h, w3b_ref[...], preferred_element_type=F32)
    gate = gate_ref[...]
    lane = lax.broadcasted_iota(I32, gate.shape, 1)
    ge = jnp.sum(jnp.where(lane == e, gate, 0.0), axis=-1, keepdims=True)
    o_ref[...] = (a * jax.nn.sigmoid(a) * b * ge).astype(BF16)


def _ffn_up(h, w1, w3, tn=512):
    T, D = h.shape
    F = w1.shape[1]
    return pl.pallas_call(
        _ffn_up_kernel, grid=(F // tn, T // ROW_TILE),
        in_specs=[pl.BlockSpec((ROW_TILE, D), lambda j, i: (i, 0)),
                  pl.BlockSpec((D, tn), lambda j, i: (0, j)), pl.BlockSpec((D, tn), lambda j, i: (0, j))],
        out_specs=pl.BlockSpec((ROW_TILE, tn), lambda j, i: (i, j)),
        out_shape=jax.ShapeDtypeStruct((T, F), BF16),
        scratch_shapes=[pltpu.VMEM((D, tn), BF16)] * 2,
        compiler_params=_cp("arbitrary", "arbitrary"), name="ffn_up")(h, w1, w3)


def _moe_up(h, w1, w3, gate):
    T, D = h.shape
    E, _, F = w1.shape
    once = pl.Buffered(1)
    return pl.pallas_call(
        _moe_up_kernel, grid=(E, T // ROW_TILE),
        in_specs=[pl.BlockSpec((ROW_TILE, D), lambda e, i: (i, 0)),
                  pl.BlockSpec((None, D, F), lambda e, i: (e, 0, 0), pipeline_mode=once),
                  pl.BlockSpec((None, D, F), lambda e, i: (e, 0, 0), pipeline_mode=once),
                  pl.BlockSpec((ROW_TILE, LANE), lambda e, i: (i, 0))],
        out_specs=pl.BlockSpec((ROW_TILE, F), lambda e, i: (i, e)),
        out_shape=jax.ShapeDtypeStruct((T, E * F), BF16),
        scratch_shapes=[pltpu.VMEM((D, F), BF16)] * 2,
        compiler_params=_cp("arbitrary", "arbitrary"), name="moe_up")(h, w1, w3, gate)


def _final_norm_kernel(x_ref, g_ref, o_ref):
    o_ref[...] = _rms(x_ref[...], g_ref[...])


def _final_norm(x, g):
    T, D = x.shape
    return pl.pallas_call(
        _final_norm_kernel, grid=(T // ROW_TILE,),
        in_specs=[pl.BlockSpec((ROW_TILE, D), lambda i: (i, 0)), pl.BlockSpec((1, D), lambda i: (0, 0))],
        out_specs=pl.BlockSpec((ROW_TILE, D), lambda i: (i, 0)),
        out_shape=jax.ShapeDtypeStruct((T, D), F32),
        compiler_params=_cp("parallel"), name="final_norm")(x, g.reshape(1, D))


def _pad_rows(a, s_pad):
    pad = s_pad - a.shape[1]
    if pad == 0:
        return a
    return jnp.pad(a, ((0, 0), (0, pad)) + ((0, 0),) * (a.ndim - 2))


def _queries_t(q, n_slots, tq_pad):
    B, T, W = q.shape
    d = W // n_slots
    qt = q.reshape(B, T, n_slots, d).transpose(0, 2, 3, 1)
    if d < LANE:
        qt = jnp.pad(qt, ((0, 0), (0, 0), (0, LANE - d), (0, 0)))
    reps = -(-tq_pad // T)
    return jnp.concatenate([qt] * reps, axis=-1)[..., :tq_pad] if reps > 1 else qt


def _slot_query(qt_ref, hl, heads_per_slot):
    if heads_per_slot == 1:
        return qt_ref[hl]
    qp = qt_ref[hl // heads_per_slot]
    d = LANE // heads_per_slot
    r = hl % heads_per_slot
    parts = [qp[s * d:(s + 1) * d] if s == r else jnp.zeros((d, qp.shape[1]), qp.dtype)
             for s in range(heads_per_slot)]
    return jnp.concatenate(parts, axis=0)


def _rows_out(ot, T):
    B, W, _ = ot.shape
    return ot[:, :, :T].transpose(0, 2, 1).reshape(B * T, W)


def _pad_in_weight(w_in):
    D = w_in.shape[0]
    out = jnp.zeros((D, IN_PAD), BF16)
    for name in _ORDER:
        s0, sw = _SRC[name]
        d0, _ = _DST[name]
        out = lax.dynamic_update_slice(out, w_in[:, s0:s0 + sw].astype(BF16), (0, d0))
    return out


def _col(hw, name, width=None):
    off, w = _DST[name]
    return hw[:, off:off + (width or w)]


def _mixers(l, hw, q16, k16, caches, lp, geom):
    n_p, B_p, T_p, B_s, T_s, past = geom
    q_nope, q_pe, dsa_q, qi, diff_q = q16
    ckv_n, kpe, dsa_k, dsa_v, ki, diff_k, diff_v = k16
    wi = _col(hw, 'wi')
    T_all = hw.shape[0]

    def split(a):
        W = a.shape[1]
        return a[:n_p].reshape(B_p, T_p, W), a[n_p:].reshape(B_s, T_s, W)

    o_a_p, v_p = _gmlp(hw, 0, n_p, min(T_p, GM_CHUNK), lp['gm_ln_g'], lp['gm_ln_b'], lp['gm_ws'], lp['gm_bs'])
    o_a_s, v_s = _gmlp(hw, n_p, hw.shape[0] - n_p, min(T_s, GM_CHUNK), lp['gm_ln_g'], lp['gm_ln_b'],
                       lp['gm_ws'], lp['gm_bs'])
    c_p, c_s = min(T_p, GM_CHUNK), min(T_s, GM_CHUNK)
    gm_v_p = v_p.reshape(B_p, T_p, GM_WIDTH)[:, T_p - c_p:]
    gm_v_s = v_s.reshape(B_s, T_s, GM_WIDTH)[:, T_s - c_s:]

    outs = {}
    for grp in ('p', 's'):
        pi = 0 if grp == 'p' else 1
        B, T = (B_p, T_p) if grp == 'p' else (B_s, T_s)

        def g(a, pi=pi):
            return split(a)[pi]

        if grp == 'p':
            S = s_pad = T
            tq = min(Q_TILE, T)
            tq_pad = T
            causal = True
            direct = B == 1

            def ext(new16, cache, direct=direct):
                return new16.reshape(1, T_all, -1) if direct else g(new16)
        else:
            S = past + T
            s_pad = -(-S // KV_BLOCK) * KV_BLOCK
            tq = tq_pad = -(-T // LANE) * LANE
            causal = False

            def ext(new16, cache, s_pad=s_pad):
                old = cache.reshape(cache.shape[:2] + (-1,)).astype(BF16)
                new = g(new16)[..., :old.shape[-1]]
                return _pad_rows(jnp.concatenate([old, new], axis=1), s_pad)

        cm_ckv, cm_kr, cd_k, cd_v, cd_ki, cf_k, cf_v = caches

        all_c = ext(ckv_n, cm_ckv)
        all_pe = ext(kpe, cm_kr)
        if all_pe.shape[-1] < LANE:
            all_pe = jnp.pad(all_pe, ((0, 0), (0, 0), (0, LANE - all_pe.shape[-1])))
        rows = all_c.shape[1]
        k_mla, v_mla = _mla_kv(all_c.reshape(B * rows, MLA_KV_LORA), all_pe.reshape(B * rows, LANE),
                               lp['w_uk_slots'], lp['w_uv'])
        q_mla = jnp.concatenate([g(q_nope).reshape(B, T, MLA_HEADS, MLA_NOPE),
                                 g(q_pe).reshape(B, T, MLA_HEADS, MLA_ROPE)], axis=-1)
        o_b = _flash(_queries_t(q_mla.reshape(B, T, -1), MLA_HEADS, tq_pad),
                     k_mla.reshape(B, rows, MLA_HEADS * LANE), v_mla.reshape(B, rows, MLA_HEADS * MLA_V),
                     s_pad=s_pad, v_block0=0,
                     tq=tq, causal=causal, s_valid=S, dv=MLA_V, k_width=HEAD_GROUP * LANE,
                     k_lanes=tuple(h * LANE for h in range(HEAD_GROUP)), name="mla_attn")

        ki_all = ext(ki, cd_ki)
        if ki_all.shape[-1] < LANE:
            ki_all = jnp.pad(ki_all, ((0, 0), (0, 0), (0, LANE - ki_all.shape[-1])))
        wit = _queries_t(g(wi)[..., :IDX_HEADS], IDX_HEADS, tq_pad)[:, :, 0, :]
        o_c = _dsa(_queries_t(g(qi), IDX_HEADS, tq_pad), wit, ki_all,
                   _queries_t(g(dsa_q), DSA_HEADS // 2, tq_pad), ext(dsa_k, cd_k), ext(dsa_v, cd_v),
                   s_pad=s_pad, tq=tq, causal=causal, s_valid=S, n_sel=min(DSA_TOPK, S // 4))

        o_d = _flash(_queries_t(g(diff_q), DIFF_HEADS, tq_pad), ext(diff_k, cf_k), ext(diff_v, cf_v),
                     s_pad=s_pad, v_block0=0,
                     tq=tq, causal=causal, s_valid=S, dv=DIFF_V_DIM, k_width=MXU_WIDTH,
                     k_lanes=(0, 0, LANE, LANE),
                     diff=(lp['diff_lambda'], lp['diff_subln_g'], l), name="diff_attn")
        outs[grp] = (_rows_out(o_b, T), _rows_out(o_c, T), _rows_out(o_d, T))

    o_a = jnp.concatenate([o_a_p, o_a_s], axis=0)
    o_b, o_c, o_d = (jnp.concatenate([outs['p'][k], outs['s'][k]], axis=0) for k in range(3))
    return (o_a, o_b, o_c, o_d), gm_v_p, gm_v_s


def kernel(x_prompt, x_sample, c_prompt, c_sample, cache_mla_ckv, cache_mla_krope, cache_dsa_k, cache_dsa_v, cache_dsa_kidx, cache_diff_k, cache_diff_v, ada_w, ada_b, norm_mix_g, norm_ff_g, w_in, w_gate, b_gate, gm_ln_g, gm_ln_b, gm_ws, gm_bs, mla_q_norm_g, mla_kv_norm_g, mla_w_uq, mla_w_ukv, diff_lambda, diff_subln_g, w_branch, w_out, ffn_w1, ffn_w3, ffn_w2, moe_router_w, moe_router_b, moe_w1, moe_w3, moe_w2, final_norm_g):
    B_p, T_p, D = x_prompt.shape
    B_s, T_s, _ = x_sample.shape
    depth = ada_w.shape[0]
    past = cache_mla_ckv.shape[2]
    n_p, n_s = B_p * T_p, B_s * T_s
    assert T_p % ROW_TILE == 0 and n_s % ROW_TILE == 0 and ROW_TILE % T_s == 0
    assert past % CHUNK == 0 and T_s <= CHUNK
    n_p_tiles, tiles_per_b = n_p // ROW_TILE, T_p // ROW_TILE
    geom = (n_p, B_p, T_p, B_s, T_s, past)

    def mod_map(i):
        return jnp.where(i < n_p_tiles, i // tiles_per_b, B_p + i - n_p_tiles)

    x = jnp.concatenate([x_prompt.reshape(n_p, D), x_sample.reshape(n_s, D)], axis=0)

    n_c = B_p + B_s
    c_rows = jnp.concatenate([c_prompt, c_sample, jnp.zeros((-n_c % 8, D), F32)], axis=0)
    mod = _ada(c_rows, ada_w, ada_b)
    mod = mod[:, :n_c].reshape(depth, n_c, 6, D).transpose(0, 2, 1, 3)
    mod_p = jnp.broadcast_to(mod[:, :, :B_p, None, :], (depth, 6, B_p, ROW_TILE, D))
    mod_s = jnp.broadcast_to(mod[:, :, B_p:, None, :], (depth, 6, B_s, T_s, D)).reshape(
        depth, 6, n_s // ROW_TILE, ROW_TILE, D)
    modexp = jnp.concatenate([mod_p, mod_s], axis=2)

    pos = jnp.concatenate([jnp.tile(jnp.arange(T_p, dtype=I32), B_p),
                           jnp.tile(past + jnp.arange(T_s, dtype=I32), B_s)])
    t64 = _rope_tables(pos, DSA_ROT, DSA_HEAD_DIM)
    t32 = _rope_tables(pos, MLA_ROPE, MLA_ROPE)
    tidx = _rope_tables(pos, IDX_ROT, IDX_DIM)

    state = ([jnp.zeros((depth, n_p, w), F32) for w in _STATE_WIDTHS]
             + [jnp.zeros((depth, n_s, w), F32) for w in _STATE_WIDTHS])
    gm_p, gm_s = [], []
    for l in range(depth):
        uq = mla_w_uq[l].reshape(MLA_Q_LORA, MLA_HEADS, MLA_NOPE + MLA_ROPE)
        ukv = mla_w_ukv[l].reshape(MLA_KV_LORA, MLA_HEADS, MLA_NOPE + MLA_V)
        lp = {
            'gm_ln_g': gm_ln_g[l], 'gm_ln_b': gm_ln_b[l], 'gm_ws': gm_ws[l], 'gm_bs': gm_bs[l],
            'w_uk_slots': jnp.pad(ukv[..., :MLA_NOPE], ((0, 0), (0, 0), (0, LANE - MLA_NOPE))).reshape(
                MLA_KV_LORA, MLA_HEADS * LANE).astype(BF16),
            'w_uv': ukv[..., MLA_NOPE:].reshape(MLA_KV_LORA, -1).astype(BF16),
            'diff_lambda': diff_lambda[l], 'diff_subln_g': diff_subln_g[l],
        }
        h = _norm_mod(x, norm_mix_g[l], modexp[l], mod_map, 1, 0)
        hw = _mm(h, _pad_in_weight(w_in[l]), F32, name="in_proj")
        state, q16, k16 = _state(l, depth, n_p, state, hw, mla_q_norm_g[l], mla_kv_norm_g[l],
                                 uq[..., :MLA_NOPE].reshape(MLA_Q_LORA, -1).astype(BF16),
                                 uq[..., MLA_NOPE:].reshape(MLA_Q_LORA, -1).astype(BF16), t64, t32, tidx)
        caches = (cache_mla_ckv[l], cache_mla_krope[l], cache_dsa_k[l], cache_dsa_v[l], cache_dsa_kidx[l],
                  cache_diff_k[l], cache_diff_v[l])
        branches, gm_v_p, gm_v_s = _mixers(l, hw, q16, k16, caches, lp, geom)
        gm_p.append(gm_v_p)
        gm_s.append(gm_v_s)
        merged = _merge(h, branches, w_gate[l], b_gate[l], w_branch[l])
        x = _mm_res(merged, w_out[l], x, modexp[l], mod_map, 2, tn=512, name="out_proj")

        j = l // 2
        if l % 2 == 0:
            h2 = _norm_mod(x, norm_ff_g[l], modexp[l], mod_map, 4, 3)
            act = _ffn_up(h2, ffn_w1[j], ffn_w3[j])
            w2 = ffn_w2[j]
        else:
            rw = jnp.zeros((D, LANE), F32).at[:, :N_EXPERTS].set(moe_router_w[j])
            rw_hi = rw.astype(BF16)
            rw_lo = (rw - rw_hi.astype(F32)).astype(BF16)
            rb = jnp.zeros((1, LANE), F32).at[0, :N_EXPERTS].set(moe_router_b[j])
            h2, gate = _norm_mod(x, norm_ff_g[l], modexp[l], mod_map, 4, 3, router=(rw_hi, rw_lo, rb))
            act = _moe_up(h2, moe_w1[j], moe_w3[j], gate)
            w2 = moe_w2[j].astype(BF16).reshape(-1, D)
        x = _mm_res(act, w2, x, modexp[l], mod_map, 5, tn=512 if w2.dtype == F32 else MXU_WIDTH, name="ffn_down")

    y = _final_norm(x, final_norm_g)
    outs = [y[:n_p].reshape(B_p, T_p, D), y[n_p:].reshape(B_s, T_s, D),
            jnp.stack(gm_p, axis=0), jnp.stack(gm_s, axis=0)]
    tails = ((), (), (DSA_HEADS, DSA_HEAD_DIM), (DSA_HEADS, DSA_HEAD_DIM), (), (DIFF_HEADS, 2, DIFF_QK_DIM),
             (DIFF_HEADS, DIFF_V_DIM))
    for k, tail in enumerate(tails):
        w = _STATE_WIDTHS[k]
        outs.append(state[k].reshape((depth, B_p, T_p) + (tail or (w,))))
        outs.append(state[N_STATE + k].reshape((depth, B_s, T_s) + (tail or (w,))))
    return tuple(outs)
```

```python
import functools
import math

import jax
import jax.numpy as jnp
from jax import lax
from jax.experimental import pallas as pl
from jax.experimental.pallas import tpu as pltpu

F32 = jnp.float32
BF16 = jnp.bfloat16
I32 = jnp.int32

CHUNK = 64
ROPE_THETA = 500000.0
NORM_EPS = 1e-6
NEG_INF = -1e30
N_BRANCH = 4
BRANCH_WIDTH = 512
GM_CHUNK = 128
GM_GROUPS = 4
GM_WIDTH = 512
GM_GROUP_DIM = GM_WIDTH // GM_GROUPS
MLA_HEADS = 8
MLA_Q_LORA = 512
MLA_KV_LORA = 256
MLA_NOPE = 64
MLA_ROPE = 32
MLA_V = 64
DSA_HEADS = 8
DSA_HEAD_DIM = 64
DSA_ROT = DSA_HEAD_DIM // 4
IDX_HEADS = 8
IDX_DIM = 32
IDX_ROT = IDX_DIM // 4
IDX_SCALE = (IDX_DIM ** -0.5) * (IDX_HEADS ** -0.5)
DSA_TOPK = 256
DIFF_HEADS = 4
DIFF_QK_DIM = 64
DIFF_V_DIM = 128
DIFF_ROT = DIFF_QK_DIM // 4
N_EXPERTS = 8

LANE = 128
MXU_WIDTH = 256
VMEM_LIMIT_BYTES = 56 * 1024 * 1024

ROW_TILE = 512
LOG2E = 1.4426950408889634
INT_MIN = -2 ** 31
MLA_QSCALE = (MLA_NOPE + MLA_ROPE) ** -0.5 * LOG2E
DSA_QSCALE = DSA_HEAD_DIM ** -0.5 * LOG2E
DIFF_QSCALE = DIFF_QK_DIM ** -0.5 * LOG2E
KV_BLOCK = 512
Q_TILE = 512

_SPLIT = (
    ('gu', GM_WIDTH), ('gv', GM_WIDTH), ('cq', MLA_Q_LORA), ('ckv', MLA_KV_LORA), ('krope', MLA_ROPE),
    ('dq', 512), ('dk', 512), ('dv', 512), ('qi', IDX_HEADS * IDX_DIM), ('ki', IDX_DIM), ('wi', IDX_HEADS),
    ('fq', 512), ('fk', 512), ('fv', 512),
)
_ORDER = ('gu', 'gv', 'cq', 'dq', 'dk', 'dv', 'fq', 'fk', 'fv', 'ckv', 'qi', 'krope', 'ki', 'wi')


def _in_layout():
    src, acc = {}, 0
    for name, w in _SPLIT:
        src[name] = (acc, w)
        acc += w
    dst, off = {}, 0
    for name in _ORDER:
        w = src[name][1]
        pw = -(-w // LANE) * LANE
        dst[name] = (off, pw)
        off += pw
    total = -(-off // ROW_TILE) * ROW_TILE
    return src, dst, total


_SRC, _DST, IN_PAD = _in_layout()


def _cp(*sem):
    return pltpu.CompilerParams(dimension_semantics=sem, vmem_limit_bytes=VMEM_LIMIT_BYTES)


def _rms(x, g):
    return x * lax.rsqrt(jnp.mean(x * x, axis=-1, keepdims=True) + NORM_EPS) * g


def _ada_kernel(c_ref, w_ref, b_ref, o_ref):
    c = c_ref[...]
    a = (c * jax.nn.sigmoid(c)).astype(BF16)
    o_ref[...] = jnp.dot(a, w_ref[...].astype(BF16), preferred_element_type=F32) + b_ref[...]


def _ada(c_rows, ada_w, ada_b):
    L, D, N = ada_w.shape
    R = c_rows.shape[0]
    tn = 512
    return pl.pallas_call(
        _ada_kernel,
        grid=(L, N // tn),
        in_specs=[pl.BlockSpec((R, D), lambda l, j: (0, 0)),
                  pl.BlockSpec((None, D, tn), lambda l, j: (l, 0, j)),
                  pl.BlockSpec((None, 1, tn), lambda l, j: (l, 0, j))],
        out_specs=pl.BlockSpec((None, R, tn), lambda l, j: (l, 0, j)),
        out_shape=jax.ShapeDtypeStruct((L, R, N), F32),
        compiler_params=_cp("parallel", "parallel"),
        name="ada_mod",
    )(c_rows, ada_w, ada_b.reshape(L, 1, N))


def _norm_mod_kernel(x_ref, g_ref, sc_ref, sh_ref, h_ref):
    y = _rms(x_ref[...], g_ref[...])
    h_ref[...] = (y * (1.0 + sc_ref[...]) + sh_ref[...]).astype(BF16)


def _norm_mod_router_kernel(x_ref, g_ref, sc_ref, sh_ref, rwh_ref, rwl_ref, rb_ref, h_ref, gate_ref):
    y = _rms(x_ref[...], g_ref[...])
    h = y * (1.0 + sc_ref[...]) + sh_ref[...]
    h_hi = h.astype(BF16)
    h_ref[...] = h_hi
    h_lo = (h - h_hi.astype(F32)).astype(BF16)
    logits = (jnp.dot(h_hi, rwh_ref[...], preferred_element_type=F32)
              + jnp.dot(h_lo, rwh_ref[...], preferred_element_type=F32)
              + jnp.dot(h_hi, rwl_ref[...], preferred_element_type=F32)) + rb_ref[...]
    lane = lax.broadcasted_iota(I32, logits.shape, 1)
    lg = jnp.where(lane < N_EXPERTS, logits, -jnp.inf)
    m1 = jnp.max(lg, axis=-1, keepdims=True)
    i1 = jnp.min(jnp.where(lg == m1, lane, LANE), axis=-1, keepdims=True)
    lg2 = jnp.where(lane == i1, -jnp.inf, lg)
    m2 = jnp.max(lg2, axis=-1, keepdims=True)
    i2 = jnp.min(jnp.where(lg2 == m2, lane, LANE), axis=-1, keepdims=True)
    e2 = jnp.exp(m2 - m1)
    w1 = 1.0 / (1.0 + e2)
    gate_ref[...] = jnp.where(lane == i1, w1, 0.0) + jnp.where(lane == i2, e2 * w1, 0.0)


def _mod_spec(mod_map, k, D):
    return pl.BlockSpec((None, None, ROW_TILE, D), lambda i, *_: (k, mod_map(i), 0, 0))


def _norm_mod(x, g, modexp, mod_map, k_scale, k_shift, router=None):
    T, D = x.shape
    specs = [pl.BlockSpec((ROW_TILE, D), lambda i: (i, 0)),
             pl.BlockSpec((1, D), lambda i: (0, 0)),
             _mod_spec(mod_map, k_scale, D), _mod_spec(mod_map, k_shift, D)]
    args = [x, g.reshape(1, D), modexp, modexp]
    h_spec = pl.BlockSpec((ROW_TILE, D), lambda i: (i, 0))
    h_shape = jax.ShapeDtypeStruct((T, D), BF16)
    if router is None:
        return pl.pallas_call(
            _norm_mod_kernel, grid=(T // ROW_TILE,), in_specs=specs, out_specs=h_spec, out_shape=h_shape,
            compiler_params=_cp("parallel"), name="norm_mod")(*args)
    rw_hi, rw_lo, rb = router
    specs += [pl.BlockSpec((D, LANE), lambda i: (0, 0)), pl.BlockSpec((D, LANE), lambda i: (0, 0)),
              pl.BlockSpec((1, LANE), lambda i: (0, 0))]
    return pl.pallas_call(
        _norm_mod_router_kernel, grid=(T // ROW_TILE,), in_specs=specs,
        out_specs=[h_spec, pl.BlockSpec((ROW_TILE, LANE), lambda i: (i, 0))],
        out_shape=[h_shape, jax.ShapeDtypeStruct((T, LANE), F32)],
        compiler_params=_cp("parallel"), name="norm_mod_router")(*args, rw_hi, rw_lo, rb)


def _mm_kernel(a_ref, b_ref, o_ref):
    o_ref[...] = jnp.dot(a_ref[...], b_ref[...], preferred_element_type=F32).astype(o_ref.dtype)


def _mm(a, b, out_dtype, tm=ROW_TILE, tn=512, name="mm"):
    M, K = a.shape
    N = b.shape[1]
    return pl.pallas_call(
        _mm_kernel, grid=(M // tm, N // tn),
        in_specs=[pl.BlockSpec((tm, K), lambda i, j: (i, 0)), pl.BlockSpec((K, tn), lambda i, j: (0, j))],
        out_specs=pl.BlockSpec((tm, tn), lambda i, j: (i, j)),
        out_shape=jax.ShapeDtypeStruct((M, N), out_dtype),
        compiler_params=_cp("parallel", "parallel"), name=name)(a, b)


def _mla_kv_kernel(c_ref, pe_ref, wk_ref, wv_ref, k_ref, v_ref):
    c = c_ref[...]
    slot = pltpu.roll(pe_ref[...].astype(F32), MLA_NOPE, axis=1)
    k = jnp.dot(c, wk_ref[...], preferred_element_type=F32) + jnp.concatenate([slot] * MLA_HEADS, axis=1)
    k_ref[...] = k.astype(BF16)
    v_ref[...] = jnp.dot(c, wv_ref[...], preferred_element_type=F32).astype(BF16)


def _mla_kv(c, pe, w_uk_slots, w_uv):
    M, K = c.shape
    nk, nv = w_uk_slots.shape[1], w_uv.shape[1]
    tm = ROW_TILE
    return pl.pallas_call(
        _mla_kv_kernel, grid=(M // tm,),
        in_specs=[pl.BlockSpec((tm, K), lambda i: (i, 0)), pl.BlockSpec((tm, LANE), lambda i: (i, 0)),
                  pl.BlockSpec((K, nk), lambda i: (0, 0)), pl.BlockSpec((K, nv), lambda i: (0, 0))],
        out_specs=[pl.BlockSpec((tm, nk), lambda i: (i, 0)), pl.BlockSpec((tm, nv), lambda i: (i, 0))],
        out_shape=[jax.ShapeDtypeStruct((M, nk), BF16), jax.ShapeDtypeStruct((M, nv), BF16)],
        compiler_params=_cp("parallel"), name="mla_kv")(c, pe, w_uk_slots, w_uv)


def _stage_weights(w_refs, wb_refs):
    @pl.when(pl.program_id(1) == 0)
    def _():
        for w_ref, wb_ref in zip(w_refs, wb_refs):
            wb_ref[...] = w_ref[...].astype(BF16)


def _mm_res_kernel(a_ref, b_ref, x_ref, g_ref, o_ref):
    o_ref[...] = x_ref[...] + g_ref[...] * jnp.dot(a_ref[...], b_ref[...], preferred_element_type=F32)


def _mm_res_ws_kernel(a_ref, w_ref, x_ref, g_ref, o_ref, wb_ref):
    _stage_weights([w_ref], [wb_ref])
    o_ref[...] = x_ref[...] + g_ref[...] * jnp.dot(a_ref[...], wb_ref[...], preferred_element_type=F32)


def _mm_res(a, b, x, modexp, mod_map, k_gate, tn, name="mm_res"):
    M, K = a.shape
    N = b.shape[1]
    tm = ROW_TILE
    out_shape = jax.ShapeDtypeStruct((M, N), F32)
    if b.dtype == BF16:
        return pl.pallas_call(
            _mm_res_kernel, grid=(M // tm, N // tn),
            in_specs=[pl.BlockSpec((tm, K), lambda i, j: (i, 0)),
                      pl.BlockSpec((K, tn), lambda i, j: (0, j)),
                      pl.BlockSpec((tm, tn), lambda i, j: (i, j)),
                      pl.BlockSpec((None, None, tm, tn), lambda i, j: (k_gate, mod_map(i), 0, j))],
            out_specs=pl.BlockSpec((tm, tn), lambda i, j: (i, j)), out_shape=out_shape,
            compiler_params=_cp("parallel", "parallel"), name=name)(a, b, x, modexp)
    return pl.pallas_call(
        _mm_res_ws_kernel, grid=(N // tn, M // tm),
        in_specs=[pl.BlockSpec((tm, K), lambda j, i: (i, 0)),
                  pl.BlockSpec((K, tn), lambda j, i: (0, j)),
                  pl.BlockSpec((tm, tn), lambda j, i: (i, j)),
                  pl.BlockSpec((None, None, tm, tn), lambda j, i: (k_gate, mod_map(i), 0, j))],
        out_specs=pl.BlockSpec((tm, tn), lambda j, i: (i, j)), out_shape=out_shape,
        scratch_shapes=[pltpu.VMEM((K, tn), BF16)],
        compiler_params=_cp("arbitrary", "arbitrary"), name=name)(a, b, x, modexp)


def _gmlp_kernel(gu_ref, gv_ref, lng_ref, lnb_ref, w_ref, bias_ref, o_ref, v_ref, *, c):
    u = jax.nn.gelu(gu_ref[...])
    gv = jax.nn.gelu(gv_ref[...])
    mu = jnp.mean(gv, axis=-1, keepdims=True)
    var = jnp.mean(jnp.square(gv - mu), axis=-1, keepdims=True)
    v = (gv - mu) * lax.rsqrt(var + NORM_EPS) * lng_ref[...] + lnb_ref[...]
    v_ref[...] = v
    vb = v.astype(BF16)
    row = lax.broadcasted_iota(I32, (c, c), 0)
    col = lax.broadcasted_iota(I32, (c, c), 1)
    bias = bias_ref[...]
    for g in range(GM_GROUPS):
        w = jnp.where(row >= col, w_ref[g], 0.0).astype(BF16)
        lo, hi = g * GM_GROUP_DIM, (g + 1) * GM_GROUP_DIM
        for n in range(ROW_TILE // c):
            r0, r1 = n * c, (n + 1) * c
            sp = jnp.dot(w, vb[r0:r1, lo:hi], preferred_element_type=F32) + bias[:, g:g + 1]
            o_ref[r0:r1, lo:hi] = (u[r0:r1, lo:hi] * sp).astype(BF16)


def _gmlp(hw, row0, rows, c, ln_g, ln_b, ws, bs):
    t0 = row0 // ROW_TILE
    gu_blk = _DST['gu'][0] // GM_WIDTH
    gv_blk = _DST['gv'][0] // GM_WIDTH
    w = ws[:, :c, :c]
    bias = bs[:, :c].T
    return pl.pallas_call(
        functools.partial(_gmlp_kernel, c=c), grid=(rows // ROW_TILE,),
        in_specs=[pl.BlockSpec((ROW_TILE, GM_WIDTH), lambda i: (t0 + i, gu_blk)),
                  pl.BlockSpec((ROW_TILE, GM_WIDTH), lambda i: (t0 + i, gv_blk)),
                  pl.BlockSpec((1, GM_WIDTH), lambda i: (0, 0)),
                  pl.BlockSpec((1, GM_WIDTH), lambda i: (0, 0)),
                  pl.BlockSpec((GM_GROUPS, c, c), lambda i: (0, 0, 0)),
                  pl.BlockSpec((c, GM_GROUPS), lambda i: (0, 0))],
        out_specs=[pl.BlockSpec((ROW_TILE, GM_WIDTH), lambda i: (i, 0)),
                   pl.BlockSpec((ROW_TILE, GM_WIDTH), lambda i: (i, 0))],
        out_shape=[jax.ShapeDtypeStruct((rows, GM_WIDTH), BF16), jax.ShapeDtypeStruct((rows, GM_WIDTH), F32)],
        compiler_params=_cp("parallel"), name="gmlp")(
            hw, hw, ln_g.reshape(1, -1), ln_b.reshape(1, -1), w, bias)


def _rope_tables(pos, rot, period):
    half = rot // 2
    inv = ROPE_THETA ** (-jnp.arange(half, dtype=F32) / half)
    ang = pos.astype(F32)[:, None] * inv[None, :]
    cos, sin = jnp.cos(ang), jnp.sin(ang)
    T = pos.shape[0]
    pad = jnp.zeros((T, period - rot), F32)
    c = jnp.concatenate([cos, cos, pad + 1.0], axis=1)
    s1 = jnp.concatenate([-sin, jnp.zeros_like(sin), pad], axis=1)
    s2 = jnp.concatenate([jnp.zeros_like(sin), sin, pad], axis=1)
    rep = LANE // period
    return jnp.stack([jnp.tile(c, (1, rep)), jnp.tile(s1, (1, rep)), jnp.tile(s2, (1, rep))], axis=0)


def _rope(x, tab_ref, half):
    w = x.shape[1]
    rep = w // LANE

    def wide(t):
        return t if rep == 1 else jnp.concatenate([t] * rep, axis=1)

    xl = pltpu.roll(x, w - half, axis=1)
    xr = pltpu.roll(x, half, axis=1)
    return x * wide(tab_ref[0]) + xl * wide(tab_ref[1]) + xr * wide(tab_ref[2])


_STATE_WIDTHS = (MLA_KV_LORA, MLA_ROPE, 512, 512, IDX_DIM, 512, 512)
N_STATE = len(_STATE_WIDTHS)
STATE_TILE = 256


def _state_kernel(*refs, n_p_tiles):
    (cq_ref, ckv_ref, kr_ref, dq_ref, dk_ref, dv_ref, qi_ref, ki_ref, fq_ref, fk_ref, fv_ref,
     qg_ref, kvg_ref, wqn_ref, wqp_ref, t64_ref, t32_ref, tidx_ref) = refs[:18]
    outs = refs[18 + 2 * N_STATE:]
    st_p, st_s = outs[:N_STATE], outs[N_STATE:2 * N_STATE]
    (qn_o, qp_o, dsaq_o, qi_o, diffq_o, ckv_b, kpe_b, dsak_b, dsav_b, ki_b, diffk_b, diffv_b) = outs[2 * N_STATE:]
    cqn = _rms(cq_ref[...], qg_ref[...]).astype(BF16)
    qn_o[...] = (jnp.dot(cqn, wqn_ref[...], preferred_element_type=F32) * MLA_QSCALE).astype(BF16)
    qp = jnp.dot(cqn, wqp_ref[...], preferred_element_type=F32)
    qp_o[...] = (_rope(qp, t32_ref, MLA_ROPE // 2) * MLA_QSCALE).astype(BF16)
    dsaq_o[...] = (_rope(dq_ref[...], t64_ref, DSA_ROT // 2) * DSA_QSCALE).astype(BF16)
    qi_o[...] = _rope(qi_ref[...], tidx_ref, IDX_ROT // 2).astype(BF16)
    diffq_o[...] = (_rope(fq_ref[...], t64_ref, DIFF_ROT // 2) * DIFF_QSCALE).astype(BF16)
    new = (_rms(ckv_ref[...], kvg_ref[...]), _rope(kr_ref[...], t32_ref, MLA_ROPE // 2),
           _rope(dk_ref[...], t64_ref, DSA_ROT // 2), dv_ref[...],
           _rope(ki_ref[...], tidx_ref, IDX_ROT // 2), _rope(fk_ref[...], t64_ref, DIFF_ROT // 2), fv_ref[...])
    for val, b_ref in zip(new, (ckv_b, kpe_b, dsak_b, dsav_b, ki_b, diffk_b, diffv_b)):
        b_ref[...] = val.astype(BF16)
    i = pl.program_id(0)

    @pl.when(i < n_p_tiles)
    def _():
        for val, o_ref, w in zip(new, st_p, _STATE_WIDTHS):
            o_ref[...] = val[:, :w]

    @pl.when(i >= n_p_tiles)
    def _():
        for val, o_ref, w in zip(new, st_s, _STATE_WIDTHS):
            o_ref[...] = val[:, :w]


def _state(l, depth, n_p, prev, hw, q_norm_g, kv_norm_g, w_uq_nope, w_uq_pe, t64, t32, tidx):
    T = hw.shape[0]
    rt = STATE_TILE
    n_s = T - n_p
    npt = n_p // rt

    def col(name):
        off, w = _DST[name]
        return pl.BlockSpec((rt, w), lambda i: (i, off // w))

    def full(shape):
        return pl.BlockSpec(shape, lambda i: (0,) * len(shape))

    def tab():
        return pl.BlockSpec((3, rt, LANE), lambda i: (0, i, 0))

    def out(w, dt=BF16):
        return pl.BlockSpec((rt, w), lambda i: (i, 0)), jax.ShapeDtypeStruct((T, w), dt)

    outs = [(pl.BlockSpec((None, rt, w), lambda i: (l, jnp.minimum(i, npt - 1), 0)),
             jax.ShapeDtypeStruct((depth, n_p, w), F32)) for w in _STATE_WIDTHS]
    outs += [(pl.BlockSpec((None, rt, w), lambda i: (l, jnp.maximum(i - npt, 0), 0)),
              jax.ShapeDtypeStruct((depth, n_s, w), F32)) for w in _STATE_WIDTHS]
    outs += [out(512), out(MLA_HEADS * MLA_ROPE), out(512), out(IDX_HEADS * IDX_DIM), out(512)]
    outs += [out(MLA_KV_LORA), out(LANE), out(512), out(512), out(LANE), out(512), out(512)]
    in_specs = [col('cq'), col('ckv'), col('krope'), col('dq'), col('dk'), col('dv'), col('qi'), col('ki'),
                col('fq'), col('fk'), col('fv'),
                full((1, MLA_Q_LORA)), full((1, MLA_KV_LORA)),
                full((MLA_Q_LORA, MLA_HEADS * MLA_NOPE)), full((MLA_Q_LORA, MLA_HEADS * MLA_ROPE)),
                tab(), tab(), tab()]
    args = [hw] * 11 + [q_norm_g.reshape(1, -1), kv_norm_g.reshape(1, -1), w_uq_nope, w_uq_pe, t64, t32, tidx]
    in_specs += [pl.BlockSpec(memory_space=pl.ANY)] * (2 * N_STATE)
    args += list(prev)
    res = pl.pallas_call(
        functools.partial(_state_kernel, n_p_tiles=npt), grid=(T // rt,),
        in_specs=in_specs, out_specs=[o[0] for o in outs], out_shape=[o[1] for o in outs],
        input_output_aliases={18 + k: k for k in range(2 * N_STATE)},
        compiler_params=_cp("arbitrary"), name="mixer_state")(*args)
    return res[:2 * N_STATE], res[2 * N_STATE:2 * N_STATE + 5], res[2 * N_STATE + 5:]


def _block_offset(j):
    return j * KV_BLOCK if isinstance(j, int) else pl.multiple_of(j * KV_BLOCK, KV_BLOCK)


def _softmax_steps(scores, values, carries):
    stats = []
    for s, (m, l, _) in zip(scores, carries):
        m_new = jnp.maximum(m, jnp.max(s, axis=0, keepdims=True))
        alpha = jnp.exp2(m - m_new)
        p = jnp.exp2(s - m_new)
        stats.append((m_new, alpha * l + jnp.sum(p, axis=0, keepdims=True), alpha, p.astype(BF16)))
    pvs = [lax.dot_general(v, st[3], (((0,), (0,)), ((), ())), preferred_element_type=F32)
           for v, st in zip(values, stats)]
    return tuple((st[0], st[1], st[2] * c[2] + pv) for st, c, pv in zip(stats, carries, pvs))


def _softmax_init(dv, tq):
    return jnp.full((1, tq), NEG_INF, F32), jnp.zeros((1, tq), F32), jnp.zeros((dv, tq), F32)


def _attend_blocks(n_blocks, qk, values, carries, mask_last):
    carries = lax.fori_loop(0, n_blocks - 1, lambda j, c: _softmax_steps(qk(j), values(j), c), carries)
    scores = qk(n_blocks - 1)
    if mask_last is not None:
        scores = mask_last(scores)
    return _softmax_steps(scores, values(n_blocks - 1), carries)


def _pair_loop(lo, hi, step, carry):
    n2 = (hi - lo) // 2

    def two(t, c):
        j = lo + 2 * t
        return step(j + 1, step(j, c))

    carry = lax.fori_loop(0, n2, two, carry)
    return lax.fori_loop(lo + 2 * n2, hi, step, carry)


def _flash_kernel(*refs, tq, nkb, causal, s_valid, k_lanes, v_rows, dv, diff_layer):
    if diff_layer is None:
        qt_ref, k_ref, vt_ref, o_ref = refs
    else:
        qt_ref, k_ref, vt_ref, lam_ref, sg_ref, o_ref = refs
    i = pl.program_id(2)
    padded = s_valid < nkb * KV_BLOCK

    hg = len(k_lanes)
    n_blocks = i * tq // KV_BLOCK + 1 if causal else nkb
    qts = [_slot_query(qt_ref, hl, k_lanes.count(k_lanes[hl])) for hl in range(hg)]

    def qk(j):
        off = _block_offset(j)
        return tuple(jnp.dot(k_ref[pl.ds(off, KV_BLOCK), k0:k0 + LANE], qts[hl], preferred_element_type=F32)
                     for hl, k0 in enumerate(k_lanes))

    def values(j):
        return [vt_ref[pl.ds(_block_offset(j), KV_BLOCK), v0:v0 + dv] for v0 in v_rows]

    def mask_last(scores):
        kpos = (n_blocks - 1) * KV_BLOCK + lax.broadcasted_iota(I32, (KV_BLOCK, tq), 0)
        if causal:
            qpos = i * tq + lax.broadcasted_iota(I32, (KV_BLOCK, tq), 1)
            ok = kpos // CHUNK <= qpos // CHUNK
        else:
            ok = kpos < s_valid
        return [jnp.where(ok, s, NEG_INF) for s in scores]

    carries = _attend_blocks(n_blocks, qk, values, tuple(_softmax_init(dv, tq) for _ in range(hg)),
                             mask_last if (causal or padded) else None)
    heads = [acc / l for _, l, acc in carries]

    if diff_layer is None:
        for hl in range(hg):
            o_ref[v_rows[hl]:v_rows[hl] + dv, :] = heads[hl].astype(o_ref.dtype)
    else:
        lam_init = 0.8 - 0.6 * math.exp(-0.3 * diff_layer)
        lp = lam_ref[...]
        lam = (jnp.exp(jnp.sum(lp[0:1] * lp[1:2], axis=-1, keepdims=True))
               - jnp.exp(jnp.sum(lp[2:3] * lp[3:4], axis=-1, keepdims=True)) + lam_init)
        for h2 in range(hg // 2):
            o = heads[2 * h2] - lam * heads[2 * h2 + 1]
            y = o * lax.rsqrt(jnp.mean(o * o, axis=0, keepdims=True) + NORM_EPS) * sg_ref[...]
            o_ref[v_rows[2 * h2]:v_rows[2 * h2] + dv, :] = (y * (1.0 - lam_init)).astype(o_ref.dtype)


def _flash(qt, k, vt, *, s_pad, v_block0, tq, causal, s_valid, dv, k_width, k_lanes, diff=None, name="flash"):
    B, n_slots, _, Tq = qt.shape
    nkb = s_pad // KV_BLOCK
    S_pad = s_pad
    assert not causal or (tq <= KV_BLOCK and KV_BLOCK % tq == 0)
    hg = len(k_lanes)
    share = k_lanes.count(k_lanes[0])
    n_grp = n_slots * share // hg
    v_rows = tuple(hl // share * dv for hl in range(hg))
    v_width = hg // share * dv
    in_specs = [pl.BlockSpec((None, hg // share, LANE, tq), lambda b, g, i: (b, g, 0, i)),
                pl.BlockSpec((None, S_pad, k_width), lambda b, g, i: (b, 0, g)),
                pl.BlockSpec((None, S_pad, v_width), lambda b, g, i: (b, 0, v_block0 + g))]
    args = [qt, k, vt]
    diff_layer = None
    if diff is not None:
        lam_params, subln_g, diff_layer = diff
        in_specs += [pl.BlockSpec(lam_params.shape, lambda b, g, i: (0, 0)),
                     pl.BlockSpec((dv, tq), lambda b, g, i: (0, 0))]
        args += [lam_params, jnp.broadcast_to(subln_g[:, None], (dv, tq))]
    kern = functools.partial(_flash_kernel, tq=tq, nkb=nkb, causal=causal, s_valid=s_valid,
                             k_lanes=k_lanes, v_rows=v_rows, dv=dv, diff_layer=diff_layer)
    return pl.pallas_call(
        kern, grid=(B, n_grp, Tq // tq), in_specs=in_specs,
        out_specs=pl.BlockSpec((None, v_width, tq), lambda b, g, i: (b, g, i)),
        out_shape=jax.ShapeDtypeStruct((B, n_grp * v_width, Tq), BF16),
        compiler_params=_cp("parallel", "parallel", "parallel"), name=name)(*args)


def _dsa_kernel(qit_ref, wit_ref, ki_ref, qt_ref, k_ref, vt_ref, o_ref, key_ref,
                *, tq, nkb, causal, s_valid, n_sel):
    i = pl.program_id(1)
    tkb = KV_BLOCK
    padded = s_valid < nkb * tkb
    nvis = jnp.minimum((i * tq + tq + tkb - 1) // tkb, nkb) if causal else nkb
    wit = wit_ref[...] * IDX_SCALE

    def kpos(j):
        return j * tkb + lax.broadcasted_iota(I32, (tkb, tq), 0)

    def score_block(j, masked):
        kib = ki_ref[pl.ds(_block_offset(j), tkb), :]
        rs = [jnp.dot(kib, qit_ref[h], preferred_element_type=F32) for h in range(IDX_HEADS)]
        sc = jnp.zeros((tkb, tq), F32)
        for h in range(IDX_HEADS):
            sc = sc + jnp.maximum(rs[h], 0.0) * wit[h:h + 1, :]
        sc = sc + 0.0
        bits = pltpu.bitcast(sc, I32)
        key = jnp.where(bits < 0, bits ^ 0x7FFFFFFF, bits)
        if masked and causal:
            qpos = i * tq + lax.broadcasted_iota(I32, (tkb, tq), 1)
            key = jnp.where(kpos(j) // CHUNK <= qpos // CHUNK, key, INT_MIN)
        elif masked:
            key = jnp.where(kpos(j) < s_valid, key, INT_MIN)
        key_ref[j] = key

    lax.fori_loop(0, nvis - 1, lambda j, c: (score_block(j, False), c)[1], 0)
    score_block(nvis - 1, causal or padded)

    def count(pred):
        def body(j, cnt):
            hit = jnp.where(pred(key_ref[j], j), 1.0, 0.0)
            part = hit[0:8]
            for r in range(1, tkb // 8):
                part = part + hit[r * 8:(r + 1) * 8]
            return cnt + part
        cnt = _pair_loop(0, nvis, body, jnp.zeros((8, tq), F32))
        return jnp.sum(cnt, axis=0, keepdims=True)

    gmax = lax.fori_loop(0, nvis, lambda j, m: jnp.maximum(m, key_ref[j]), jnp.full((tkb, tq), INT_MIN, I32))
    hi = jnp.max(gmax, axis=0, keepdims=True)
    while gmax.shape[0] // 2 >= max(n_sel, 8):
        half = gmax.shape[0] // 2
        gmax = jnp.maximum(gmax[:half], gmax[half:])
    lo = jnp.min(gmax, axis=0, keepdims=True)
    nbits = jnp.max((32 - lax.clz(lo ^ hi)).astype(F32)).astype(I32)
    low_mask = lax.shift_left(jnp.int32(1), jnp.minimum(nbits, 31)) - 1
    t0 = jnp.where(nbits >= 32, INT_MIN, lo & ~low_mask)

    def bit_step(it, t):
        cand = t + lax.shift_left(jnp.int32(1), nbits - 1 - it)
        cnt = count(lambda k, j: k >= cand)
        return jnp.where(cnt >= n_sel, cand, t)

    thr = lax.fori_loop(0, nbits, bit_step, t0)
    cnt_ge = count(lambda k, j: k >= thr)
    need_tie = jnp.logical_and(cnt_ge > n_sel, thr > INT_MIN)
    any_tie = jnp.max(jnp.where(need_tie, 1.0, 0.0)) > 0.0

    def write_bias(sel_of):
        def wr(j, _):
            key_ref[j] = pltpu.bitcast(jnp.where(sel_of(key_ref[j], j), 0.0, NEG_INF), I32)
            return 0
        lax.fori_loop(0, nvis, wr, 0)

    @pl.when(jnp.logical_not(any_tie))
    def _():
        tsel = jnp.maximum(thr, INT_MIN + 1)
        write_bias(lambda k, j: k >= tsel)

    @pl.when(any_tie)
    def _():
        cnt_gt = count(lambda k, j: k > thr)

        def idx_step(it, x):
            cand = x + lax.shift_left(jnp.int32(1), 14 - it)
            g = cnt_gt + count(lambda k, j: jnp.logical_and(k == thr, kpos(j) < cand))
            return jnp.where(g < n_sel, cand, x)

        xj = lax.fori_loop(0, 15, idx_step, jnp.zeros((1, tq), I32))
        jmax = jnp.where(need_tie, xj, jnp.where(thr == INT_MIN, -1, 2 ** 30))
        write_bias(lambda k, j: jnp.logical_or(k > thr, jnp.logical_and(k == thr, kpos(j) <= jmax)))

    qts = [_slot_query(qt_ref, hl, 2) for hl in range(DSA_HEADS)]

    def qk(j):
        off = _block_offset(j)
        bias = pltpu.bitcast(key_ref[j], F32)
        return tuple(jnp.dot(k_ref[pl.ds(off, tkb), hl // 2 * LANE:(hl // 2 + 1) * LANE], qts[hl],
                             preferred_element_type=F32) + bias for hl in range(DSA_HEADS))

    def values(j):
        off = _block_offset(j)
        return [vt_ref[pl.ds(off, tkb), hl * DSA_HEAD_DIM:(hl + 1) * DSA_HEAD_DIM] for hl in range(DSA_HEADS)]

    carries = _attend_blocks(nvis, qk, values,
                             tuple(_softmax_init(DSA_HEAD_DIM, tq) for _ in range(DSA_HEADS)), None)
    for hl, (_, l, acc) in enumerate(carries):
        o_ref[hl * DSA_HEAD_DIM:(hl + 1) * DSA_HEAD_DIM, :] = (acc / l).astype(o_ref.dtype)


def _dsa(qit, wit, ki, qt, k, vt, *, s_pad, tq, causal, s_valid, n_sel, name="dsa"):
    B, H, _, Tq = qt.shape
    nkb = s_pad // KV_BLOCK
    S_pad = s_pad
    assert not causal or (tq <= KV_BLOCK and KV_BLOCK % tq == 0)
    assert KV_BLOCK >= n_sel
    W = DSA_HEADS * DSA_HEAD_DIM
    kern = functools.partial(_dsa_kernel, tq=tq, nkb=nkb, causal=causal, s_valid=s_valid, n_sel=n_sel)
    once = pl.Buffered(1)
    return pl.pallas_call(
        kern, grid=(B, Tq // tq),
        in_specs=[pl.BlockSpec((None, IDX_HEADS, LANE, tq), lambda b, i: (b, 0, 0, i)),
                  pl.BlockSpec((None, IDX_HEADS, tq), lambda b, i: (b, 0, i)),
                  pl.BlockSpec((None, S_pad, LANE), lambda b, i: (b, 0, 0), pipeline_mode=once),
                  pl.BlockSpec((None, H, LANE, tq), lambda b, i: (b, 0, 0, i)),
                  pl.BlockSpec((None, S_pad, W), lambda b, i: (b, 0, 0), pipeline_mode=once),
                  pl.BlockSpec((None, S_pad, W), lambda b, i: (b, 0, 0), pipeline_mode=once)],
        out_specs=pl.BlockSpec((None, W, tq), lambda b, i: (b, 0, i)),
        out_shape=jax.ShapeDtypeStruct((B, W, Tq), BF16),
        scratch_shapes=[pltpu.VMEM((nkb, KV_BLOCK, tq), I32)],
        compiler_params=_cp("parallel", "parallel"), name=name)(qit, wit, ki, qt, k, vt)


def _merge_kernel(h_ref, oa_ref, ob_ref, oc_ref, od_ref, wg0, wg1, wg2, wg3, bg0, bg1, bg2, bg3, wb_ref, o_ref,
                  wgb0, wgb1, wgb2, wgb3, wbb_ref):
    _stage_weights([wg0, wg1, wg2, wg3, wb_ref], [wgb0, wgb1, wgb2, wgb3, wbb_ref])
    h = h_ref[...]
    acc = None
    for n, (o_r, wg, bg) in enumerate(((oa_ref, wgb0, bg0), (ob_ref, wgb1, bg1), (oc_ref, wgb2, bg2),
                                       (od_ref, wgb3, bg3))):
        gate = jax.nn.sigmoid(jnp.dot(h, wg[...], preferred_element_type=F32) + bg[...])
        term = gate * jnp.dot(o_r[...], wbb_ref[n], preferred_element_type=F32)
        acc = term if acc is None else acc + term
    o_ref[...] = acc.astype(BF16)


def _merge(h, branches, w_gate, b_gate, w_branch, tn=512):
    T, D = h.shape
    nd = D // tn
    b_gate = b_gate.reshape(1, -1)
    once = pl.Buffered(1)
    specs = [pl.BlockSpec((ROW_TILE, D), lambda j, i: (i, 0))]
    specs += [pl.BlockSpec((ROW_TILE, BRANCH_WIDTH), lambda j, i: (i, 0))] * N_BRANCH
    specs += [pl.BlockSpec((D, tn), lambda j, i, n=n: (0, n * nd + j), pipeline_mode=once) for n in range(N_BRANCH)]
    specs += [pl.BlockSpec((1, tn), lambda j, i, n=n: (0, n * nd + j)) for n in range(N_BRANCH)]
    specs += [pl.BlockSpec((N_BRANCH, BRANCH_WIDTH, tn), lambda j, i: (0, 0, j), pipeline_mode=once)]
    return pl.pallas_call(
        _merge_kernel, grid=(nd, T // ROW_TILE), in_specs=specs,
        out_specs=pl.BlockSpec((ROW_TILE, tn), lambda j, i: (i, j)),
        out_shape=jax.ShapeDtypeStruct((T, D), BF16),
        scratch_shapes=[pltpu.VMEM((D, tn), BF16)] * N_BRANCH + [pltpu.VMEM((N_BRANCH, BRANCH_WIDTH, tn), BF16)],
        compiler_params=_cp("arbitrary", "arbitrary"), name="merge")(
            h, *branches, w_gate, w_gate, w_gate, w_gate, b_gate, b_gate, b_gate, b_gate, w_branch)


def _ffn_up_kernel(h_ref, w1_ref, w3_ref, o_ref, w1b_ref, w3b_ref):
    _stage_weights([w1_ref, w3_ref], [w1b_ref, w3b_ref])
    h = h_ref[...]
    a = jnp.dot(h, w1b_ref[...], preferred_element_type=F32)
    b = jnp.dot(h, w3b_ref[...], preferred_element_type=F32)
    o_ref[...] = (a * jax.nn.sigmoid(a) * b).astype(BF16)


def _moe_up_kernel(h_ref, w1_ref, w3_ref, gate_ref, o_ref, w1b_ref, w3b_ref):
    _stage_weights([w1_ref, w3_ref], [w1b_ref, w3b_ref])
    e = pl.program_id(0)
    h = h_ref[...]
    a = jnp.dot(h, w1b_ref[...], preferred_element_type=F32)
    b = jnp.dot(h, w3b_ref[...], preferred_element_type=F32)
    gate = gate_ref[...]
    lane = lax.broadcasted_iota(I32, gate.shape, 1)
    ge = jnp.sum(jnp.where(lane == e, gate, 0.0), axis=-1, keepdims=True)
    o_ref[...] = (a * jax.nn.sigmoid(a) * b * ge).astype(BF16)


def _ffn_up(h, w1, w3, tn=512):
    T, D = h.shape
    F = w1.shape[1]
    return pl.pallas_call(
        _ffn_up_kernel, grid=(F // tn, T // ROW_TILE),
        in_specs=[pl.BlockSpec((ROW_TILE, D), lambda j, i: (i, 0)),
                  pl.BlockSpec((D, tn), lambda j, i: (0, j)), pl.BlockSpec((D, tn), lambda j, i: (0, j))],
        out_specs=pl.BlockSpec((ROW_TILE, tn), lambda j, i: (i, j)),
        out_shape=jax.ShapeDtypeStruct((T, F), BF16),
        scratch_shapes=[pltpu.VMEM((D, tn), BF16)] * 2,
        compiler_params=_cp("arbitrary", "arbitrary"), name="ffn_up")(h, w1, w3)


def _moe_up(h, w1, w3, gate):
    T, D = h.shape
    E, _, F = w1.shape
    once = pl.Buffered(1)
    return pl.pallas_call(
        _moe_up_kernel, grid=(E, T // ROW_TILE),
        in_specs=[pl.BlockSpec((ROW_TILE, D), lambda e, i: (i, 0)),
                  pl.BlockSpec((None, D, F), lambda e, i: (e, 0, 0), pipeline_mode=once),
                  pl.BlockSpec((None, D, F), lambda e, i: (e, 0, 0), pipeline_mode=once),
                  pl.BlockSpec((ROW_TILE, LANE), lambda e, i: (i, 0))],
        out_specs=pl.BlockSpec((ROW_TILE, F), lambda e, i: (i, e)),
        out_shape=jax.ShapeDtypeStruct((T, E * F), BF16),
        scratch_shapes=[pltpu.VMEM((D, F), BF16)] * 2,
        compiler_params=_cp("arbitrary", "arbitrary"), name="moe_up")(h, w1, w3, gate)


def _final_norm_kernel(x_ref, g_ref, o_ref):
    o_ref[...] = _rms(x_ref[...], g_ref[...])


def _final_norm(x, g):
    T, D = x.shape
    return pl.pallas_call(
        _final_norm_kernel, grid=(T // ROW_TILE,),
        in_specs=[pl.BlockSpec((ROW_TILE, D), lambda i: (i, 0)), pl.BlockSpec((1, D), lambda i: (0, 0))],
        out_specs=pl.BlockSpec((ROW_TILE, D), lambda i: (i, 0)),
        out_shape=jax.ShapeDtypeStruct((T, D), F32),
        compiler_params=_cp("parallel"), name="final_norm")(x, g.reshape(1, D))


def _pad_rows(a, s_pad):
    pad = s_pad - a.shape[1]
    if pad == 0:
        return a
    return jnp.pad(a, ((0, 0), (0, pad)) + ((0, 0),) * (a.ndim - 2))


def _queries_t(q, n_slots, tq_pad):
    B, T, W = q.shape
    d = W // n_slots
    qt = q.reshape(B, T, n_slots, d).transpose(0, 2, 3, 1)
    if d < LANE:
        qt = jnp.pad(qt, ((0, 0), (0, 0), (0, LANE - d), (0, 0)))
    reps = -(-tq_pad // T)
    return jnp.concatenate([qt] * reps, axis=-1)[..., :tq_pad] if reps > 1 else qt


def _slot_query(qt_ref, hl, heads_per_slot):
    if heads_per_slot == 1:
        return qt_ref[hl]
    qp = qt_ref[hl // heads_per_slot]
    d = LANE // heads_per_slot
    r = hl % heads_per_slot
    parts = [qp[s * d:(s + 1) * d] if s == r else jnp.zeros((d, qp.shape[1]), qp.dtype)
             for s in range(heads_per_slot)]
    return jnp.concatenate(parts, axis=0)


def _rows_out(ot, T):
    B, W, _ = ot.shape
    return ot[:, :, :T].transpose(0, 2, 1).reshape(B * T, W)


def _pad_in_weight(w_in):
    D = w_in.shape[0]
    out = jnp.zeros((D, IN_PAD), BF16)
    for name in _ORDER:
        s0, sw = _SRC[name]
        d0, _ = _DST[name]
        out = lax.dynamic_update_slice(out, w_in[:, s0:s0 + sw].astype(BF16), (0, d0))
    return out


def _col(hw, name, width=None):
    off, w = _DST[name]
    return hw[:, off:off + (width or w)]


def _mixers(l, hw, q16, k16, caches, lp, geom):
    n_p, B_p, T_p, B_s, T_s, past = geom
    q_nope, q_pe, dsa_q, qi, diff_q = q16
    ckv_n, kpe, dsa_k, dsa_v, ki, diff_k, diff_v = k16
    wi = _col(hw, 'wi')
    T_all = hw.shape[0]

    def split(a):
        W = a.shape[1]
        return a[:n_p].reshape(B_p, T_p, W), a[n_p:].reshape(B_s, T_s, W)

    o_a_p, v_p = _gmlp(hw, 0, n_p, min(T_p, GM_CHUNK), lp['gm_ln_g'], lp['gm_ln_b'], lp['gm_ws'], lp['gm_bs'])
    o_a_s, v_s = _gmlp(hw, n_p, hw.shape[0] - n_p, min(T_s, GM_CHUNK), lp['gm_ln_g'], lp['gm_ln_b'],
                       lp['gm_ws'], lp['gm_bs'])
    c_p, c_s = min(T_p, GM_CHUNK), min(T_s, GM_CHUNK)
    gm_v_p = v_p.reshape(B_p, T_p, GM_WIDTH)[:, T_p - c_p:]
    gm_v_s = v_s.reshape(B_s, T_s, GM_WIDTH)[:, T_s - c_s:]

    outs = {}
    for grp in ('p', 's'):
        pi = 0 if grp == 'p' else 1
        B, T = (B_p, T_p) if grp == 'p' else (B_s, T_s)

        def g(a, pi=pi):
            return split(a)[pi]

        if grp == 'p':
            S = s_pad = T
            tq = min(Q_TILE, T)
            tq_pad = T
            causal = True
            direct = B == 1

            def ext(new16, cache, direct=direct):
                return new16.reshape(1, T_all, -1) if direct else g(new16)
        else:
            S = past + T
            s_pad = -(-S // KV_BLOCK) * KV_BLOCK
            tq = tq_pad = -(-T // LANE) * LANE
            causal = False

            def ext(new16, cache, s_pad=s_pad):
                old = cache.reshape(cache.shape[:2] + (-1,)).astype(BF16)
                new = g(new16)[..., :old.shape[-1]]
                return _pad_rows(jnp.concatenate([old, new], axis=1), s_pad)

        cm_ckv, cm_kr, cd_k, cd_v, cd_ki, cf_k, cf_v = caches

        all_c = ext(ckv_n, cm_ckv)
        all_pe = ext(kpe, cm_kr)
        if all_pe.shape[-1] < LANE:
            all_pe = jnp.pad(all_pe, ((0, 0), (0, 0), (0, LANE - all_pe.shape[-1])))
        rows = all_c.shape[1]
        k_mla, v_mla = _mla_kv(all_c.reshape(B * rows, MLA_KV_LORA), all_pe.reshape(B * rows, LANE),
                               lp['w_uk_slots'], lp['w_uv'])
        q_mla = jnp.concatenate([g(q_nope).reshape(B, T, MLA_HEADS, MLA_NOPE),
                                 g(q_pe).reshape(B, T, MLA_HEADS, MLA_ROPE)], axis=-1)
        o_b = _flash(_queries_t(q_mla.reshape(B, T, -1), MLA_HEADS, tq_pad),
                     k_mla.reshape(B, rows, MLA_HEADS * LANE), v_mla.reshape(B, rows, MLA_HEADS * MLA_V),
                     s_pad=s_pad, v_block0=0,
                     tq=tq, causal=causal, s_valid=S, dv=MLA_V, k_width=MLA_HEADS * LANE,
                     k_lanes=tuple(h * LANE for h in range(MLA_HEADS)), name="mla_attn")

        ki_all = ext(ki, cd_ki)
        if ki_all.shape[-1] < LANE:
            ki_all = jnp.pad(ki_all, ((0, 0), (0, 0), (0, LANE - ki_all.shape[-1])))
        wit = _queries_t(g(wi)[..., :IDX_HEADS], IDX_HEADS, tq_pad)[:, :, 0, :]
        o_c = _dsa(_queries_t(g(qi), IDX_HEADS, tq_pad), wit, ki_all,
                   _queries_t(g(dsa_q), DSA_HEADS // 2, tq_pad), ext(dsa_k, cd_k), ext(dsa_v, cd_v),
                   s_pad=s_pad, tq=tq, causal=causal, s_valid=S, n_sel=min(DSA_TOPK, S // 4))

        o_d = _flash(_queries_t(g(diff_q), DIFF_HEADS, tq_pad), ext(diff_k, cf_k), ext(diff_v, cf_v),
                     s_pad=s_pad, v_block0=0,
                     tq=tq, causal=causal, s_valid=S, dv=DIFF_V_DIM, k_width=MXU_WIDTH,
                     k_lanes=(0, 0, LANE, LANE),
                     diff=(lp['diff_lambda'], lp['diff_subln_g'], l), name="diff_attn")
        outs[grp] = (_rows_out(o_b, T), _rows_out(o_c, T), _rows_out(o_d, T))

    o_a = jnp.concatenate([o_a_p, o_a_s], axis=0)
    o_b, o_c, o_d = (jnp.concatenate([outs['p'][k], outs['s'][k]], axis=0) for k in range(3))
    return (o_a, o_b, o_c, o_d), gm_v_p, gm_v_s


def kernel(x_prompt, x_sample, c_prompt, c_sample, cache_mla_ckv, cache_mla_krope, cache_dsa_k, cache_dsa_v, cache_dsa_kidx, cache_diff_k, cache_diff_v, ada_w, ada_b, norm_mix_g, norm_ff_g, w_in, w_gate, b_gate, gm_ln_g, gm_ln_b, gm_ws, gm_bs, mla_q_norm_g, mla_kv_norm_g, mla_w_uq, mla_w_ukv, diff_lambda, diff_subln_g, w_branch, w_out, ffn_w1, ffn_w3, ffn_w2, moe_router_w, moe_router_b, moe_w1, moe_w3, moe_w2, final_norm_g):
    B_p, T_p, D = x_prompt.shape
    B_s, T_s, _ = x_sample.shape
    depth = ada_w.shape[0]
    past = cache_mla_ckv.shape[2]
    n_p, n_s = B_p * T_p, B_s * T_s
    assert T_p % ROW_TILE == 0 and n_s % ROW_TILE == 0 and ROW_TILE % T_s == 0
    assert past % CHUNK == 0 and T_s <= CHUNK
    n_p_tiles, tiles_per_b = n_p // ROW_TILE, T_p // ROW_TILE
    geom = (n_p, B_p, T_p, B_s, T_s, past)

    def mod_map(i):
        return jnp.where(i < n_p_tiles, i // tiles_per_b, B_p + i - n_p_tiles)

    x = jnp.concatenate([x_prompt.reshape(n_p, D), x_sample.reshape(n_s, D)], axis=0)

    n_c = B_p + B_s
    c_rows = jnp.concatenate([c_prompt, c_sample, jnp.zeros((-n_c % 8, D), F32)], axis=0)
    mod = _ada(c_rows, ada_w, ada_b)
    mod = mod[:, :n_c].reshape(depth, n_c, 6, D).transpose(0, 2, 1, 3)
    mod_p = jnp.broadcast_to(mod[:, :, :B_p, None, :], (depth, 6, B_p, ROW_TILE, D))
    mod_s = jnp.broadcast_to(mod[:, :, B_p:, None, :], (depth, 6, B_s, T_s, D)).reshape(
        depth, 6, n_s // ROW_TILE, ROW_TILE, D)
    modexp = jnp.concatenate([mod_p, mod_s], axis=2)

    pos = jnp.concatenate([jnp.tile(jnp.arange(T_p, dtype=I32), B_p),
                           jnp.tile(past + jnp.arange(T_s, dtype=I32), B_s)])
    t64 = _rope_tables(pos, DSA_ROT, DSA_HEAD_DIM)
    t32 = _rope_tables(pos, MLA_ROPE, MLA_ROPE)
    tidx = _rope_tables(pos, IDX_ROT, IDX_DIM)

    state = ([jnp.zeros((depth, n_p, w), F32) for w in _STATE_WIDTHS]
             + [jnp.zeros((depth, n_s, w), F32) for w in _STATE_WIDTHS])
    gm_p, gm_s = [], []
    for l in range(depth):
        uq = mla_w_uq[l].reshape(MLA_Q_LORA, MLA_HEADS, MLA_NOPE + MLA_ROPE)
        ukv = mla_w_ukv[l].reshape(MLA_KV_LORA, MLA_HEADS, MLA_NOPE + MLA_V)
        lp = {
            'gm_ln_g': gm_ln_g[l], 'gm_ln_b': gm_ln_b[l], 'gm_ws': gm_ws[l], 'gm_bs': gm_bs[l],
            'w_uk_slots': jnp.pad(ukv[..., :MLA_NOPE], ((0, 0), (0, 0), (0, LANE - MLA_NOPE))).reshape(
                MLA_KV_LORA, MLA_HEADS * LANE).astype(BF16),
            'w_uv': ukv[..., MLA_NOPE:].reshape(MLA_KV_LORA, -1).astype(BF16),
            'diff_lambda': diff_lambda[l], 'diff_subln_g': diff_subln_g[l],
        }
        h = _norm_mod(x, norm_mix_g[l], modexp[l], mod_map, 1, 0)
        hw = _mm(h, _pad_in_weight(w_in[l]), F32, tn=IN_PAD // 2, name="in_proj")
        state, q16, k16 = _state(l, depth, n_p, state, hw, mla_q_norm_g[l], mla_kv_norm_g[l],
                                 uq[..., :MLA_NOPE].reshape(MLA_Q_LORA, -1).astype(BF16),
                                 uq[..., MLA_NOPE:].reshape(MLA_Q_LORA, -1).astype(BF16), t64, t32, tidx)
        caches = (cache_mla_ckv[l], cache_mla_krope[l], cache_dsa_k[l], cache_dsa_v[l], cache_dsa_kidx[l],
                  cache_diff_k[l], cache_diff_v[l])
        branches, gm_v_p, gm_v_s = _mixers(l, hw, q16, k16, caches, lp, geom)
        gm_p.append(gm_v_p)
        gm_s.append(gm_v_s)
        merged = _merge(h, branches, w_gate[l], b_gate[l], w_branch[l])
        x = _mm_res(merged, w_out[l], x, modexp[l], mod_map, 2, tn=1024, name="out_proj")

        j = l // 2
        if l % 2 == 0:
            h2 = _norm_mod(x, norm_ff_g[l], modexp[l], mod_map, 4, 3)
            act = _ffn_up(h2, ffn_w1[j], ffn_w3[j])
            w2 = ffn_w2[j]
        else:
            rw = jnp.zeros((D, LANE), F32).at[:, :N_EXPERTS].set(moe_router_w[j])
            rw_hi = rw.astype(BF16)
            rw_lo = (rw - rw_hi.astype(F32)).astype(BF16)
            rb = jnp.zeros((1, LANE), F32).at[0, :N_EXPERTS].set(moe_router_b[j])
            h2, gate = _norm_mod(x, norm_ff_g[l], modexp[l], mod_map, 4, 3, router=(rw_hi, rw_lo, rb))
            act = _moe_up(h2, moe_w1[j], moe_w3[j], gate)
            w2 = moe_w2[j].astype(BF16).reshape(-1, D)
        x = _mm_res(act, w2, x, modexp[l], mod_map, 5, tn=512 if w2.dtype == F32 else MXU_WIDTH, name="ffn_down")

    y = _final_norm(x, final_norm_g)
    outs = [y[:n_p].reshape(B_p, T_p, D), y[n_p:].reshape(B_s, T_s, D),
            jnp.stack(gm_p, axis=0), jnp.stack(gm_s, axis=0)]
    tails = ((), (), (DSA_HEADS, DSA_HEAD_DIM), (DSA_HEADS, DSA_HEAD_DIM), (), (DIFF_HEADS, 2, DIFF_QK_DIM),
             (DIFF_HEADS, DIFF_V_DIM))
    for k, tail in enumerate(tails):
        w = _STATE_WIDTHS[k]
        outs.append(state[k].reshape((depth, B_p, T_p) + (tail or (w,))))
        outs.append(state[N_STATE + k].reshape((depth, B_s, T_s) + (tail or (w,))))
    return tuple(outs)
```

```python
import functools
import math

import jax
import jax.numpy as jnp
from jax import lax
from jax.experimental import pallas as pl
from jax.experimental.pallas import tpu as pltpu

F32 = jnp.float32
BF16 = jnp.bfloat16
I32 = jnp.int32

CHUNK = 64
ROPE_THETA = 500000.0
NORM_EPS = 1e-6
NEG_INF = -1e30
N_BRANCH = 4
BRANCH_WIDTH = 512
GM_CHUNK = 128
GM_GROUPS = 4
GM_WIDTH = 512
GM_GROUP_DIM = GM_WIDTH // GM_GROUPS
MLA_HEADS = 8
MLA_Q_LORA = 512
MLA_KV_LORA = 256
MLA_NOPE = 64
MLA_ROPE = 32
MLA_V = 64
DSA_HEADS = 8
DSA_HEAD_DIM = 64
DSA_ROT = DSA_HEAD_DIM // 4
IDX_HEADS = 8
IDX_DIM = 32
IDX_ROT = IDX_DIM // 4
IDX_SCALE = (IDX_DIM ** -0.5) * (IDX_HEADS ** -0.5)
DSA_TOPK = 256
DIFF_HEADS = 4
DIFF_QK_DIM = 64
DIFF_V_DIM = 128
DIFF_ROT = DIFF_QK_DIM // 4
N_EXPERTS = 8

LANE = 128
MXU_WIDTH = 256
VMEM_LIMIT_BYTES = 56 * 1024 * 1024

ROW_TILE = 512
LOG2E = 1.4426950408889634
INT_MIN = -2 ** 31
MLA_QSCALE = (MLA_NOPE + MLA_ROPE) ** -0.5 * LOG2E
DSA_QSCALE = DSA_HEAD_DIM ** -0.5 * LOG2E
DIFF_QSCALE = DIFF_QK_DIM ** -0.5 * LOG2E
KV_BLOCK = 512
Q_TILE = 512

_SPLIT = (
    ('gu', GM_WIDTH), ('gv', GM_WIDTH), ('cq', MLA_Q_LORA), ('ckv', MLA_KV_LORA), ('krope', MLA_ROPE),
    ('dq', 512), ('dk', 512), ('dv', 512), ('qi', IDX_HEADS * IDX_DIM), ('ki', IDX_DIM), ('wi', IDX_HEADS),
    ('fq', 512), ('fk', 512), ('fv', 512),
)
_ORDER = ('gu', 'gv', 'cq', 'dq', 'dk', 'dv', 'fq', 'fk', 'fv', 'ckv', 'qi', 'krope', 'ki', 'wi')


def _in_layout():
    src, acc = {}, 0
    for name, w in _SPLIT:
        src[name] = (acc, w)
        acc += w
    dst, off = {}, 0
    for name in _ORDER:
        w = src[name][1]
        pw = -(-w // LANE) * LANE
        dst[name] = (off, pw)
        off += pw
    total = -(-off // ROW_TILE) * ROW_TILE
    return src, dst, total


_SRC, _DST, IN_PAD = _in_layout()


def _cp(*sem):
    return pltpu.CompilerParams(dimension_semantics=sem, vmem_limit_bytes=VMEM_LIMIT_BYTES)


def _rms(x, g):
    return x * lax.rsqrt(jnp.mean(x * x, axis=-1, keepdims=True) + NORM_EPS) * g


def _ada_kernel(c_ref, w_ref, b_ref, o_ref):
    c = c_ref[...]
    a = (c * jax.nn.sigmoid(c)).astype(BF16)
    o_ref[...] = jnp.dot(a, w_ref[...].astype(BF16), preferred_element_type=F32) + b_ref[...]


def _ada(c_rows, ada_w, ada_b):
    L, D, N = ada_w.shape
    R = c_rows.shape[0]
    tn = 512
    return pl.pallas_call(
        _ada_kernel,
        grid=(L, N // tn),
        in_specs=[pl.BlockSpec((R, D), lambda l, j: (0, 0)),
                  pl.BlockSpec((None, D, tn), lambda l, j: (l, 0, j)),
                  pl.BlockSpec((None, 1, tn), lambda l, j: (l, 0, j))],
        out_specs=pl.BlockSpec((None, R, tn), lambda l, j: (l, 0, j)),
        out_shape=jax.ShapeDtypeStruct((L, R, N), F32),
        compiler_params=_cp("parallel", "parallel"),
        name="ada_mod",
    )(c_rows, ada_w, ada_b.reshape(L, 1, N))


def _norm_mod_kernel(x_ref, g_ref, sc_ref, sh_ref, h_ref):
    y = _rms(x_ref[...], g_ref[...])
    h_ref[...] = (y * (1.0 + sc_ref[...]) + sh_ref[...]).astype(BF16)


def _norm_mod_router_kernel(x_ref, g_ref, sc_ref, sh_ref, rwh_ref, rwl_ref, rb_ref, h_ref, gate_ref):
    y = _rms(x_ref[...], g_ref[...])
    h = y * (1.0 + sc_ref[...]) + sh_ref[...]
    h_hi = h.astype(BF16)
    h_ref[...] = h_hi
    h_lo = (h - h_hi.astype(F32)).astype(BF16)
    logits = (jnp.dot(h_hi, rwh_ref[...], preferred_element_type=F32)
              + jnp.dot(h_lo, rwh_ref[...], preferred_element_type=F32)
              + jnp.dot(h_hi, rwl_ref[...], preferred_element_type=F32)) + rb_ref[...]
    lane = lax.broadcasted_iota(I32, logits.shape, 1)
    lg = jnp.where(lane < N_EXPERTS, logits, -jnp.inf)
    m1 = jnp.max(lg, axis=-1, keepdims=True)
    i1 = jnp.min(jnp.where(lg == m1, lane, LANE), axis=-1, keepdims=True)
    lg2 = jnp.where(lane == i1, -jnp.inf, lg)
    m2 = jnp.max(lg2, axis=-1, keepdims=True)
    i2 = jnp.min(jnp.where(lg2 == m2, lane, LANE), axis=-1, keepdims=True)
    e2 = jnp.exp(m2 - m1)
    w1 = 1.0 / (1.0 + e2)
    gate_ref[...] = jnp.where(lane == i1, w1, 0.0) + jnp.where(lane == i2, e2 * w1, 0.0)


def _mod_spec(mod_map, k, D):
    return pl.BlockSpec((None, None, ROW_TILE, D), lambda i, *_: (k, mod_map(i), 0, 0))


def _norm_mod(x, g, modexp, mod_map, k_scale, k_shift, router=None):
    T, D = x.shape
    specs = [pl.BlockSpec((ROW_TILE, D), lambda i: (i, 0)),
             pl.BlockSpec((1, D), lambda i: (0, 0)),
             _mod_spec(mod_map, k_scale, D), _mod_spec(mod_map, k_shift, D)]
    args = [x, g.reshape(1, D), modexp, modexp]
    h_spec = pl.BlockSpec((ROW_TILE, D), lambda i: (i, 0))
    h_shape = jax.ShapeDtypeStruct((T, D), BF16)
    if router is None:
        return pl.pallas_call(
            _norm_mod_kernel, grid=(T // ROW_TILE,), in_specs=specs, out_specs=h_spec, out_shape=h_shape,
            compiler_params=_cp("parallel"), name="norm_mod")(*args)
    rw_hi, rw_lo, rb = router
    specs += [pl.BlockSpec((D, LANE), lambda i: (0, 0)), pl.BlockSpec((D, LANE), lambda i: (0, 0)),
              pl.BlockSpec((1, LANE), lambda i: (0, 0))]
    return pl.pallas_call(
        _norm_mod_router_kernel, grid=(T // ROW_TILE,), in_specs=specs,
        out_specs=[h_spec, pl.BlockSpec((ROW_TILE, LANE), lambda i: (i, 0))],
        out_shape=[h_shape, jax.ShapeDtypeStruct((T, LANE), F32)],
        compiler_params=_cp("parallel"), name="norm_mod_router")(*args, rw_hi, rw_lo, rb)


def _mm_kernel(a_ref, b_ref, o_ref):
    o_ref[...] = jnp.dot(a_ref[...], b_ref[...], preferred_element_type=F32).astype(o_ref.dtype)


def _mm(a, b, out_dtype, tm=ROW_TILE, tn=512, name="mm"):
    M, K = a.shape
    N = b.shape[1]
    return pl.pallas_call(
        _mm_kernel, grid=(M // tm, N // tn),
        in_specs=[pl.BlockSpec((tm, K), lambda i, j: (i, 0)), pl.BlockSpec((K, tn), lambda i, j: (0, j))],
        out_specs=pl.BlockSpec((tm, tn), lambda i, j: (i, j)),
        out_shape=jax.ShapeDtypeStruct((M, N), out_dtype),
        compiler_params=_cp("parallel", "parallel"), name=name)(a, b)


def _mla_kv_kernel(c_ref, pe_ref, wk_ref, wv_ref, k_ref, v_ref):
    c = c_ref[...]
    slot = pltpu.roll(pe_ref[...].astype(F32), MLA_NOPE, axis=1)
    k = jnp.dot(c, wk_ref[...], preferred_element_type=F32) + jnp.concatenate([slot] * MLA_HEADS, axis=1)
    k_ref[...] = k.astype(BF16)
    v_ref[...] = jnp.dot(c, wv_ref[...], preferred_element_type=F32).astype(BF16)


def _mla_kv(c, pe, w_uk_slots, w_uv):
    M, K = c.shape
    nk, nv = w_uk_slots.shape[1], w_uv.shape[1]
    tm = ROW_TILE
    return pl.pallas_call(
        _mla_kv_kernel, grid=(M // tm,),
        in_specs=[pl.BlockSpec((tm, K), lambda i: (i, 0)), pl.BlockSpec((tm, LANE), lambda i: (i, 0)),
                  pl.BlockSpec((K, nk), lambda i: (0, 0)), pl.BlockSpec((K, nv), lambda i: (0, 0))],
        out_specs=[pl.BlockSpec((tm, nk), lambda i: (i, 0)), pl.BlockSpec((tm, nv), lambda i: (i, 0))],
        out_shape=[jax.ShapeDtypeStruct((M, nk), BF16), jax.ShapeDtypeStruct((M, nv), BF16)],
        compiler_params=_cp("parallel"), name="mla_kv")(c, pe, w_uk_slots, w_uv)


def _stage_weights(w_refs, wb_refs):
    @pl.when(pl.program_id(1) == 0)
    def _():
        for w_ref, wb_ref in zip(w_refs, wb_refs):
            wb_ref[...] = w_ref[...].astype(BF16)


def _mm_res_kernel(a_ref, b_ref, x_ref, g_ref, o_ref):
    o_ref[...] = x_ref[...] + g_ref[...] * jnp.dot(a_ref[...], b_ref[...], preferred_element_type=F32)


def _mm_res_ws_kernel(a_ref, w_ref, x_ref, g_ref, o_ref, wb_ref):
    _stage_weights([w_ref], [wb_ref])
    o_ref[...] = x_ref[...] + g_ref[...] * jnp.dot(a_ref[...], wb_ref[...], preferred_element_type=F32)


def _mm_res(a, b, x, modexp, mod_map, k_gate, tn, name="mm_res"):
    M, K = a.shape
    N = b.shape[1]
    tm = ROW_TILE
    out_shape = jax.ShapeDtypeStruct((M, N), F32)
    if b.dtype == BF16:
        return pl.pallas_call(
            _mm_res_kernel, grid=(M // tm, N // tn),
            in_specs=[pl.BlockSpec((tm, K), lambda i, j: (i, 0)),
                      pl.BlockSpec((K, tn), lambda i, j: (0, j)),
                      pl.BlockSpec((tm, tn), lambda i, j: (i, j)),
                      pl.BlockSpec((None, None, tm, tn), lambda i, j: (k_gate, mod_map(i), 0, j))],
            out_specs=pl.BlockSpec((tm, tn), lambda i, j: (i, j)), out_shape=out_shape,
            compiler_params=_cp("parallel", "parallel"), name=name)(a, b, x, modexp)
    return pl.pallas_call(
        _mm_res_ws_kernel, grid=(N // tn, M // tm),
        in_specs=[pl.BlockSpec((tm, K), lambda j, i: (i, 0)),
                  pl.BlockSpec((K, tn), lambda j, i: (0, j)),
                  pl.BlockSpec((tm, tn), lambda j, i: (i, j)),
                  pl.BlockSpec((None, None, tm, tn), lambda j, i: (k_gate, mod_map(i), 0, j))],
        out_specs=pl.BlockSpec((tm, tn), lambda j, i: (i, j)), out_shape=out_shape,
        scratch_shapes=[pltpu.VMEM((K, tn), BF16)],
        compiler_params=_cp("arbitrary", "arbitrary"), name=name)(a, b, x, modexp)


def _gmlp_kernel(gu_ref, gv_ref, lng_ref, lnb_ref, w_ref, bias_ref, o_ref, v_ref, *, c):
    u = jax.nn.gelu(gu_ref[...])
    gv = jax.nn.gelu(gv_ref[...])
    mu = jnp.mean(gv, axis=-1, keepdims=True)
    var = jnp.mean(jnp.square(gv - mu), axis=-1, keepdims=True)
    v = (gv - mu) * lax.rsqrt(var + NORM_EPS) * lng_ref[...] + lnb_ref[...]
    v_ref[...] = v
    vb = v.astype(BF16)
    row = lax.broadcasted_iota(I32, (c, c), 0)
    col = lax.broadcasted_iota(I32, (c, c), 1)
    bias = bias_ref[...]
    for g in range(GM_GROUPS):
        w = jnp.where(row >= col, w_ref[g], 0.0).astype(BF16)
        lo, hi = g * GM_GROUP_DIM, (g + 1) * GM_GROUP_DIM
        for n in range(ROW_TILE // c):
            r0, r1 = n * c, (n + 1) * c
            sp = jnp.dot(w, vb[r0:r1, lo:hi], preferred_element_type=F32) + bias[:, g:g + 1]
            o_ref[r0:r1, lo:hi] = (u[r0:r1, lo:hi] * sp).astype(BF16)


def _gmlp(hw, row0, rows, c, ln_g, ln_b, ws, bs):
    t0 = row0 // ROW_TILE
    gu_blk = _DST['gu'][0] // GM_WIDTH
    gv_blk = _DST['gv'][0] // GM_WIDTH
    w = ws[:, :c, :c]
    bias = bs[:, :c].T
    return pl.pallas_call(
        functools.partial(_gmlp_kernel, c=c), grid=(rows // ROW_TILE,),
        in_specs=[pl.BlockSpec((ROW_TILE, GM_WIDTH), lambda i: (t0 + i, gu_blk)),
                  pl.BlockSpec((ROW_TILE, GM_WIDTH), lambda i: (t0 + i, gv_blk)),
                  pl.BlockSpec((1, GM_WIDTH), lambda i: (0, 0)),
                  pl.BlockSpec((1, GM_WIDTH), lambda i: (0, 0)),
                  pl.BlockSpec((GM_GROUPS, c, c), lambda i: (0, 0, 0)),
                  pl.BlockSpec((c, GM_GROUPS), lambda i: (0, 0))],
        out_specs=[pl.BlockSpec((ROW_TILE, GM_WIDTH), lambda i: (i, 0)),
                   pl.BlockSpec((ROW_TILE, GM_WIDTH), lambda i: (i, 0))],
        out_shape=[jax.ShapeDtypeStruct((rows, GM_WIDTH), BF16), jax.ShapeDtypeStruct((rows, GM_WIDTH), F32)],
        compiler_params=_cp("parallel"), name="gmlp")(
            hw, hw, ln_g.reshape(1, -1), ln_b.reshape(1, -1), w, bias)


def _rope_tables(pos, rot, period):
    half = rot // 2
    inv = ROPE_THETA ** (-jnp.arange(half, dtype=F32) / half)
    ang = pos.astype(F32)[:, None] * inv[None, :]
    cos, sin = jnp.cos(ang), jnp.sin(ang)
    T = pos.shape[0]
    pad = jnp.zeros((T, period - rot), F32)
    c = jnp.concatenate([cos, cos, pad + 1.0], axis=1)
    s1 = jnp.concatenate([-sin, jnp.zeros_like(sin), pad], axis=1)
    s2 = jnp.concatenate([jnp.zeros_like(sin), sin, pad], axis=1)
    rep = LANE // period
    return jnp.stack([jnp.tile(c, (1, rep)), jnp.tile(s1, (1, rep)), jnp.tile(s2, (1, rep))], axis=0)


def _rope(x, tab_ref, half):
    w = x.shape[1]
    rep = w // LANE

    def wide(t):
        return t if rep == 1 else jnp.concatenate([t] * rep, axis=1)

    xl = pltpu.roll(x, w - half, axis=1)
    xr = pltpu.roll(x, half, axis=1)
    return x * wide(tab_ref[0]) + xl * wide(tab_ref[1]) + xr * wide(tab_ref[2])


_STATE_WIDTHS = (MLA_KV_LORA, MLA_ROPE, 512, 512, IDX_DIM, 512, 512)
N_STATE = len(_STATE_WIDTHS)
STATE_TILE = 256


def _state_kernel(*refs, n_p_tiles):
    (cq_ref, ckv_ref, kr_ref, dq_ref, dk_ref, dv_ref, qi_ref, ki_ref, fq_ref, fk_ref, fv_ref,
     qg_ref, kvg_ref, wqn_ref, wqp_ref, t64_ref, t32_ref, tidx_ref) = refs[:18]
    outs = refs[18 + 2 * N_STATE:]
    st_p, st_s = outs[:N_STATE], outs[N_STATE:2 * N_STATE]
    (qn_o, qp_o, dsaq_o, qi_o, diffq_o, ckv_b, kpe_b, dsak_b, dsav_b, ki_b, diffk_b, diffv_b) = outs[2 * N_STATE:]
    cqn = _rms(cq_ref[...], qg_ref[...]).astype(BF16)
    qn_o[...] = (jnp.dot(cqn, wqn_ref[...], preferred_element_type=F32) * MLA_QSCALE).astype(BF16)
    qp = jnp.dot(cqn, wqp_ref[...], preferred_element_type=F32)
    qp_o[...] = (_rope(qp, t32_ref, MLA_ROPE // 2) * MLA_QSCALE).astype(BF16)
    dsaq_o[...] = (_rope(dq_ref[...], t64_ref, DSA_ROT // 2) * DSA_QSCALE).astype(BF16)
    qi_o[...] = _rope(qi_ref[...], tidx_ref, IDX_ROT // 2).astype(BF16)
    diffq_o[...] = (_rope(fq_ref[...], t64_ref, DIFF_ROT // 2) * DIFF_QSCALE).astype(BF16)
    new = (_rms(ckv_ref[...], kvg_ref[...]), _rope(kr_ref[...], t32_ref, MLA_ROPE // 2),
           _rope(dk_ref[...], t64_ref, DSA_ROT // 2), dv_ref[...],
           _rope(ki_ref[...], tidx_ref, IDX_ROT // 2), _rope(fk_ref[...], t64_ref, DIFF_ROT // 2), fv_ref[...])
    for val, b_ref in zip(new, (ckv_b, kpe_b, dsak_b, dsav_b, ki_b, diffk_b, diffv_b)):
        b_ref[...] = val.astype(BF16)
    i = pl.program_id(0)

    @pl.when(i < n_p_tiles)
    def _():
        for val, o_ref, w in zip(new, st_p, _STATE_WIDTHS):
            o_ref[...] = val[:, :w]

    @pl.when(i >= n_p_tiles)
    def _():
        for val, o_ref, w in zip(new, st_s, _STATE_WIDTHS):
            o_ref[...] = val[:, :w]


def _state(l, depth, n_p, prev, hw, q_norm_g, kv_norm_g, w_uq_nope, w_uq_pe, t64, t32, tidx):
    T = hw.shape[0]
    rt = STATE_TILE
    n_s = T - n_p
    npt = n_p // rt

    def col(name):
        off, w = _DST[name]
        return pl.BlockSpec((rt, w), lambda i: (i, off // w))

    def full(shape):
        return pl.BlockSpec(shape, lambda i: (0,) * len(shape))

    def tab():
        return pl.BlockSpec((3, rt, LANE), lambda i: (0, i, 0))

    def out(w, dt=BF16):
        return pl.BlockSpec((rt, w), lambda i: (i, 0)), jax.ShapeDtypeStruct((T, w), dt)

    outs = [(pl.BlockSpec((None, rt, w), lambda i: (l, jnp.minimum(i, npt - 1), 0)),
             jax.ShapeDtypeStruct((depth, n_p, w), F32)) for w in _STATE_WIDTHS]
    outs += [(pl.BlockSpec((None, rt, w), lambda i: (l, jnp.maximum(i - npt, 0), 0)),
              jax.ShapeDtypeStruct((depth, n_s, w), F32)) for w in _STATE_WIDTHS]
    outs += [out(512), out(MLA_HEADS * MLA_ROPE), out(512), out(IDX_HEADS * IDX_DIM), out(512)]
    outs += [out(MLA_KV_LORA), out(LANE), out(512), out(512), out(LANE), out(512), out(512)]
    in_specs = [col('cq'), col('ckv'), col('krope'), col('dq'), col('dk'), col('dv'), col('qi'), col('ki'),
                col('fq'), col('fk'), col('fv'),
                full((1, MLA_Q_LORA)), full((1, MLA_KV_LORA)),
                full((MLA_Q_LORA, MLA_HEADS * MLA_NOPE)), full((MLA_Q_LORA, MLA_HEADS * MLA_ROPE)),
                tab(), tab(), tab()]
    args = [hw] * 11 + [q_norm_g.reshape(1, -1), kv_norm_g.reshape(1, -1), w_uq_nope, w_uq_pe, t64, t32, tidx]
    in_specs += [pl.BlockSpec(memory_space=pl.ANY)] * (2 * N_STATE)
    args += list(prev)
    res = pl.pallas_call(
        functools.partial(_state_kernel, n_p_tiles=npt), grid=(T // rt,),
        in_specs=in_specs, out_specs=[o[0] for o in outs], out_shape=[o[1] for o in outs],
        input_output_aliases={18 + k: k for k in range(2 * N_STATE)},
        compiler_params=_cp("arbitrary"), name="mixer_state")(*args)
    return res[:2 * N_STATE], res[2 * N_STATE:2 * N_STATE + 5], res[2 * N_STATE + 5:]


def _block_offset(j):
    return j * KV_BLOCK if isinstance(j, int) else pl.multiple_of(j * KV_BLOCK, KV_BLOCK)


def _softmax_steps(scores, values, carries):
    stats = []
    for s, (m, l, _) in zip(scores, carries):
        m_new = jnp.maximum(m, jnp.max(s, axis=0, keepdims=True))
        alpha = jnp.exp2(m - m_new)
        p = jnp.exp2(s - m_new)
        stats.append((m_new, alpha * l + jnp.sum(p, axis=0, keepdims=True), alpha, p.astype(BF16)))
    pvs = [lax.dot_general(v, st[3], (((0,), (0,)), ((), ())), preferred_element_type=F32)
           for v, st in zip(values, stats)]
    return tuple((st[0], st[1], st[2] * c[2] + pv) for st, c, pv in zip(stats, carries, pvs))


def _softmax_init(dv, tq):
    return jnp.full((1, tq), NEG_INF, F32), jnp.zeros((1, tq), F32), jnp.zeros((dv, tq), F32)


def _attend_blocks(n_blocks, qk, values, carries, mask_last):
    carries = lax.fori_loop(0, n_blocks - 1, lambda j, c: _softmax_steps(qk(j), values(j), c), carries)
    scores = qk(n_blocks - 1)
    if mask_last is not None:
        scores = mask_last(scores)
    return _softmax_steps(scores, values(n_blocks - 1), carries)


def _pair_loop(lo, hi, step, carry):
    n2 = (hi - lo) // 2

    def two(t, c):
        j = lo + 2 * t
        return step(j + 1, step(j, c))

    carry = lax.fori_loop(0, n2, two, carry)
    return lax.fori_loop(lo + 2 * n2, hi, step, carry)


def _flash_kernel(*refs, tq, nkb, causal, s_valid, k_lanes, v_rows, dv, diff_layer):
    if diff_layer is None:
        qt_ref, k_ref, vt_ref, o_ref = refs
    else:
        qt_ref, k_ref, vt_ref, lam_ref, sg_ref, o_ref = refs
    i = pl.program_id(2)
    padded = s_valid < nkb * KV_BLOCK

    hg = len(k_lanes)
    n_blocks = i * tq // KV_BLOCK + 1 if causal else nkb
    qts = [_slot_query(qt_ref, hl, k_lanes.count(k_lanes[hl])) for hl in range(hg)]

    def qk(j):
        off = _block_offset(j)
        return tuple(jnp.dot(k_ref[pl.ds(off, KV_BLOCK), k0:k0 + LANE], qts[hl], preferred_element_type=F32)
                     for hl, k0 in enumerate(k_lanes))

    def values(j):
        return [vt_ref[pl.ds(_block_offset(j), KV_BLOCK), v0:v0 + dv] for v0 in v_rows]

    def mask_last(scores):
        kpos = (n_blocks - 1) * KV_BLOCK + lax.broadcasted_iota(I32, (KV_BLOCK, tq), 0)
        if causal:
            qpos = i * tq + lax.broadcasted_iota(I32, (KV_BLOCK, tq), 1)
            ok = kpos // CHUNK <= qpos // CHUNK
        else:
            ok = kpos < s_valid
        return [jnp.where(ok, s, NEG_INF) for s in scores]

    carries = _attend_blocks(n_blocks, qk, values, tuple(_softmax_init(dv, tq) for _ in range(hg)),
                             mask_last if (causal or padded) else None)
    heads = [acc / l for _, l, acc in carries]

    if diff_layer is None:
        for hl in range(hg):
            o_ref[v_rows[hl]:v_rows[hl] + dv, :] = heads[hl].astype(o_ref.dtype)
    else:
        lam_init = 0.8 - 0.6 * math.exp(-0.3 * diff_layer)
        lp = lam_ref[...]
        lam = (jnp.exp(jnp.sum(lp[0:1] * lp[1:2], axis=-1, keepdims=True))
               - jnp.exp(jnp.sum(lp[2:3] * lp[3:4], axis=-1, keepdims=True)) + lam_init)
        for h2 in range(hg // 2):
            o = heads[2 * h2] - lam * heads[2 * h2 + 1]
            y = o * lax.rsqrt(jnp.mean(o * o, axis=0, keepdims=True) + NORM_EPS) * sg_ref[...]
            o_ref[v_rows[2 * h2]:v_rows[2 * h2] + dv, :] = (y * (1.0 - lam_init)).astype(o_ref.dtype)


def _flash(qt, k, vt, *, s_pad, v_block0, tq, causal, s_valid, dv, k_width, k_lanes, diff=None, name="flash"):
    B, n_slots, _, Tq = qt.shape
    nkb = s_pad // KV_BLOCK
    S_pad = s_pad
    assert not causal or (tq <= KV_BLOCK and KV_BLOCK % tq == 0)
    hg = len(k_lanes)
    share = k_lanes.count(k_lanes[0])
    n_grp = n_slots * share // hg
    v_rows = tuple(hl // share * dv for hl in range(hg))
    v_width = hg // share * dv
    in_specs = [pl.BlockSpec((None, hg // share, LANE, tq), lambda b, g, i: (b, g, 0, i)),
                pl.BlockSpec((None, S_pad, k_width), lambda b, g, i: (b, 0, g)),
                pl.BlockSpec((None, S_pad, v_width), lambda b, g, i: (b, 0, v_block0 + g))]
    args = [qt, k, vt]
    diff_layer = None
    if diff is not None:
        lam_params, subln_g, diff_layer = diff
        in_specs += [pl.BlockSpec(lam_params.shape, lambda b, g, i: (0, 0)),
                     pl.BlockSpec((dv, tq), lambda b, g, i: (0, 0))]
        args += [lam_params, jnp.broadcast_to(subln_g[:, None], (dv, tq))]
    kern = functools.partial(_flash_kernel, tq=tq, nkb=nkb, causal=causal, s_valid=s_valid,
                             k_lanes=k_lanes, v_rows=v_rows, dv=dv, diff_layer=diff_layer)
    return pl.pallas_call(
        kern, grid=(B, n_grp, Tq // tq), in_specs=in_specs,
        out_specs=pl.BlockSpec((None, v_width, tq), lambda b, g, i: (b, g, i)),
        out_shape=jax.ShapeDtypeStruct((B, n_grp * v_width, Tq), BF16),
        compiler_params=_cp("parallel", "parallel", "parallel"), name=name)(*args)


def _dsa_kernel(qit_ref, wit_ref, ki_ref, qt_ref, k_ref, vt_ref, o_ref, key_ref,
                *, tq, nkb, causal, s_valid, n_sel):
    i = pl.program_id(1)
    tkb = KV_BLOCK
    padded = s_valid < nkb * tkb
    nvis = jnp.minimum((i * tq + tq + tkb - 1) // tkb, nkb) if causal else nkb
    wit = wit_ref[...] * IDX_SCALE

    def kpos(j):
        return j * tkb + lax.broadcasted_iota(I32, (tkb, tq), 0)

    def score_block(j, masked):
        kib = ki_ref[pl.ds(_block_offset(j), tkb), :]
        rs = [jnp.dot(kib, qit_ref[h], preferred_element_type=F32) for h in range(IDX_HEADS)]
        sc = jnp.zeros((tkb, tq), F32)
        for h in range(IDX_HEADS):
            sc = sc + jnp.maximum(rs[h], 0.0) * wit[h:h + 1, :]
        sc = sc + 0.0
        bits = pltpu.bitcast(sc, I32)
        key = jnp.where(bits < 0, bits ^ 0x7FFFFFFF, bits)
        if masked and causal:
            qpos = i * tq + lax.broadcasted_iota(I32, (tkb, tq), 1)
            key = jnp.where(kpos(j) // CHUNK <= qpos // CHUNK, key, INT_MIN)
        elif masked:
            key = jnp.where(kpos(j) < s_valid, key, INT_MIN)
        key_ref[j] = key

    lax.fori_loop(0, nvis - 1, lambda j, c: (score_block(j, False), c)[1], 0)
    score_block(nvis - 1, causal or padded)

    def count(pred):
        def body(j, cnt):
            hit = jnp.where(pred(key_ref[j], j), 1.0, 0.0)
            part = hit[0:8]
            for r in range(1, tkb // 8):
                part = part + hit[r * 8:(r + 1) * 8]
            return cnt + part
        cnt = _pair_loop(0, nvis, body, jnp.zeros((8, tq), F32))
        return jnp.sum(cnt, axis=0, keepdims=True)

    gmax = lax.fori_loop(0, nvis, lambda j, m: jnp.maximum(m, key_ref[j]), jnp.full((tkb, tq), INT_MIN, I32))
    hi = jnp.max(gmax, axis=0, keepdims=True)
    while gmax.shape[0] // 2 >= max(n_sel, 8):
        half = gmax.shape[0] // 2
        gmax = jnp.maximum(gmax[:half], gmax[half:])
    lo = jnp.min(gmax, axis=0, keepdims=True)
    nbits = jnp.max((32 - lax.clz(lo ^ hi)).astype(F32)).astype(I32)
    low_mask = lax.shift_left(jnp.int32(1), jnp.minimum(nbits, 31)) - 1
    t0 = jnp.where(nbits >= 32, INT_MIN, lo & ~low_mask)

    def bit_step(it, t):
        cand = t + lax.shift_left(jnp.int32(1), nbits - 1 - it)
        cnt = count(lambda k, j: k >= cand)
        return jnp.where(cnt >= n_sel, cand, t)

    thr = lax.fori_loop(0, nbits, bit_step, t0)
    cnt_ge = count(lambda k, j: k >= thr)
    need_tie = jnp.logical_and(cnt_ge > n_sel, thr > INT_MIN)
    any_tie = jnp.max(jnp.where(need_tie, 1.0, 0.0)) > 0.0

    def write_bias(sel_of):
        def wr(j, _):
            key_ref[j] = pltpu.bitcast(jnp.where(sel_of(key_ref[j], j), 0.0, NEG_INF), I32)
            return 0
        lax.fori_loop(0, nvis, wr, 0)

    @pl.when(jnp.logical_not(any_tie))
    def _():
        tsel = jnp.maximum(thr, INT_MIN + 1)
        write_bias(lambda k, j: k >= tsel)

    @pl.when(any_tie)
    def _():
        cnt_gt = count(lambda k, j: k > thr)

        def idx_step(it, x):
            cand = x + lax.shift_left(jnp.int32(1), 14 - it)
            g = cnt_gt + count(lambda k, j: jnp.logical_and(k == thr, kpos(j) < cand))
            return jnp.where(g < n_sel, cand, x)

        xj = lax.fori_loop(0, 15, idx_step, jnp.zeros((1, tq), I32))
        jmax = jnp.where(need_tie, xj, jnp.where(thr == INT_MIN, -1, 2 ** 30))
        write_bias(lambda k, j: jnp.logical_or(k > thr, jnp.logical_and(k == thr, kpos(j) <= jmax)))

    qts = [_slot_query(qt_ref, hl, 2) for hl in range(DSA_HEADS)]

    def qk(j):
        off = _block_offset(j)
        bias = pltpu.bitcast(key_ref[j], F32)
        return tuple(jnp.dot(k_ref[pl.ds(off, tkb), hl // 2 * LANE:(hl // 2 + 1) * LANE], qts[hl],
                             preferred_element_type=F32) + bias for hl in range(DSA_HEADS))

    def values(j):
        off = _block_offset(j)
        return [vt_ref[pl.ds(off, tkb), hl * DSA_HEAD_DIM:(hl + 1) * DSA_HEAD_DIM] for hl in range(DSA_HEADS)]

    carries = _attend_blocks(nvis, qk, values,
                             tuple(_softmax_init(DSA_HEAD_DIM, tq) for _ in range(DSA_HEADS)), None)
    for hl, (_, l, acc) in enumerate(carries):
        o_ref[hl * DSA_HEAD_DIM:(hl + 1) * DSA_HEAD_DIM, :] = (acc / l).astype(o_ref.dtype)


def _dsa(qit, wit, ki, qt, k, vt, *, s_pad, tq, causal, s_valid, n_sel, name="dsa"):
    B, H, _, Tq = qt.shape
    nkb = s_pad // KV_BLOCK
    S_pad = s_pad
    assert not causal or (tq <= KV_BLOCK and KV_BLOCK % tq == 0)
    assert KV_BLOCK >= n_sel
    W = DSA_HEADS * DSA_HEAD_DIM
    kern = functools.partial(_dsa_kernel, tq=tq, nkb=nkb, causal=causal, s_valid=s_valid, n_sel=n_sel)
    once = pl.Buffered(1)
    return pl.pallas_call(
        kern, grid=(B, Tq // tq),
        in_specs=[pl.BlockSpec((None, IDX_HEADS, LANE, tq), lambda b, i: (b, 0, 0, i)),
                  pl.BlockSpec((None, IDX_HEADS, tq), lambda b, i: (b, 0, i)),
                  pl.BlockSpec((None, S_pad, LANE), lambda b, i: (b, 0, 0), pipeline_mode=once),
                  pl.BlockSpec((None, H, LANE, tq), lambda b, i: (b, 0, 0, i)),
                  pl.BlockSpec((None, S_pad, W), lambda b, i: (b, 0, 0), pipeline_mode=once),
                  pl.BlockSpec((None, S_pad, W), lambda b, i: (b, 0, 0), pipeline_mode=once)],
        out_specs=pl.BlockSpec((None, W, tq), lambda b, i: (b, 0, i)),
        out_shape=jax.ShapeDtypeStruct((B, W, Tq), BF16),
        scratch_shapes=[pltpu.VMEM((nkb, KV_BLOCK, tq), I32)],
        compiler_params=_cp("parallel", "parallel"), name=name)(qit, wit, ki, qt, k, vt)


def _merge_kernel(h_ref, oa_ref, ob_ref, oc_ref, od_ref, wg0, wg1, wg2, wg3, bg0, bg1, bg2, bg3, wb_ref, o_ref,
                  wgb0, wgb1, wgb2, wgb3, wbb_ref):
    _stage_weights([wg0, wg1, wg2, wg3, wb_ref], [wgb0, wgb1, wgb2, wgb3, wbb_ref])
    h = h_ref[...]
    acc = None
    for n, (o_r, wg, bg) in enumerate(((oa_ref, wgb0, bg0), (ob_ref, wgb1, bg1), (oc_ref, wgb2, bg2),
                                       (od_ref, wgb3, bg3))):
        gate = jax.nn.sigmoid(jnp.dot(h, wg[...], preferred_element_type=F32) + bg[...])
        term = gate * jnp.dot(o_r[...], wbb_ref[n], preferred_element_type=F32)
        acc = term if acc is None else acc + term
    o_ref[...] = acc.astype(BF16)


def _merge(h, branches, w_gate, b_gate, w_branch, tn=512):
    T, D = h.shape
    nd = D // tn
    b_gate = b_gate.reshape(1, -1)
    once = pl.Buffered(1)
    specs = [pl.BlockSpec((ROW_TILE, D), lambda j, i: (i, 0))]
    specs += [pl.BlockSpec((ROW_TILE, BRANCH_WIDTH), lambda j, i: (i, 0))] * N_BRANCH
    specs += [pl.BlockSpec((D, tn), lambda j, i, n=n: (0, n * nd + j), pipeline_mode=once) for n in range(N_BRANCH)]
    specs += [pl.BlockSpec((1, tn), lambda j, i, n=n: (0, n * nd + j)) for n in range(N_BRANCH)]
    specs += [pl.BlockSpec((N_BRANCH, BRANCH_WIDTH, tn), lambda j, i: (0, 0, j), pipeline_mode=once)]
    return pl.pallas_call(
        _merge_kernel, grid=(nd, T // ROW_TILE), in_specs=specs,
        out_specs=pl.BlockSpec((ROW_TILE, tn), lambda j, i: (i, j)),
        out_shape=jax.ShapeDtypeStruct((T, D), BF16),
        scratch_shapes=[pltpu.VMEM((D, tn), BF16)] * N_BRANCH + [pltpu.VMEM((N_BRANCH, BRANCH_WIDTH, tn), BF16)],
        compiler_params=_cp("arbitrary", "arbitrary"), name="merge")(
            h, *branches, w_gate, w_gate, w_gate, w_gate, b_gate, b_gate, b_gate, b_gate, w_branch)


def _ffn_up_kernel(h_ref, w1_ref, w3_ref, o_ref, w1b_ref, w3b_ref):
    _stage_weights([w1_ref, w3_ref], [w1b_ref, w3b_ref])
    h = h_ref[...]
    a = jnp.dot(h, w1b_ref[...], preferred_element_type=F32)
    b = jnp.dot(h, w3b_ref[...], preferred_element_type=F32)
    o_ref[...] = (a * jax.nn.sigmoid(a) * b).astype(BF16)


def _moe_up_kernel(h_ref, w1_ref, w3_ref, gate_ref, o_ref, w1b_ref, w3b_ref):
    _stage_weights([w1_ref, w3_ref], [w1b_ref, w3b_ref])
    e = pl.program_id(0)
    h = h_ref[...]
    a = jnp.dot(h, w1b_ref[...], preferred_element_type=F32)
    b = jnp.dot(h, w3b_ref[...], preferred_element_type=F32)
    gate = gate_ref[...]
    lane = lax.broadcasted_iota(I32, gate.shape, 1)
    ge = jnp.sum(jnp.where(lane == e, gate, 0.0), axis=-1, keepdims=True)
    o_ref[...] = (a * jax.nn.sigmoid(a) * b * ge).astype(BF16)


def _ffn_up(h, w1, w3, tn=1408):
    T, D = h.shape
    F = w1.shape[1]
    once = pl.Buffered(1)
    return pl.pallas_call(
        _ffn_up_kernel, grid=(F // tn, T // ROW_TILE),
        in_specs=[pl.BlockSpec((ROW_TILE, D), lambda j, i: (i, 0)),
                  pl.BlockSpec((D, tn), lambda j, i: (0, j), pipeline_mode=once),
                  pl.BlockSpec((D, tn), lambda j, i: (0, j), pipeline_mode=once)],
        out_specs=pl.BlockSpec((ROW_TILE, tn), lambda j, i: (i, j)),
        out_shape=jax.ShapeDtypeStruct((T, F), BF16),
        scratch_shapes=[pltpu.VMEM((D, tn), BF16)] * 2,
        compiler_params=_cp("arbitrary", "arbitrary"), name="ffn_up")(h, w1, w3)


def _moe_up(h, w1, w3, gate):
    T, D = h.shape
    E, _, F = w1.shape
    once = pl.Buffered(1)
    return pl.pallas_call(
        _moe_up_kernel, grid=(E, T // ROW_TILE),
        in_specs=[pl.BlockSpec((ROW_TILE, D), lambda e, i: (i, 0)),
                  pl.BlockSpec((None, D, F), lambda e, i: (e, 0, 0), pipeline_mode=once),
                  pl.BlockSpec((None, D, F), lambda e, i: (e, 0, 0), pipeline_mode=once),
                  pl.BlockSpec((ROW_TILE, LANE), lambda e, i: (i, 0))],
        out_specs=pl.BlockSpec((ROW_TILE, F), lambda e, i: (i, e)),
        out_shape=jax.ShapeDtypeStruct((T, E * F), BF16),
        scratch_shapes=[pltpu.VMEM((D, F), BF16)] * 2,
        compiler_params=_cp("arbitrary", "arbitrary"), name="moe_up")(h, w1, w3, gate)


def _final_norm_kernel(x_ref, g_ref, o_ref):
    o_ref[...] = _rms(x_ref[...], g_ref[...])


def _final_norm(x, g):
    T, D = x.shape
    return pl.pallas_call(
        _final_norm_kernel, grid=(T // ROW_TILE,),
        in_specs=[pl.BlockSpec((ROW_TILE, D), lambda i: (i, 0)), pl.BlockSpec((1, D), lambda i: (0, 0))],
        out_specs=pl.BlockSpec((ROW_TILE, D), lambda i: (i, 0)),
        out_shape=jax.ShapeDtypeStruct((T, D), F32),
        compiler_params=_cp("parallel"), name="final_norm")(x, g.reshape(1, D))


def _pad_rows(a, s_pad):
    pad = s_pad - a.shape[1]
    if pad == 0:
        return a
    return jnp.pad(a, ((0, 0), (0, pad)) + ((0, 0),) * (a.ndim - 2))


def _queries_t(q, n_slots, tq_pad):
    B, T, W = q.shape
    d = W // n_slots
    qt = q.reshape(B, T, n_slots, d).transpose(0, 2, 3, 1)
    if d < LANE:
        qt = jnp.pad(qt, ((0, 0), (0, 0), (0, LANE - d), (0, 0)))
    reps = -(-tq_pad // T)
    return jnp.concatenate([qt] * reps, axis=-1)[..., :tq_pad] if reps > 1 else qt


def _slot_query(qt_ref, hl, heads_per_slot):
    if heads_per_slot == 1:
        return qt_ref[hl]
    qp = qt_ref[hl // heads_per_slot]
    d = LANE // heads_per_slot
    r = hl % heads_per_slot
    parts = [qp[s * d:(s + 1) * d] if s == r else jnp.zeros((d, qp.shape[1]), qp.dtype)
             for s in range(heads_per_slot)]
    return jnp.concatenate(parts, axis=0)


def _rows_out(ot, T):
    B, W, _ = ot.shape
    return ot[:, :, :T].transpose(0, 2, 1).reshape(B * T, W)


def _pad_in_weight(w_in):
    D = w_in.shape[0]
    out = jnp.zeros((D, IN_PAD), BF16)
    for name in _ORDER:
        s0, sw = _SRC[name]
        d0, _ = _DST[name]
        out = lax.dynamic_update_slice(out, w_in[:, s0:s0 + sw].astype(BF16), (0, d0))
    return out


def _col(hw, name, width=None):
    off, w = _DST[name]
    return hw[:, off:off + (width or w)]


def _mixers(l, hw, q16, k16, caches, lp, geom):
    n_p, B_p, T_p, B_s, T_s, past = geom
    q_nope, q_pe, dsa_q, qi, diff_q = q16
    ckv_n, kpe, dsa_k, dsa_v, ki, diff_k, diff_v = k16
    wi = _col(hw, 'wi')
    T_all = hw.shape[0]

    def split(a):
        W = a.shape[1]
        return a[:n_p].reshape(B_p, T_p, W), a[n_p:].reshape(B_s, T_s, W)

    o_a_p, v_p = _gmlp(hw, 0, n_p, min(T_p, GM_CHUNK), lp['gm_ln_g'], lp['gm_ln_b'], lp['gm_ws'], lp['gm_bs'])
    o_a_s, v_s = _gmlp(hw, n_p, hw.shape[0] - n_p, min(T_s, GM_CHUNK), lp['gm_ln_g'], lp['gm_ln_b'],
                       lp['gm_ws'], lp['gm_bs'])
    c_p, c_s = min(T_p, GM_CHUNK), min(T_s, GM_CHUNK)
    gm_v_p = v_p.reshape(B_p, T_p, GM_WIDTH)[:, T_p - c_p:]
    gm_v_s = v_s.reshape(B_s, T_s, GM_WIDTH)[:, T_s - c_s:]

    outs = {}
    for grp in ('p', 's'):
        pi = 0 if grp == 'p' else 1
        B, T = (B_p, T_p) if grp == 'p' else (B_s, T_s)

        def g(a, pi=pi):
            return split(a)[pi]

        if grp == 'p':
            S = s_pad = T
            tq = min(Q_TILE, T)
            tq_pad = T
            causal = True
            direct = B == 1

            def ext(new16, cache, direct=direct):
                return new16.reshape(1, T_all, -1) if direct else g(new16)
        else:
            S = past + T
            s_pad = -(-S // KV_BLOCK) * KV_BLOCK
            tq = tq_pad = -(-T // LANE) * LANE
            causal = False

            def ext(new16, cache, s_pad=s_pad):
                old = cache.reshape(cache.shape[:2] + (-1,)).astype(BF16)
                new = g(new16)[..., :old.shape[-1]]
                return _pad_rows(jnp.concatenate([old, new], axis=1), s_pad)

        cm_ckv, cm_kr, cd_k, cd_v, cd_ki, cf_k, cf_v = caches

        all_c = ext(ckv_n, cm_ckv)
        all_pe = ext(kpe, cm_kr)
        if all_pe.shape[-1] < LANE:
            all_pe = jnp.pad(all_pe, ((0, 0), (0, 0), (0, LANE - all_pe.shape[-1])))
        rows = all_c.shape[1]
        k_mla, v_mla = _mla_kv(all_c.reshape(B * rows, MLA_KV_LORA), all_pe.reshape(B * rows, LANE),
                               lp['w_uk_slots'], lp['w_uv'])
        q_mla = jnp.concatenate([g(q_nope).reshape(B, T, MLA_HEADS, MLA_NOPE),
                                 g(q_pe).reshape(B, T, MLA_HEADS, MLA_ROPE)], axis=-1)
        o_b = _flash(_queries_t(q_mla.reshape(B, T, -1), MLA_HEADS, tq_pad),
                     k_mla.reshape(B, rows, MLA_HEADS * LANE), v_mla.reshape(B, rows, MLA_HEADS * MLA_V),
                     s_pad=s_pad, v_block0=0,
                     tq=tq, causal=causal, s_valid=S, dv=MLA_V, k_width=MLA_HEADS * LANE,
                     k_lanes=tuple(h * LANE for h in range(MLA_HEADS)), name="mla_attn")

        ki_all = ext(ki, cd_ki)
        if ki_all.shape[-1] < LANE:
            ki_all = jnp.pad(ki_all, ((0, 0), (0, 0), (0, LANE - ki_all.shape[-1])))
        wit = _queries_t(g(wi)[..., :IDX_HEADS], IDX_HEADS, tq_pad)[:, :, 0, :]
        o_c = _dsa(_queries_t(g(qi), IDX_HEADS, tq_pad), wit, ki_all,
                   _queries_t(g(dsa_q), DSA_HEADS // 2, tq_pad), ext(dsa_k, cd_k), ext(dsa_v, cd_v),
                   s_pad=s_pad, tq=tq, causal=causal, s_valid=S, n_sel=min(DSA_TOPK, S // 4))

        o_d = _flash(_queries_t(g(diff_q), DIFF_HEADS, tq_pad), ext(diff_k, cf_k), ext(diff_v, cf_v),
                     s_pad=s_pad, v_block0=0,
                     tq=tq, causal=causal, s_valid=S, dv=DIFF_V_DIM, k_width=MXU_WIDTH,
                     k_lanes=(0, 0, LANE, LANE),
                     diff=(lp['diff_lambda'], lp['diff_subln_g'], l), name="diff_attn")
        outs[grp] = (_rows_out(o_b, T), _rows_out(o_c, T), _rows_out(o_d, T))

    o_a = jnp.concatenate([o_a_p, o_a_s], axis=0)
    o_b, o_c, o_d = (jnp.concatenate([outs['p'][k], outs['s'][k]], axis=0) for k in range(3))
    return (o_a, o_b, o_c, o_d), gm_v_p, gm_v_s


def kernel(x_prompt, x_sample, c_prompt, c_sample, cache_mla_ckv, cache_mla_krope, cache_dsa_k, cache_dsa_v, cache_dsa_kidx, cache_diff_k, cache_diff_v, ada_w, ada_b, norm_mix_g, norm_ff_g, w_in, w_gate, b_gate, gm_ln_g, gm_ln_b, gm_ws, gm_bs, mla_q_norm_g, mla_kv_norm_g, mla_w_uq, mla_w_ukv, diff_lambda, diff_subln_g, w_branch, w_out, ffn_w1, ffn_w3, ffn_w2, moe_router_w, moe_router_b, moe_w1, moe_w3, moe_w2, final_norm_g):
    B_p, T_p, D = x_prompt.shape
    B_s, T_s, _ = x_sample.shape
    depth = ada_w.shape[0]
    past = cache_mla_ckv.shape[2]
    n_p, n_s = B_p * T_p, B_s * T_s
    assert T_p % ROW_TILE == 0 and n_s % ROW_TILE == 0 and ROW_TILE % T_s == 0
    assert past % CHUNK == 0 and T_s <= CHUNK
    n_p_tiles, tiles_per_b = n_p // ROW_TILE, T_p // ROW_TILE
    geom = (n_p, B_p, T_p, B_s, T_s, past)

    def mod_map(i):
        return jnp.where(i < n_p_tiles, i // tiles_per_b, B_p + i - n_p_tiles)

    x = jnp.concatenate([x_prompt.reshape(n_p, D), x_sample.reshape(n_s, D)], axis=0)

    n_c = B_p + B_s
    c_rows = jnp.concatenate([c_prompt, c_sample, jnp.zeros((-n_c % 8, D), F32)], axis=0)
    mod = _ada(c_rows, ada_w, ada_b)
    mod = mod[:, :n_c].reshape(depth, n_c, 6, D).transpose(0, 2, 1, 3)
    mod_p = jnp.broadcast_to(mod[:, :, :B_p, None, :], (depth, 6, B_p, ROW_TILE, D))
    mod_s = jnp.broadcast_to(mod[:, :, B_p:, None, :], (depth, 6, B_s, T_s, D)).reshape(
        depth, 6, n_s // ROW_TILE, ROW_TILE, D)
    modexp = jnp.concatenate([mod_p, mod_s], axis=2)

    pos = jnp.concatenate([jnp.tile(jnp.arange(T_p, dtype=I32), B_p),
                           jnp.tile(past + jnp.arange(T_s, dtype=I32), B_s)])
    t64 = _rope_tables(pos, DSA_ROT, DSA_HEAD_DIM)
    t32 = _rope_tables(pos, MLA_ROPE, MLA_ROPE)
    tidx = _rope_tables(pos, IDX_ROT, IDX_DIM)

    state = ([jnp.zeros((depth, n_p, w), F32) for w in _STATE_WIDTHS]
             + [jnp.zeros((depth, n_s, w), F32) for w in _STATE_WIDTHS])
    gm_p, gm_s = [], []
    for l in range(depth):
        uq = mla_w_uq[l].reshape(MLA_Q_LORA, MLA_HEADS, MLA_NOPE + MLA_ROPE)
        ukv = mla_w_ukv[l].reshape(MLA_KV_LORA, MLA_HEADS, MLA_NOPE + MLA_V)
        lp = {
            'gm_ln_g': gm_ln_g[l], 'gm_ln_b': gm_ln_b[l], 'gm_ws': gm_ws[l], 'gm_bs': gm_bs[l],
            'w_uk_slots': jnp.pad(ukv[..., :MLA_NOPE], ((0, 0), (0, 0), (0, LANE - MLA_NOPE))).reshape(
                MLA_KV_LORA, MLA_HEADS * LANE).astype(BF16),
            'w_uv': ukv[..., MLA_NOPE:].reshape(MLA_KV_LORA, -1).astype(BF16),
            'diff_lambda': diff_lambda[l], 'diff_subln_g': diff_subln_g[l],
        }
        h = _norm_mod(x, norm_mix_g[l], modexp[l], mod_map, 1, 0)
        hw = _mm(h, _pad_in_weight(w_in[l]), F32, tn=IN_PAD // 2, name="in_proj")
        state, q16, k16 = _state(l, depth, n_p, state, hw, mla_q_norm_g[l], mla_kv_norm_g[l],
                                 uq[..., :MLA_NOPE].reshape(MLA_Q_LORA, -1).astype(BF16),
                                 uq[..., MLA_NOPE:].reshape(MLA_Q_LORA, -1).astype(BF16), t64, t32, tidx)
        caches = (cache_mla_ckv[l], cache_mla_krope[l], cache_dsa_k[l], cache_dsa_v[l], cache_dsa_kidx[l],
                  cache_diff_k[l], cache_diff_v[l])
        branches, gm_v_p, gm_v_s = _mixers(l, hw, q16, k16, caches, lp, geom)
        gm_p.append(gm_v_p)
        gm_s.append(gm_v_s)
        merged = _merge(h, branches, w_gate[l], b_gate[l], w_branch[l])
        x = _mm_res(merged, w_out[l], x, modexp[l], mod_map, 2, tn=1024, name="out_proj")

        j = l // 2
        if l % 2 == 0:
            h2 = _norm_mod(x, norm_ff_g[l], modexp[l], mod_map, 4, 3)
            act = _ffn_up(h2, ffn_w1[j], ffn_w3[j])
            w2 = ffn_w2[j]
        else:
            rw = jnp.zeros((D, LANE), F32).at[:, :N_EXPERTS].set(moe_router_w[j])
            rw_hi = rw.astype(BF16)
            rw_lo = (rw - rw_hi.astype(F32)).astype(BF16)
            rb = jnp.zeros((1, LANE), F32).at[0, :N_EXPERTS].set(moe_router_b[j])
            h2, gate = _norm_mod(x, norm_ff_g[l], modexp[l], mod_map, 4, 3, router=(rw_hi, rw_lo, rb))
            act = _moe_up(h2, moe_w1[j], moe_w3[j], gate)
            w2 = moe_w2[j].astype(BF16).reshape(-1, D)
        x = _mm_res(act, w2, x, modexp[l], mod_map, 5, tn=512 if w2.dtype == F32 else MXU_WIDTH, name="ffn_down")

    y = _final_norm(x, final_norm_g)
    outs = [y[:n_p].reshape(B_p, T_p, D), y[n_p:].reshape(B_s, T_s, D),
            jnp.stack(gm_p, axis=0), jnp.stack(gm_s, axis=0)]
    tails = ((), (), (DSA_HEADS, DSA_HEAD_DIM), (DSA_HEADS, DSA_HEAD_DIM), (), (DIFF_HEADS, 2, DIFF_QK_DIM),
             (DIFF_HEADS, DIFF_V_DIM))
    for k, tail in enumerate(tails):
        w = _STATE_WIDTHS[k]
        outs.append(state[k].reshape((depth, B_p, T_p) + (tail or (w,))))
        outs.append(state[N_STATE + k].reshape((depth, B_s, T_s) + (tail or (w,))))
    return tuple(outs)
```
